```python
import math, functools
import jax, jax.numpy as jnp
from jax import lax
import numpy as np

D_MODEL = 1024
BATCH = 8
SEQ = 2048
DEPTH = 2
DEC_BATCH = 32
DEC_SEQ = 8
PAST_LEN = 16384
PAGE_SIZE = 128

F32 = jnp.float32
EPS = 1e-6
N_EVEN = (DEPTH + 1) // 2
N_ODD = DEPTH // 2
D_FF = 4 * D_MODEL

POOL_WINDOWS = (2, 4, 8, 16)
POOL_GROUPS = len(POOL_WINDOWS)
POOL_GROUP_DIM = D_MODEL // 16
POOL_DIM = POOL_GROUPS * POOL_GROUP_DIM
POOL_HIST = max(POOL_WINDOWS) - 1

FOX_HEAD_DIM = 64
FOX_HEADS = (D_MODEL - POOL_DIM) // FOX_HEAD_DIM
FOX_DIM = FOX_HEADS * FOX_HEAD_DIM
Q_BLOCK = 128
FORGET_BIAS = 3.0
CACHE_FORGET_BIAS = 9.0

EV_Q = POOL_DIM
EV_K = EV_Q + FOX_DIM
EV_V = EV_K + FOX_DIM
EV_F = EV_V + FOX_DIM
EV_TOTAL = EV_F + FOX_HEADS

GMLP_CHUNK = 128
GMLP_GROUPS = 4
GMLP_DIM = D_MODEL // 2
GMLP_GROUP_DIM = GMLP_DIM // GMLP_GROUPS

RET_HEADS = 4
RET_DIM = D_MODEL - GMLP_DIM
RET_HEAD_DIM = RET_DIM // RET_HEADS
RET_CHUNK = 128
ROPE_BASE = 10000.0

OD_V = GMLP_DIM
OD_Q = 2 * GMLP_DIM
OD_K = OD_Q + RET_DIM
OD_RV = OD_K + RET_DIM
OD_G = OD_RV + RET_DIM
OD_TOTAL = OD_G + RET_DIM

kernel_name = 'hybrid_pool_fox_gmlp_retention_step'


def rmsnorm(x, g):
    xf = x.astype(F32)
    y = xf * lax.rsqrt(jnp.mean(xf * xf, axis=-1, keepdims=True) + EPS)
    return (y * g.astype(F32)).astype(x.dtype)


def squared_relu_mlp(x, w_up, w_down):
    h = jax.nn.relu(x @ w_up)
    return (h * h) @ w_down


def pool_mix(a, hist, start_pos, pool_w, pool_scale):
    b, t, _ = a.shape
    af = a.astype(F32)
    ext = jnp.concatenate([hist.astype(F32), af], axis=1)
    cs = jnp.concatenate([jnp.zeros((b, 1, POOL_DIM), F32), jnp.cumsum(ext, axis=1)], axis=1)
    end = cs[:, POOL_HIST + 1:]
    pos = start_pos + jnp.arange(t)
    parts = []
    for g, w in enumerate(POOL_WINDOWS):
        sl = slice(g * POOL_GROUP_DIM, (g + 1) * POOL_GROUP_DIM)
        win = end[:, :, sl] - cs[:, POOL_HIST + 1 - w:POOL_HIST + 1 - w + t, sl]
        cnt = jnp.minimum(pos + 1, w).astype(F32)
        parts.append(win / cnt[None, :, None])
    d = (jnp.concatenate(parts, axis=-1) - af).reshape(b, t, POOL_GROUPS, POOL_GROUP_DIM)
    y = jnp.einsum('btgc,gcd->btgd', d, pool_w.astype(F32)).reshape(b, t, POOL_DIM) * pool_scale.astype(F32)
    return y.astype(a.dtype), ext[:, -POOL_HIST:].astype(a.dtype)


def fox_prompt(q, k, v, logf):
    b, s, h, d = q.shape
    scale = d ** -0.5
    cum = jnp.cumsum(logf.astype(F32), axis=1).transpose(0, 2, 1)
    kpos = jnp.arange(s)

    def block(n):
        start = n * Q_BLOCK
        qb = lax.dynamic_slice_in_dim(q, start, Q_BLOCK, axis=1)
        cb = lax.dynamic_slice_in_dim(cum, start, Q_BLOCK, axis=2)
        sc = jnp.einsum('bqhd,bkhd->bhqk', qb, k).astype(F32) * scale
        sc = sc + cb[..., :, None] - cum[..., None, :]
        qpos = start + jnp.arange(Q_BLOCK)
        sc = jnp.where(kpos[None, :] <= qpos[:, None], sc, -jnp.inf)
        p = jax.nn.softmax(sc, axis=-1)
        return jnp.einsum('bhqk,bkhd->bqhd', p.astype(v.dtype), v)

    o = lax.map(block, jnp.arange(s // Q_BLOCK))
    return o.transpose(1, 0, 2, 3, 4).reshape(b, s, h, d)


def fox_sample(k_past, v_past, logf_past, q, k, v, logf):
    t, d = q.shape[1], q.shape[-1]
    n_past = k_past.shape[1]
    scale = d ** -0.5
    lp = logf_past.astype(F32)
    suffix = (lax.cumsum(lp, axis=1, reverse=True) - lp).transpose(0, 2, 1)
    g = jnp.cumsum(logf.astype(F32), axis=1).transpose(0, 2, 1)
    s_past = jnp.einsum('bqhd,bkhd->bhqk', q, k_past).astype(F32) * scale + g[..., :, None] + suffix[..., None, :]
    s_new = jnp.einsum('bqhd,bkhd->bhqk', q, k).astype(F32) * scale + g[..., :, None] - g[..., None, :]
    causal = jnp.arange(t)[None, :] <= jnp.arange(t)[:, None]
    s_new = jnp.where(causal, s_new, -jnp.inf)
    p = jax.nn.softmax(jnp.concatenate([s_past, s_new], axis=-1), axis=-1).astype(v.dtype)
    return (jnp.einsum('bhqk,bkhd->bqhd', p[..., :n_past], v_past)
            + jnp.einsum('bhqk,bkhd->bqhd', p[..., n_past:], v))


def gmlp_mix(u, v, ln_g, ln_b, ws, bs):
    b, t, _ = u.shape
    vf = v.astype(F32)
    mu = jnp.mean(vf, axis=-1, keepdims=True)
    var = jnp.mean((vf - mu) ** 2, axis=-1, keepdims=True)
    vn = (vf - mu) * lax.rsqrt(var + EPS) * ln_g.astype(F32) + ln_b.astype(F32)
    tp = -(-t // GMLP_CHUNK) * GMLP_CHUNK
    vp = jnp.pad(vn, ((0, 0), (0, tp - t), (0, 0))).reshape(b, tp // GMLP_CHUNK, GMLP_CHUNK, GMLP_GROUPS, GMLP_GROUP_DIM)
    ws_c = ws.astype(F32) * jnp.tril(jnp.ones((GMLP_CHUNK, GMLP_CHUNK), F32))
    z = jnp.einsum('gts,bnsgc->bntgc', ws_c, vp) + bs.astype(F32).T[None, None, :, :, None]
    z = z.reshape(b, tp, GMLP_DIM)[:, :t]
    return (u.astype(F32) * z).astype(u.dtype), vn.astype(v.dtype)


def rotary(x, pos):
    half = x.shape[-1] // 2
    inv = ROPE_BASE ** (-jnp.arange(half, dtype=F32) / half)
    ang = pos[:, None] * inv[None, :]
    cos, sin = jnp.cos(ang)[None, :, None, :], jnp.sin(ang)[None, :, None, :]
    x1, x2 = x[..., :half], x[..., half:]
    return jnp.concatenate([x1 * cos - x2 * sin, x1 * sin + x2 * cos], axis=-1)


def retention(q, k, v, s0, start_pos):
    b, t, h, dk = q.shape
    pos = (start_pos + jnp.arange(t)).astype(F32)
    qf = rotary(q.astype(F32), pos)
    kf = rotary(k.astype(F32), pos) * dk ** -0.5
    vf = v.astype(F32)
    lc = math.gcd(t, RET_CHUNK) if t >= RET_CHUNK else t
    n = t // lc
    log_g = jnp.log1p(-jnp.exp2(-5.0 - jnp.arange(h, dtype=F32)))
    idx = jnp.arange(lc, dtype=F32)
    diff = idx[:, None] - idx[None, :]
    decay_intra = jnp.where(diff[None] >= 0, jnp.exp(diff[None] * log_g[:, None, None]), 0.0)
    decay_q = jnp.exp((idx[:, None] + 1.0) * log_g[None, :])
    decay_k = jnp.exp((lc - 1.0 - idx)[:, None] * log_g[None, :])
    decay_s = jnp.exp(lc * log_g)

    def chunk(state, inp):
        qc, kc, vc = inp
        sc = jnp.einsum('blhd,bmhd->bhlm', qc, kc) * decay_intra[None]
        o = (jnp.einsum('bhlm,bmhe->blhe', sc, vc)
             + jnp.einsum('blhd,bhde->blhe', qc, state) * decay_q[None, :, :, None])
        state = state * decay_s[None, :, None, None] + jnp.einsum('bmhd,bmhe->bhde', kc * decay_k[None, :, :, None], vc)
        return state, o

    xs = tuple(a.reshape(b, n, lc, h, a.shape[-1]).swapaxes(0, 1) for a in (qf, kf, vf))
    s_final, o = lax.scan(chunk, s0.astype(F32), xs)
    return o.swapaxes(0, 1).reshape(b, t, h, vf.shape[-1]), s_final


def head_norm(o, g):
    mu = jnp.mean(o, axis=-1, keepdims=True)
    var = jnp.mean((o - mu) ** 2, axis=-1, keepdims=True)
    return (o - mu) * lax.rsqrt(var + EPS) * g.astype(F32).reshape(o.shape[2], o.shape[3])


def even_token_mix(xn, w_in, b_f, pool_w, pool_scale, w_out, hist, start_pos, attend):
    b, t, _ = xn.shape
    proj = xn @ w_in
    a = proj[..., :EV_Q]
    q = proj[..., EV_Q:EV_K].reshape(b, t, FOX_HEADS, FOX_HEAD_DIM)
    k = proj[..., EV_K:EV_V].reshape(b, t, FOX_HEADS, FOX_HEAD_DIM)
    v = proj[..., EV_V:EV_F].reshape(b, t, FOX_HEADS, FOX_HEAD_DIM)
    logf = jax.nn.log_sigmoid((proj[..., EV_F:EV_TOTAL] + b_f).astype(F32))
    pool_out, new_hist = pool_mix(a, hist, start_pos, pool_w, pool_scale)
    att = attend(q, k, v, logf).reshape(b, t, FOX_DIM)
    out = jnp.concatenate([pool_out, att.astype(pool_out.dtype)], axis=-1) @ w_out
    return out, new_hist, k, v, logf.astype(xn.dtype)


def odd_token_mix(xn, w_in, ln_g, ln_b, ws, bs, gn_g, w_out, s0, start_pos):
    b, t, _ = xn.shape
    proj = xn @ w_in
    u = proj[..., :OD_V]
    gv = proj[..., OD_V:OD_Q]
    rq = proj[..., OD_Q:OD_K].reshape(b, t, RET_HEADS, RET_HEAD_DIM)
    rk = proj[..., OD_K:OD_RV].reshape(b, t, RET_HEADS, RET_HEAD_DIM)
    rv = proj[..., OD_RV:OD_G].reshape(b, t, RET_HEADS, RET_HEAD_DIM)
    rg = proj[..., OD_G:OD_TOTAL]
    c_out, vn = gmlp_mix(u, gv, ln_g, ln_b, ws, bs)
    r, s_new = retention(rq, rk, rv, s0, start_pos)
    r = head_norm(r, gn_g).reshape(b, t, RET_DIM) * jax.nn.silu(rg.astype(F32))
    out = jnp.concatenate([c_out, r.astype(c_out.dtype)], axis=-1) @ w_out
    return out, vn, s_new


def setup_inputs(seed: int = 0) -> dict:
    key = jax.random.key(seed)
    ks = jax.random.split(key, 32)
    n_pages = PAST_LEN // PAGE_SIZE
    n_used = DEC_BATCH * n_pages
    n_phys = n_used + n_used // 4

    def nrm(i, shape, scale=1.0):
        return jax.random.normal(ks[i], shape, F32) * scale

    x_prompt = nrm(0, (BATCH, SEQ, D_MODEL))
    x_sample = nrm(1, (DEC_BATCH, DEC_SEQ, D_MODEL))
    state_pool = nrm(2, (N_EVEN, DEC_BATCH, POOL_HIST, POOL_DIM))
    cache_k = nrm(3, (N_EVEN, n_phys, PAGE_SIZE, FOX_HEADS, FOX_HEAD_DIM))
    cache_v = nrm(4, (N_EVEN, n_phys, PAGE_SIZE, FOX_HEADS, FOX_HEAD_DIM))
    cache_logf = jax.nn.log_sigmoid(CACHE_FORGET_BIAS + nrm(5, (N_EVEN, n_phys, PAGE_SIZE, FOX_HEADS), 0.5))
    page_table = jax.random.permutation(ks[6], n_phys)[:n_used].reshape(DEC_BATCH, n_pages).astype(jnp.int32)
    state_ret = nrm(7, (N_ODD, DEC_BATCH, RET_HEADS, RET_HEAD_DIM, RET_HEAD_DIM), 0.5)
    norm_mix = 1.0 + nrm(8, (DEPTH, D_MODEL), 0.1)
    w_in_even = nrm(9, (N_EVEN, D_MODEL, EV_TOTAL), D_MODEL ** -0.5)
    b_forget = FORGET_BIAS + nrm(10, (N_EVEN, FOX_HEADS), 0.5)
    pool_w = nrm(11, (N_EVEN, POOL_GROUPS, POOL_GROUP_DIM, POOL_GROUP_DIM), POOL_GROUP_DIM ** -0.5)
    pool_scale = 1.0 + nrm(12, (N_EVEN, POOL_DIM), 0.1)
    w_out_even = nrm(13, (N_EVEN, D_MODEL, D_MODEL), D_MODEL ** -0.5)
    w_in_odd = nrm(14, (N_ODD, D_MODEL, OD_TOTAL), D_MODEL ** -0.5)
    gmlp_ln_g = 1.0 + nrm(15, (N_ODD, GMLP_DIM), 0.1)
    gmlp_ln_b = nrm(16, (N_ODD, GMLP_DIM), 0.1)
    gmlp_ws = nrm(17, (N_ODD, GMLP_GROUPS, GMLP_CHUNK, GMLP_CHUNK), GMLP_CHUNK ** -0.5)
    gmlp_bs = 1.0 + nrm(18, (N_ODD, GMLP_GROUPS, GMLP_CHUNK), 0.1)
    ret_gn_g = 1.0 + nrm(19, (N_ODD, RET_DIM), 0.1)
    w_out_odd = nrm(20, (N_ODD, D_MODEL, D_MODEL), D_MODEL ** -0.5)
    norm_ffn = 1.0 + nrm(21, (DEPTH, D_MODEL), 0.1)
    w_up = nrm(22, (DEPTH, D_MODEL, D_FF), D_MODEL ** -0.5)
    w_down = nrm(23, (DEPTH, D_FF, D_MODEL), D_FF ** -0.5)
    norm_final = 1.0 + nrm(24, (D_MODEL,), 0.1)
    return {'x_prompt': x_prompt, 'x_sample': x_sample, 'state_pool': state_pool,
            'cache_k': cache_k, 'cache_v': cache_v, 'cache_logf': cache_logf,
            'page_table': page_table, 'state_ret': state_ret, 'norm_mix': norm_mix,
            'w_in_even': w_in_even, 'b_forget': b_forget, 'pool_w': pool_w,
            'pool_scale': pool_scale, 'w_out_even': w_out_even, 'w_in_odd': w_in_odd,
            'gmlp_ln_g': gmlp_ln_g, 'gmlp_ln_b': gmlp_ln_b, 'gmlp_ws': gmlp_ws,
            'gmlp_bs': gmlp_bs, 'ret_gn_g': ret_gn_g, 'w_out_odd': w_out_odd,
            'norm_ffn': norm_ffn, 'w_up': w_up, 'w_down': w_down, 'norm_final': norm_final}


def reference(x_prompt, x_sample, state_pool, cache_k, cache_v, cache_logf, page_table, state_ret,
              norm_mix, w_in_even, b_forget, pool_w, pool_scale, w_out_even, w_in_odd,
              gmlp_ln_g, gmlp_ln_b, gmlp_ws, gmlp_bs, ret_gn_g, w_out_odd,
              norm_ffn, w_up, w_down, norm_final):
    dec_b, n_pages = page_table.shape
    past = n_pages * cache_k.shape[2]

    def gather(pool, j):
        rows = pool[j][page_table]
        return rows.reshape((dec_b, past) + rows.shape[3:])

    def run(x, start_pos, get_hist, get_attend, get_state):
        h = x
        hists, ks, vs, lfs, gvs, rs = [], [], [], [], [], []
        for layer in range(DEPTH):
            j = layer // 2
            xn = rmsnorm(h, norm_mix[layer])
            if layer % 2 == 0:
                out, hist, k, v, lf = even_token_mix(xn, w_in_even[j], b_forget[j], pool_w[j], pool_scale[j],
                                                     w_out_even[j], get_hist(j), start_pos, get_attend(j))
                hists.append(hist)
                ks.append(k)
                vs.append(v)
                lfs.append(lf)
            else:
                out, vn, s_new = odd_token_mix(xn, w_in_odd[j], gmlp_ln_g[j], gmlp_ln_b[j], gmlp_ws[j],
                                               gmlp_bs[j], ret_gn_g[j], w_out_odd[j], get_state(j), start_pos)
                gvs.append(vn)
                rs.append(s_new)
            h = h + out
            h = h + squared_relu_mlp(rmsnorm(h, norm_ffn[layer]), w_up[layer], w_down[layer])
        return (rmsnorm(h, norm_final), jnp.stack(hists), jnp.stack(ks), jnp.stack(vs),
                jnp.stack(lfs), jnp.stack(gvs), jnp.stack(rs))

    b = x_prompt.shape[0]
    y_prompt, pool_p, k_p, v_p, lf_p, _, ret_p = run(
        x_prompt, 0,
        lambda j: jnp.zeros((b, POOL_HIST, POOL_DIM), x_prompt.dtype),
        lambda j: fox_prompt,
        lambda j: jnp.zeros((b, RET_HEADS, RET_HEAD_DIM, RET_HEAD_DIM), F32))
    y_sample, pool_s, k_s, v_s, lf_s, gv_s, ret_s = run(
        x_sample, past,
        lambda j: state_pool[j],
        lambda j: functools.partial(fox_sample, gather(cache_k, j), gather(cache_v, j), gather(cache_logf, j)),
        lambda j: state_ret[j])
    return (y_prompt, y_sample, pool_p, k_p, v_p, lf_p, ret_p, pool_s, k_s, v_s, lf_s, gv_s, ret_s)
```

```python
import functools
import math

import numpy as np
import jax
import jax.numpy as jnp
from jax import lax
from jax.experimental import pallas as pl
from jax.experimental.pallas import tpu as pltpu

F32 = jnp.float32
BF16 = jnp.bfloat16
EPS = 1e-6

POOL_WINDOWS = (2, 4, 8, 16)
POOL_GROUP_DIM = 64
POOL_DIM = 256
POOL_HIST = 15
FOX_HEAD_DIM = 64
FOX_HEADS = 12
FOX_DIM = FOX_HEADS * FOX_HEAD_DIM
GMLP_CHUNK = 128
GMLP_GROUPS = 4
GMLP_DIM = 512
RET_HEADS = 4
RET_DIM = 512
RET_HEAD_DIM = 128
RET_CHUNK = 128
ROPE_BASE = 10000.0

LANES = 128
SUBLANES = 8
HEAD_ROWS = 16
NEG_BIG = -1e30
VMEM_LIMIT = 56 * 1024 * 1024
ROW_TILE = 512
ATT_TILE = 512
PAGES_PER_STEP = 8
SUFFIX_ROW_TILE = 2048


def _row_tile(rows):
    return ROW_TILE if rows % ROW_TILE == 0 else rows


def _params(*sem):
    return pltpu.CompilerParams(dimension_semantics=sem, vmem_limit_bytes=VMEM_LIMIT)


def _const_spec(shape):
    nd = len(shape)
    return pl.BlockSpec(shape, lambda *_: (0,) * nd, pipeline_mode=pl.Buffered(1))


def _rms(x, g):
    return x * lax.rsqrt(jnp.mean(x * x, axis=-1, keepdims=True) + EPS) * g


def _bdot(a, b):
    return jnp.dot(a, b, preferred_element_type=F32)


def _bdot_nt(a, b):
    return lax.dot_general(a, b, (((1,), (1,)), ((), ())), preferred_element_type=F32)


def _bdot_tn(a, b):
    return lax.dot_general(a, b, (((0,), (0,)), ((), ())), preferred_element_type=F32)


def _split3(x):
    hi = x.astype(BF16)
    r = x - hi.astype(F32)
    mid = r.astype(BF16)
    lo = (r - mid.astype(F32)).astype(BF16)
    return hi, mid, lo


def _log_sigmoid(x):
    return jnp.minimum(x, 0.0) - jnp.log1p(jnp.exp(-jnp.abs(x)))


def _in_even_kernel(x_ref, g_ref, w_ref, bf_ref, tri_ref,
                    a_ref, q_ref, k_ref, v_ref, kb_ref, vb_ref, lf_ref, cum_ref, cumt_ref,
                    carry_ref, *, tiles_per_seq):
    i = pl.program_id(0)
    xn = _rms(x_ref[...], g_ref[...]).astype(BF16)
    proj = _bdot(xn, w_ref[...])
    o_q, o_k, o_v, o_f = POOL_DIM, POOL_DIM + FOX_DIM, POOL_DIM + 2 * FOX_DIM, POOL_DIM + 3 * FOX_DIM
    a_ref[...] = proj[:, :o_q]
    q_ref[...] = (proj[:, o_q:o_k] * (FOX_HEAD_DIM ** -0.5)).astype(BF16)
    k = proj[:, o_k:o_v]
    v = proj[:, o_v:o_f]
    k_ref[...] = k
    v_ref[...] = v
    kb_ref[...] = k.astype(BF16)
    vb_ref[...] = v.astype(BF16)
    lf = _log_sigmoid(proj[:, o_f:] + bf_ref[...])
    lf_ref[...] = lf
    tri = tri_ref[...]
    cum = sum(_bdot(tri, p) for p in _split3(lf))
    if tiles_per_seq > 1:
        @pl.when(i % tiles_per_seq == 0)
        def _():
            carry_ref[...] = jnp.zeros_like(carry_ref)
        cum = cum + carry_ref[...]
        carry_ref[...] = cum[-1:, :]
    cum_ref[...] = cum
    cumt_ref[0] = cum.T[:HEAD_ROWS, :]


def _in_even(x, g, w, bf, seq_len):
    rows, d = x.shape
    tm = _row_tile(rows)
    n_tiles = rows // tm
    r = np.arange(tm)
    if seq_len >= tm:
        assert seq_len % tm == 0
        tiles_per_seq = seq_len // tm
        tri = r[None, :] <= r[:, None]
        cumt_shape = (rows // seq_len, HEAD_ROWS, seq_len)
        cumt_map = lambda i: (i // tiles_per_seq, 0, i % tiles_per_seq)
    else:
        assert tm % seq_len == 0
        tiles_per_seq = 1
        tri = (r[None, :] <= r[:, None]) & ((r[None, :] // seq_len) == (r[:, None] // seq_len))
        cumt_shape = (n_tiles, HEAD_ROWS, tm)
        cumt_map = lambda i: (i, 0, 0)
    tri = jnp.asarray(tri, BF16)
    n = w.shape[1]
    row = lambda width: pl.BlockSpec((tm, width), lambda i: (i, 0))
    out_shape = (
        jax.ShapeDtypeStruct((rows, POOL_DIM), F32),
        jax.ShapeDtypeStruct((rows, FOX_DIM), BF16),
        jax.ShapeDtypeStruct((rows, FOX_DIM), F32),
        jax.ShapeDtypeStruct((rows, FOX_DIM), F32),
        jax.ShapeDtypeStruct((rows, FOX_DIM), BF16),
        jax.ShapeDtypeStruct((rows, FOX_DIM), BF16),
        jax.ShapeDtypeStruct((rows, LANES), F32),
        jax.ShapeDtypeStruct((rows, LANES), F32),
        jax.ShapeDtypeStruct(cumt_shape, F32),
    )
    out_specs = (row(POOL_DIM), row(FOX_DIM), row(FOX_DIM), row(FOX_DIM), row(FOX_DIM), row(FOX_DIM),
                 row(LANES), row(LANES), pl.BlockSpec((1, HEAD_ROWS, tm), cumt_map))
    return pl.pallas_call(
        functools.partial(_in_even_kernel, tiles_per_seq=tiles_per_seq),
        grid=(n_tiles,),
        in_specs=[row(d), _const_spec((1, d)), _const_spec((d, n)), _const_spec((1, LANES)),
                  _const_spec((tm, tm))],
        out_specs=out_specs,
        out_shape=out_shape,
        scratch_shapes=[pltpu.VMEM((1, LANES), F32)],
        compiler_params=_params("arbitrary"),
        name="in_even",
    )(x, g, w, bf, tri)


def _pool_kernel(a_ref, hist_ref, w_ref, scale_ref, out_ref, newhist_ref, ext_ref, *, seq_len, start_pos):
    t = seq_len
    base = POOL_HIST + 1
    x0 = a_ref[0]
    ext_ref[0:1, :] = jnp.zeros((1, POOL_DIM), F32)
    ext_ref[1:base, :] = hist_ref[0]
    ext_ref[base:base + t, :] = x0

    def back(kk):
        return ext_ref[base - kk:base - kk + t, :]

    sums = []
    run = x0
    nxt = 1
    for w in POOL_WINDOWS:
        while nxt < w:
            run = run + back(nxt)
            nxt += 1
        sums.append(run)
    grp = lax.broadcasted_iota(jnp.int32, (1, POOL_DIM), 1) // POOL_GROUP_DIM
    sel = sums[-1]
    win = jnp.full((1, POOL_DIM), POOL_WINDOWS[-1], jnp.int32)
    for gi in range(len(POOL_WINDOWS) - 2, -1, -1):
        sel = jnp.where(grp == gi, sums[gi], sel)
        win = jnp.where(grp == gi, POOL_WINDOWS[gi], win)
    pos = start_pos + lax.broadcasted_iota(jnp.int32, (t, 1), 0)
    cnt = jnp.minimum(pos + 1, win).astype(F32)
    dd = sel / cnt - x0
    y = _bdot(dd.astype(BF16), w_ref[...]) * scale_ref[...]
    out_ref[0] = y.astype(out_ref.dtype)
    newhist_ref[0] = ext_ref[t + 1:t + base, :]


def _pool_mix(a, hist, w_bd, scale, start_pos):
    b, t, _ = a.shape
    return pl.pallas_call(
        functools.partial(_pool_kernel, seq_len=t, start_pos=start_pos),
        grid=(b,),
        in_specs=[pl.BlockSpec((1, t, POOL_DIM), lambda i: (i, 0, 0)),
                  pl.BlockSpec((1, POOL_HIST, POOL_DIM), lambda i: (i, 0, 0)),
                  _const_spec((POOL_DIM, POOL_DIM)), _const_spec((1, POOL_DIM))],
        out_specs=(pl.BlockSpec((1, t, POOL_DIM), lambda i: (i, 0, 0)),
                   pl.BlockSpec((1, POOL_HIST, POOL_DIM), lambda i: (i, 0, 0))),
        out_shape=(jax.ShapeDtypeStruct((b, t, POOL_DIM), BF16),
                   jax.ShapeDtypeStruct((b, POOL_HIST, POOL_DIM), F32)),
        scratch_shapes=[pltpu.VMEM((POOL_HIST + 1 + t, POOL_DIM), F32)],
        compiler_params=_params("arbitrary"),
        name="pool_mix",
    )(a, hist, w_bd, scale)


def _fox_prompt_kernel(q_ref, k_ref, v_ref, cq_ref, ck_ref, o_ref, *, tile):
    hp = pl.program_id(1)
    qi = pl.program_id(2)
    q = q_ref[0]
    cq_all = cq_ref[0]
    lane = lax.broadcasted_iota(jnp.int32, (1, LANES), 1)
    row = lax.broadcasted_iota(jnp.int32, (tile, tile), 0)
    col = lax.broadcasted_iota(jnp.int32, (tile, tile), 1)
    head_row = lax.broadcasted_iota(jnp.int32, (HEAD_ROWS, 1), 0)
    outs = []
    for e in range(2):
        h = hp * 2 + e
        in_head = (lane >= FOX_HEAD_DIM) if e else (lane < FOX_HEAD_DIM)
        qm = jnp.where(in_head, q, jnp.zeros_like(q))
        cq = jnp.sum(jnp.where(lane == h, cq_all, 0.0), axis=-1, keepdims=True)

        def block(ki, carry, masked):
            m, l, acc = carry
            off = pl.multiple_of(ki * tile, tile)
            k = k_ref[0, pl.ds(off, tile), :]
            v = v_ref[0, pl.ds(off, tile), :]
            ck_all = ck_ref[0, :, pl.ds(off, tile)]
            ck = jnp.sum(jnp.where(head_row == h, ck_all, 0.0), axis=0, keepdims=True)
            s = _bdot_nt(qm, k) + (cq - ck)
            if masked:
                s = jnp.where(col <= row, s, NEG_BIG)
            m_new = jnp.maximum(m, jnp.max(s, axis=-1, keepdims=True))
            alpha = jnp.exp(m - m_new)
            p = jnp.exp(s - m_new)
            l_new = alpha * l + jnp.sum(p, axis=-1, keepdims=True)
            acc_new = alpha * acc + _bdot(p.astype(BF16), v)
            return m_new, l_new, acc_new

        init = (jnp.full((tile, 1), NEG_BIG, F32), jnp.zeros((tile, 1), F32), jnp.zeros((tile, LANES), F32))
        carry = lax.fori_loop(0, qi, functools.partial(block, masked=False), init)
        m, l, acc = block(qi, carry, True)
        outs.append(acc / l)
    o_ref[0] = jnp.where(lane < FOX_HEAD_DIM, outs[0], outs[1]).astype(o_ref.dtype)


def _fox_prompt(q, kb, vb, cum, cumt):
    b, s, _ = q.shape
    tile = ATT_TILE if s % ATT_TILE == 0 else s
    col = lambda: pl.BlockSpec((1, tile, LANES), lambda bi, hp, qi: (bi, qi, hp))
    full = lambda: pl.BlockSpec((1, s, LANES), lambda bi, hp, qi: (bi, 0, hp))
    return pl.pallas_call(
        functools.partial(_fox_prompt_kernel, tile=tile),
        grid=(b, FOX_HEADS // 2, s // tile),
        in_specs=[col(), full(), full(),
                  pl.BlockSpec((1, tile, LANES), lambda bi, hp, qi: (bi, qi, 0)),
                  pl.BlockSpec((1, HEAD_ROWS, s), lambda bi, hp, qi: (bi, 0, 0))],
        out_specs=col(),
        out_shape=jax.ShapeDtypeStruct((b, s, FOX_DIM), BF16),
        compiler_params=_params("arbitrary", "arbitrary", "arbitrary"),
        name="fox_prompt",
    )(q, kb, vb, cum, cumt)


def _suffix_kernel(lf_ref, upper_ref, ones_ref, out_ref):
    parts = _split3(lf_ref[...])
    out_ref[:, :LANES] = sum(_bdot(p, upper_ref[...]) for p in parts)
    out_ref[:, LANES:] = sum(_bdot(p, ones_ref[...]) for p in parts)


def _page_suffix(lf_rows):
    rows, page = lf_rows.shape
    tm = SUFFIX_ROW_TILE if rows % SUFFIX_ROW_TILE == 0 else rows
    r = np.arange(page)
    upper = jnp.asarray(r[:, None] > r[None, :], BF16)
    ones = jnp.ones((page, page), BF16)
    return pl.pallas_call(
        _suffix_kernel,
        grid=(rows // tm,),
        in_specs=[pl.BlockSpec((tm, page), lambda i: (i, 0)), _const_spec((page, page)), _const_spec((page, page))],
        out_specs=pl.BlockSpec((tm, 2 * page), lambda i: (i, 0)),
        out_shape=jax.ShapeDtypeStruct((rows, 2 * page), F32),
        compiler_params=_params("arbitrary"),
        name="page_suffix",
    )(lf_rows, upper, ones)


def _fox_sample_kernel(pt_ref, q_ref, kn_ref, vn_ref, gcol_ref, grow_ref, *rest, n_pg, t_new):
    k_refs = rest[:n_pg]
    v_refs = rest[n_pg:2 * n_pg]
    s_refs = rest[2 * n_pg:3 * n_pg]
    o_ref, m_sc, l_sc, acc_sc, run_sc = rest[3 * n_pg:]
    i = pl.program_id(1)
    last = pl.num_programs(1) - 1
    page = LANES

    @pl.when(i == 0)
    def _():
        m_sc[...] = jnp.full_like(m_sc, NEG_BIG)
        l_sc[...] = jnp.zeros_like(l_sc)
        acc_sc[...] = jnp.zeros_like(acc_sc)
        run_sc[...] = jnp.zeros_like(run_sc)

    run = run_sc[...]
    bias = [None] * n_pg
    for j in range(n_pg - 1, -1, -1):
        sf = s_refs[j][:, 0, 0, :]
        bias[j] = sf[:, :page] + run
        run = run + sf[:, page:]
    run_sc[...] = run

    q = q_ref[0]
    gcol = gcol_ref[0]
    for h in range(FOX_HEADS):
        hs = slice(h * FOX_HEAD_DIM, (h + 1) * FOX_HEAD_DIM)
        qh = q[:, hs]
        gq = gcol[:, h:h + 1]
        kt = jnp.concatenate([k_refs[j][0, h] for j in range(n_pg)], axis=1).astype(BF16)
        vt = jnp.concatenate([v_refs[j][0, h] for j in range(n_pg)], axis=1).astype(BF16)
        s = _bdot(qh, kt) + (gq + jnp.concatenate([bias[j][h:h + 1, :] for j in range(n_pg)], axis=1))
        m_prev = m_sc[h]
        m_new = jnp.maximum(m_prev, jnp.max(s, axis=-1, keepdims=True))
        alpha = jnp.exp(m_prev - m_new)
        p32 = jnp.exp(s - m_new[:, :1])
        p = p32.astype(BF16)
        l_sc[h] = alpha * l_sc[h] + jnp.sum(p32, axis=-1, keepdims=True)
        m_sc[h] = m_new
        acc_sc[h] = alpha[:, :FOX_HEAD_DIM] * acc_sc[h] + _bdot_nt(p, vt)

    @pl.when(i == last)
    def _():
        kn = kn_ref[0]
        vn = vn_ref[0]
        grow = grow_ref[0]
        r = lax.broadcasted_iota(jnp.int32, (t_new, t_new), 0)
        c = lax.broadcasted_iota(jnp.int32, (t_new, t_new), 1)
        for h in range(FOX_HEADS):
            hs = slice(h * FOX_HEAD_DIM, (h + 1) * FOX_HEAD_DIM)
            s = _bdot_nt(q[:, hs], kn[:, hs]) + (gcol[:, h:h + 1] - grow[h:h + 1, :])
            s = jnp.where(c <= r, s, NEG_BIG)
            m_prev = m_sc[h][:, :1]
            m_new = jnp.maximum(m_prev, jnp.max(s, axis=-1, keepdims=True))
            alpha = jnp.exp(m_prev - m_new)
            p32 = jnp.exp(s - m_new)
            l = alpha * l_sc[h][:, :1] + jnp.sum(p32, axis=-1, keepdims=True)
            acc = alpha * acc_sc[h] + _bdot(p32.astype(BF16), vn[:, hs])
            o_ref[0, :, hs] = (acc / l).astype(o_ref.dtype)


def _fox_sample(page_table, q, kn, vn, gcol, grow, cache_kt, cache_vt, suffix):
    b, t, _ = q.shape
    n_pages = page_table.shape[1]
    n_pg = PAGES_PER_STEP if n_pages % PAGES_PER_STEP == 0 else 1
    steps = n_pages // n_pg
    page = cache_kt.shape[-1]
    assert page == LANES

    def page_idx(bi, i, pt, j):
        return pt[bi, n_pages - n_pg * (i + 1) + j]

    tok = lambda width: pl.BlockSpec((1, t, width), lambda bi, i, pt: (bi, 0, 0))
    kv_specs = [pl.BlockSpec((1, FOX_HEADS, FOX_HEAD_DIM, page),
                             functools.partial(lambda bi, i, pt, j: (page_idx(bi, i, pt, j), 0, 0, 0), j=j))
                for j in range(n_pg)]
    sfx_specs = [pl.BlockSpec((FOX_HEADS, 1, 1, 2 * page),
                              functools.partial(lambda bi, i, pt, j: (0, page_idx(bi, i, pt, j), 0, 0), j=j))
                 for j in range(n_pg)]
    grid_spec = pltpu.PrefetchScalarGridSpec(
        num_scalar_prefetch=1,
        grid=(b, steps),
        in_specs=[tok(FOX_DIM), tok(FOX_DIM), tok(FOX_DIM), tok(LANES),
                  pl.BlockSpec((1, HEAD_ROWS, t), lambda bi, i, pt: (bi, 0, 0))]
                 + kv_specs + kv_specs + sfx_specs,
        out_specs=tok(FOX_DIM),
        scratch_shapes=[pltpu.VMEM((FOX_HEADS, t, LANES), F32), pltpu.VMEM((FOX_HEADS, t, LANES), F32),
                        pltpu.VMEM((FOX_HEADS, t, FOX_HEAD_DIM), F32), pltpu.VMEM((FOX_HEADS, LANES), F32)],
    )
    return pl.pallas_call(
        functools.partial(_fox_sample_kernel, n_pg=n_pg, t_new=t),
        grid_spec=grid_spec,
        out_shape=jax.ShapeDtypeStruct((b, t, FOX_DIM), BF16),
        compiler_params=_params("arbitrary", "arbitrary"),
        name="fox_sample",
    )(page_table, q, kn, vn, gcol, grow, *([cache_kt] * n_pg), *([cache_vt] * n_pg), *([suffix] * n_pg))


def _mix_mlp_kernel(h_ref, x1_ref, x2_ref, wo1_ref, wo2_ref, gf_ref, wu_ref, wd_ref, gl_ref, o_ref, *,
                    ff_chunk, final_norm):
    h1 = h_ref[...] + _bdot(x1_ref[...], wo1_ref[...]) + _bdot(x2_ref[...], wo2_ref[...])
    xn = _rms(h1, gf_ref[...]).astype(BF16)
    o_ref[...] = h1
    d_ff = wu_ref.shape[1]
    for c in range(d_ff // ff_chunk):
        u = jnp.maximum(_bdot(xn, wu_ref[:, c * ff_chunk:(c + 1) * ff_chunk]), 0.0)
        o_ref[...] += _bdot((u * u).astype(BF16), wd_ref[c * ff_chunk:(c + 1) * ff_chunk, :])
    if final_norm:
        o_ref[...] = _rms(o_ref[...], gl_ref[...])


def _mix_mlp(h, x1, x2, wo1, wo2, gf, wu, wd, gl, final_norm):
    rows, d = h.shape
    tm = _row_tile(rows)
    d_ff = wu.shape[1]
    row = lambda width: pl.BlockSpec((tm, width), lambda i: (i, 0))
    return pl.pallas_call(
        functools.partial(_mix_mlp_kernel, ff_chunk=min(d_ff, 1024), final_norm=final_norm),
        grid=(rows // tm,),
        in_specs=[row(d), row(x1.shape[1]), row(x2.shape[1]), _const_spec(wo1.shape), _const_spec(wo2.shape),
                  _const_spec((1, d)), _const_spec(wu.shape), _const_spec(wd.shape), _const_spec((1, d))],
        out_specs=row(d),
        out_shape=jax.ShapeDtypeStruct((rows, d), F32),
        compiler_params=_params("arbitrary"),
        name="mix_mlp",
    )(h, x1, x2, wo1, wo2, gf, wu, wd, gl)


def _in_odd_kernel(x_ref, g_ref, w_ref, lng_ref, lnb_ref, cos_ref, sin_ref,
                   u_ref, vn_ref, q_ref, k_ref, rv_ref, rg_ref):
    xn = _rms(x_ref[...], g_ref[...]).astype(BF16)
    proj = _bdot(xn, w_ref[...])
    gd, rd = GMLP_DIM, RET_DIM
    u_ref[...] = proj[:, :gd]
    gv = proj[:, gd:2 * gd]
    mu = jnp.mean(gv, axis=-1, keepdims=True)
    var = jnp.mean((gv - mu) ** 2, axis=-1, keepdims=True)
    vn_ref[...] = (gv - mu) * lax.rsqrt(var + EPS) * lng_ref[...] + lnb_ref[...]
    cos = cos_ref[...]
    sin = sin_ref[...]
    o_q, o_k, o_v, o_g = 2 * gd, 2 * gd + rd, 2 * gd + 2 * rd, 2 * gd + 3 * rd
    for hh in range(RET_HEADS):
        sl = slice(hh * RET_HEAD_DIM, (hh + 1) * RET_HEAD_DIM)
        qh = proj[:, o_q + hh * RET_HEAD_DIM:o_q + (hh + 1) * RET_HEAD_DIM]
        kh = proj[:, o_k + hh * RET_HEAD_DIM:o_k + (hh + 1) * RET_HEAD_DIM]
        q_ref[:, sl] = (qh * cos + pltpu.roll(qh, RET_HEAD_DIM // 2, 1) * sin).astype(BF16)
        k_ref[:, sl] = (kh * cos + pltpu.roll(kh, RET_HEAD_DIM // 2, 1) * sin) * (RET_HEAD_DIM ** -0.5)
    rv_ref[...] = proj[:, o_v:o_g].astype(BF16)
    rg_ref[...] = proj[:, o_g:]


def _in_odd(x, g, w, lng, lnb, cos_t, sin_t, seq_len):
    rows, d = x.shape
    tm = _row_tile(rows)
    if seq_len >= tm:
        tiles_per_seq = seq_len // tm
        rope_map = lambda i: (i % tiles_per_seq, 0)
    else:
        reps = tm // seq_len
        cos_t = jnp.tile(cos_t, (reps, 1))
        sin_t = jnp.tile(sin_t, (reps, 1))
        rope_map = lambda i: (0, 0)
    row = lambda width: pl.BlockSpec((tm, width), lambda i: (i, 0))
    outs = ((GMLP_DIM, F32), (GMLP_DIM, F32), (RET_DIM, BF16), (RET_DIM, F32), (RET_DIM, BF16), (RET_DIM, F32))
    return pl.pallas_call(
        _in_odd_kernel,
        grid=(rows // tm,),
        in_specs=[row(d), _const_spec((1, d)), _const_spec(w.shape), _const_spec((1, GMLP_DIM)),
                  _const_spec((1, GMLP_DIM)), pl.BlockSpec((tm, RET_HEAD_DIM), rope_map),
                  pl.BlockSpec((tm, RET_HEAD_DIM), rope_map)],
        out_specs=tuple(row(wd) for wd, _ in outs),
        out_shape=tuple(jax.ShapeDtypeStruct((rows, wd), dt) for wd, dt in outs),
        compiler_params=_params("arbitrary"),
        name="in_odd",
    )(x, g, w, lng, lnb, cos_t, sin_t)


def _gmlp_kernel(u_ref, vn_ref, ws_ref, bs_ref, o_ref, *, chunk):
    vn = vn_ref[0].astype(BF16)
    r = lax.broadcasted_iota(jnp.int32, (chunk, chunk), 0)
    c = lax.broadcasted_iota(jnp.int32, (chunk, chunk), 1)
    for gi in range(GMLP_GROUPS):
        sl = slice(gi * LANES, (gi + 1) * LANES)
        ws = jnp.where(c <= r, ws_ref[gi], 0.0).astype(BF16)
        z = _bdot(ws, vn[:, sl]) + bs_ref[gi]
        o_ref[0, :, sl] = (u_ref[0, :, sl] * z).astype(o_ref.dtype)


def _gmlp_mix(u, vn, ws, bs):
    b, t, _ = u.shape
    chunk = ws.shape[-1]
    blk = lambda: pl.BlockSpec((1, chunk, GMLP_DIM), lambda bi, ci: (bi, ci, 0))
    return pl.pallas_call(
        functools.partial(_gmlp_kernel, chunk=chunk),
        grid=(b, t // chunk),
        in_specs=[blk(), blk(), _const_spec(ws.shape), _const_spec(bs.shape)],
        out_specs=blk(),
        out_shape=jax.ShapeDtypeStruct((b, t, GMLP_DIM), BF16),
        compiler_params=_params("arbitrary", "arbitrary"),
        name="gmlp_mix",
    )(u, vn, ws, bs)


def _ret_kernel(q_ref, k_ref, v_ref, g_ref, s0_ref, dintra_ref, dq_ref, dk_ref, ds_ref, gn_ref,
                o_ref, sout_ref, st_sc):
    c = pl.program_id(1)

    @pl.when(c == 0)
    def _():
        st_sc[...] = s0_ref[0]

    for hh in range(RET_HEADS):
        sl = slice(hh * RET_HEAD_DIM, (hh + 1) * RET_HEAD_DIM)
        q = q_ref[0, :, sl]
        kf = k_ref[0, :, sl]
        v = v_ref[0, :, sl]
        st = st_sc[hh]
        sc = _bdot_nt(q, kf.astype(BF16)) * dintra_ref[hh]
        o = _bdot(sc.astype(BF16), v) + _bdot(q, st.astype(BF16)) * dq_ref[hh]
        st_sc[hh] = st * ds_ref[hh] + _bdot_tn((kf * dk_ref[hh]).astype(BF16), v)
        mu = jnp.mean(o, axis=-1, keepdims=True)
        var = jnp.mean((o - mu) ** 2, axis=-1, keepdims=True)
        on = (o - mu) * lax.rsqrt(var + EPS) * gn_ref[:, sl]
        gate = g_ref[0, :, sl]
        o_ref[0, :, sl] = (on * (gate * jax.nn.sigmoid(gate))).astype(o_ref.dtype)

    @pl.when(c == pl.num_programs(1) - 1)
    def _():
        sout_ref[0] = st_sc[...]


def _retention(q, k, v, gate, s0, gn):
    b, t, _ = q.shape
    lc = math.gcd(t, RET_CHUNK) if t >= RET_CHUNK else t
    log_g = jnp.log1p(-jnp.exp2(-5.0 - jnp.arange(RET_HEADS, dtype=F32)))
    idx = jnp.arange(lc, dtype=F32)
    diff = idx[:, None] - idx[None, :]
    dintra = jnp.where(diff[None] >= 0, jnp.exp(diff[None] * log_g[:, None, None]), 0.0)
    lanes = (RET_HEADS, lc, RET_HEAD_DIM)
    dq = jnp.broadcast_to(jnp.exp((idx[None, :] + 1.0) * log_g[:, None])[:, :, None], lanes)
    dk = jnp.broadcast_to(jnp.exp((lc - 1.0 - idx)[None, :] * log_g[:, None])[:, :, None], lanes)
    ds = jnp.broadcast_to(jnp.exp(lc * log_g)[:, None, None], (RET_HEADS, 1, RET_HEAD_DIM))
    blk = lambda: pl.BlockSpec((1, lc, RET_DIM), lambda bi, ci: (bi, ci, 0))
    st = lambda: pl.BlockSpec((1, RET_HEADS, RET_HEAD_DIM, RET_HEAD_DIM), lambda bi, ci: (bi, 0, 0, 0))
    return pl.pallas_call(
        _ret_kernel,
        grid=(b, t // lc),
        in_specs=[blk(), blk(), blk(), blk(), st(), _const_spec(dintra.shape), _const_spec(lanes),
                  _const_spec(lanes), _const_spec(ds.shape), _const_spec((1, RET_DIM))],
        out_specs=(blk(), st()),
        out_shape=(jax.ShapeDtypeStruct((b, t, RET_DIM), BF16),
                   jax.ShapeDtypeStruct((b, RET_HEADS, RET_HEAD_DIM, RET_HEAD_DIM), F32)),
        scratch_shapes=[pltpu.VMEM((RET_HEADS, RET_HEAD_DIM, RET_HEAD_DIM), F32)],
        compiler_params=_params("arbitrary", "arbitrary"),
        name="retention",
    )(q, k, v, gate, s0, dintra, dq, dk, ds, gn)


def _rope_tables(start_pos, t):
    half = RET_HEAD_DIM // 2
    pos = (start_pos + jnp.arange(t)).astype(F32)
    inv = ROPE_BASE ** (-jnp.arange(half, dtype=F32) / half)
    ang = pos[:, None] * inv[None, :]
    cos, sin = jnp.cos(ang), jnp.sin(ang)
    return jnp.concatenate([cos, cos], axis=1), jnp.concatenate([-sin, sin], axis=1)


def _block_diag(w):
    g, n, _ = w.shape
    out = jnp.zeros((g * n, g * n), w.dtype)
    for i in range(g):
        out = out.at[i * n:(i + 1) * n, i * n:(i + 1) * n].set(w[i])
    return out


def _run_group(x, start_pos, hist0, state0, attend, wts):
    b, t, d = x.shape
    rows = b * t
    h = x.reshape(rows, d)
    row1 = lambda a: a.reshape(1, -1)

    a, q, k, v, kb, vb, lf, cum, cumt = _in_even(h, row1(wts["norm_mix"][0]), wts["w_in_even"], wts["b_forget"], t)
    pool_out, new_hist = _pool_mix(a.reshape(b, t, POOL_DIM), hist0, wts["pool_w"], wts["pool_scale"], start_pos)
    att = attend(q.reshape(b, t, FOX_DIM), kb.reshape(b, t, FOX_DIM), vb.reshape(b, t, FOX_DIM), cum, cumt)
    h = _mix_mlp(h, pool_out.reshape(rows, POOL_DIM), att.reshape(rows, FOX_DIM),
                 wts["w_out_even"][:POOL_DIM], wts["w_out_even"][POOL_DIM:], row1(wts["norm_ffn"][0]),
                 wts["w_up"][0], wts["w_down"][0], row1(wts["norm_final"]), False)

    cos_t, sin_t = _rope_tables(start_pos, t)
    u, vn, rq, rk, rv, rg = _in_odd(h, row1(wts["norm_mix"][1]), wts["w_in_odd"], row1(wts["gmlp_ln_g"]),
                                    row1(wts["gmlp_ln_b"]), cos_t, sin_t, t)
    chunk = min(t, GMLP_CHUNK)
    assert t % chunk == 0
    ws = wts["gmlp_ws"][:, :chunk, :chunk]
    bs = jnp.broadcast_to(wts["gmlp_bs"][:, :chunk, None], (GMLP_GROUPS, chunk, LANES))
    r3 = lambda z, wd: z.reshape(b, t, wd)
    c_out = _gmlp_mix(r3(u, GMLP_DIM), r3(vn, GMLP_DIM), ws, bs)
    r_out, s_new = _retention(r3(rq, RET_DIM), r3(rk, RET_DIM), r3(rv, RET_DIM), r3(rg, RET_DIM), state0,
                              row1(wts["ret_gn_g"]))
    y = _mix_mlp(h, c_out.reshape(rows, GMLP_DIM), r_out.reshape(rows, RET_DIM),
                 wts["w_out_odd"][:GMLP_DIM], wts["w_out_odd"][GMLP_DIM:], row1(wts["norm_ffn"][1]),
                 wts["w_up"][1], wts["w_down"][1], row1(wts["norm_final"]), True)

    return (y.reshape(b, t, d), new_hist[None],
            k.reshape(1, b, t, FOX_HEADS, FOX_HEAD_DIM), v.reshape(1, b, t, FOX_HEADS, FOX_HEAD_DIM),
            lf[:, :FOX_HEADS].reshape(1, b, t, FOX_HEADS), vn.reshape(1, b, t, GMLP_DIM), s_new[None])


def kernel(x_prompt, x_sample, state_pool, cache_k, cache_v, cache_logf, page_table, state_ret, norm_mix, w_in_even, b_forget, pool_w, pool_scale, w_out_even, w_in_odd, gmlp_ln_g, gmlp_ln_b, gmlp_ws, gmlp_bs, ret_gn_g, w_out_odd, norm_ffn, w_up, w_down, norm_final):
    assert norm_mix.shape[0] == 2 and w_in_even.shape[0] == 1 and w_in_odd.shape[0] == 1
    d = x_prompt.shape[-1]
    ev_cols = w_in_even.shape[-1]
    ev_pad = POOL_DIM + 3 * FOX_DIM + LANES - ev_cols
    wts = {
        "norm_mix": norm_mix, "norm_ffn": norm_ffn, "norm_final": norm_final,
        "w_in_even": jnp.pad(w_in_even[0], ((0, 0), (0, ev_pad))).astype(BF16),
        "b_forget": jnp.pad(b_forget[0], (0, LANES - FOX_HEADS)).reshape(1, LANES),
        "pool_w": _block_diag(pool_w[0]).astype(BF16),
        "pool_scale": pool_scale[0].reshape(1, POOL_DIM),
        "w_out_even": w_out_even[0].astype(BF16),
        "w_in_odd": w_in_odd[0].astype(BF16),
        "gmlp_ln_g": gmlp_ln_g[0], "gmlp_ln_b": gmlp_ln_b[0], "gmlp_ws": gmlp_ws[0], "gmlp_bs": gmlp_bs[0],
        "ret_gn_g": ret_gn_g[0],
        "w_out_odd": w_out_odd[0].astype(BF16),
        "w_up": w_up.astype(BF16), "w_down": w_down.astype(BF16),
    }
    b, s, _ = x_prompt.shape
    db, t, _ = x_sample.shape
    n_pages = page_table.shape[1]
    page = cache_k.shape[2]
    past = n_pages * page

    def attend_prompt(q, kb, vb, cum, cumt):
        return _fox_prompt(q, kb, vb, cum.reshape(b, s, LANES), cumt)

    cache_kt = jnp.transpose(cache_k[0], (0, 2, 3, 1))
    cache_vt = jnp.transpose(cache_v[0], (0, 2, 3, 1))
    n_phys = cache_kt.shape[0]
    lf_rows = jnp.transpose(cache_logf[0], (2, 0, 1)).reshape(FOX_HEADS * n_phys, page)
    suffix = _page_suffix(lf_rows).reshape(FOX_HEADS, n_phys, 1, 2 * page)

    def attend_sample(q, kb, vb, cum, cumt):
        gcol = cum.reshape(db, t, LANES)
        grow = jnp.transpose(gcol, (0, 2, 1))[:, :HEAD_ROWS, :]
        return _fox_sample(page_table, q, kb, vb, gcol, grow, cache_kt, cache_vt, suffix)

    y_p, pool_p, k_p, v_p, lf_p, _, ret_p = _run_group(
        x_prompt, 0, jnp.zeros((b, POOL_HIST, POOL_DIM), F32),
        jnp.zeros((b, RET_HEADS, RET_HEAD_DIM, RET_HEAD_DIM), F32), attend_prompt, wts)
    y_s, pool_s, k_s, v_s, lf_s, gv_s, ret_s = _run_group(
        x_sample, past, state_pool[0], state_ret[0], attend_sample, wts)
    return (y_p, y_s, pool_p, k_p, v_p, lf_p, ret_p, pool_s, k_s, v_s, lf_s, gv_s, ret_s)
```

```python
import functools
import math

import numpy as np
import jax
import jax.numpy as jnp
from jax import lax
from jax.experimental import pallas as pl
from jax.experimental.pallas import tpu as pltpu

F32 = jnp.float32
BF16 = jnp.bfloat16
EPS = 1e-6

POOL_WINDOWS = (2, 4, 8, 16)
POOL_GROUP_DIM = 64
POOL_DIM = 256
POOL_HIST = 15
FOX_HEAD_DIM = 64
FOX_HEADS = 12
FOX_DIM = FOX_HEADS * FOX_HEAD_DIM
GMLP_CHUNK = 128
GMLP_GROUPS = 4
GMLP_DIM = 512
RET_HEADS = 4
RET_DIM = 512
RET_HEAD_DIM = 128
RET_CHUNK = 128
ROPE_BASE = 10000.0

LANES = 128
SUBLANES = 8
HEAD_ROWS = 16
NEG_BIG = -1e30
VMEM_LIMIT = 56 * 1024 * 1024
ROW_TILE = 512
ATT_TILE = 512
PAGES_PER_STEP = 8
SUFFIX_PAGE_TILE = 256


def _row_tile(rows):
    return ROW_TILE if rows % ROW_TILE == 0 else rows


def _params(*sem):
    return pltpu.CompilerParams(dimension_semantics=sem, vmem_limit_bytes=VMEM_LIMIT)


def _const_spec(shape):
    nd = len(shape)
    return pl.BlockSpec(shape, lambda *_: (0,) * nd, pipeline_mode=pl.Buffered(1))


def _rms(x, g):
    return x * lax.rsqrt(jnp.mean(x * x, axis=-1, keepdims=True) + EPS) * g


def _bdot(a, b):
    return jnp.dot(a, b, preferred_element_type=F32)


def _bdot_nt(a, b):
    return lax.dot_general(a, b, (((1,), (1,)), ((), ())), preferred_element_type=F32)


def _bdot_tn(a, b):
    return lax.dot_general(a, b, (((0,), (0,)), ((), ())), preferred_element_type=F32)


def _split3(x):
    hi = x.astype(BF16)
    r = x - hi.astype(F32)
    mid = r.astype(BF16)
    lo = (r - mid.astype(F32)).astype(BF16)
    return hi, mid, lo


def _log_sigmoid(x):
    return jnp.minimum(x, 0.0) - jnp.log1p(jnp.exp(-jnp.abs(x)))


def _in_even_kernel(x_ref, g_ref, w_ref, bf_ref, tri_ref,
                    a_ref, q_ref, k_ref, v_ref, kb_ref, vb_ref, lf_ref, cum_ref, cumt_ref,
                    carry_ref, *, tiles_per_seq):
    i = pl.program_id(0)
    xn = _rms(x_ref[...], g_ref[...]).astype(BF16)
    proj = _bdot(xn, w_ref[...])
    o_q, o_k, o_v, o_f = POOL_DIM, POOL_DIM + FOX_DIM, POOL_DIM + 2 * FOX_DIM, POOL_DIM + 3 * FOX_DIM
    a_ref[...] = proj[:, :o_q]
    q_ref[...] = (proj[:, o_q:o_k] * (FOX_HEAD_DIM ** -0.5)).astype(BF16)
    k = proj[:, o_k:o_v]
    v = proj[:, o_v:o_f]
    k_ref[...] = k
    v_ref[...] = v
    kb_ref[...] = k.astype(BF16)
    vb_ref[...] = v.astype(BF16)
    lf = _log_sigmoid(proj[:, o_f:] + bf_ref[...])
    lf_ref[...] = lf
    tri = tri_ref[...]
    cum = sum(_bdot(tri, p) for p in _split3(lf))
    if tiles_per_seq > 1:
        @pl.when(i % tiles_per_seq == 0)
        def _():
            carry_ref[...] = jnp.zeros_like(carry_ref)
        cum = cum + carry_ref[...]
        carry_ref[...] = cum[-1:, :]
    cum_ref[...] = cum
    cumt_ref[0] = cum.T[:HEAD_ROWS, :]


def _in_even(x, g, w, bf, seq_len):
    rows, d = x.shape
    tm = _row_tile(rows)
    n_tiles = rows // tm
    r = np.arange(tm)
    if seq_len >= tm:
        assert seq_len % tm == 0
        tiles_per_seq = seq_len // tm
        tri = r[None, :] <= r[:, None]
        cumt_shape = (rows // seq_len, HEAD_ROWS, seq_len)
        cumt_map = lambda i: (i // tiles_per_seq, 0, i % tiles_per_seq)
    else:
        assert tm % seq_len == 0
        tiles_per_seq = 1
        tri = (r[None, :] <= r[:, None]) & ((r[None, :] // seq_len) == (r[:, None] // seq_len))
        cumt_shape = (n_tiles, HEAD_ROWS, tm)
        cumt_map = lambda i: (i, 0, 0)
    tri = jnp.asarray(tri, BF16)
    n = w.shape[1]
    row = lambda width: pl.BlockSpec((tm, width), lambda i: (i, 0))
    out_shape = (
        jax.ShapeDtypeStruct((rows, POOL_DIM), F32),
        jax.ShapeDtypeStruct((rows, FOX_DIM), BF16),
        jax.ShapeDtypeStruct((rows, FOX_DIM), F32),
        jax.ShapeDtypeStruct((rows, FOX_DIM), F32),
        jax.ShapeDtypeStruct((rows, FOX_DIM), BF16),
        jax.ShapeDtypeStruct((rows, FOX_DIM), BF16),
        jax.ShapeDtypeStruct((rows, LANES), F32),
        jax.ShapeDtypeStruct((rows, LANES), F32),
        jax.ShapeDtypeStruct(cumt_shape, F32),
    )
    out_specs = (row(POOL_DIM), row(FOX_DIM), row(FOX_DIM), row(FOX_DIM), row(FOX_DIM), row(FOX_DIM),
                 row(LANES), row(LANES), pl.BlockSpec((1, HEAD_ROWS, tm), cumt_map))
    return pl.pallas_call(
        functools.partial(_in_even_kernel, tiles_per_seq=tiles_per_seq),
        grid=(n_tiles,),
        in_specs=[row(d), _const_spec((1, d)), _const_spec((d, n)), _const_spec((1, LANES)),
                  _const_spec((tm, tm))],
        out_specs=out_specs,
        out_shape=out_shape,
        scratch_shapes=[pltpu.VMEM((1, LANES), F32)],
        compiler_params=_params("arbitrary"),
        name="in_even",
    )(x, g, w, bf, tri)


def _pool_kernel(a_ref, hist_ref, w_ref, scale_ref, out_ref, newhist_ref, ext_ref, *, seq_len, start_pos):
    t = seq_len
    base = POOL_HIST + 1
    x0 = a_ref[0]
    ext_ref[0:1, :] = jnp.zeros((1, POOL_DIM), F32)
    ext_ref[1:base, :] = hist_ref[0]
    ext_ref[base:base + t, :] = x0

    def back(kk):
        return ext_ref[base - kk:base - kk + t, :]

    sums = []
    run = x0
    nxt = 1
    for w in POOL_WINDOWS:
        while nxt < w:
            run = run + back(nxt)
            nxt += 1
        sums.append(run)
    grp = lax.broadcasted_iota(jnp.int32, (1, POOL_DIM), 1) // POOL_GROUP_DIM
    sel = sums[-1]
    win = jnp.full((1, POOL_DIM), POOL_WINDOWS[-1], jnp.int32)
    for gi in range(len(POOL_WINDOWS) - 2, -1, -1):
        sel = jnp.where(grp == gi, sums[gi], sel)
        win = jnp.where(grp == gi, POOL_WINDOWS[gi], win)
    pos = start_pos + lax.broadcasted_iota(jnp.int32, (t, 1), 0)
    cnt = jnp.minimum(pos + 1, win).astype(F32)
    dd = sel / cnt - x0
    y = _bdot(dd.astype(BF16), w_ref[...]) * scale_ref[...]
    out_ref[0] = y.astype(out_ref.dtype)
    newhist_ref[0] = ext_ref[t + 1:t + base, :]


def _pool_mix(a, hist, w_bd, scale, start_pos):
    b, t, _ = a.shape
    return pl.pallas_call(
        functools.partial(_pool_kernel, seq_len=t, start_pos=start_pos),
        grid=(b,),
        in_specs=[pl.BlockSpec((1, t, POOL_DIM), lambda i: (i, 0, 0)),
                  pl.BlockSpec((1, POOL_HIST, POOL_DIM), lambda i: (i, 0, 0)),
                  _const_spec((POOL_DIM, POOL_DIM)), _const_spec((1, POOL_DIM))],
        out_specs=(pl.BlockSpec((1, t, POOL_DIM), lambda i: (i, 0, 0)),
                   pl.BlockSpec((1, POOL_HIST, POOL_DIM), lambda i: (i, 0, 0))),
        out_shape=(jax.ShapeDtypeStruct((b, t, POOL_DIM), BF16),
                   jax.ShapeDtypeStruct((b, POOL_HIST, POOL_DIM), F32)),
        scratch_shapes=[pltpu.VMEM((POOL_HIST + 1 + t, POOL_DIM), F32)],
        compiler_params=_params("arbitrary"),
        name="pool_mix",
    )(a, hist, w_bd, scale)


def _fox_prompt_kernel(q_ref, k_ref, v_ref, cq_ref, ck_ref, o_ref, *, tile):
    hp = pl.program_id(1)
    qi = pl.program_id(2)
    q = q_ref[0]
    cq_all = cq_ref[0]
    lane = lax.broadcasted_iota(jnp.int32, (1, LANES), 1)
    row = lax.broadcasted_iota(jnp.int32, (tile, tile), 0)
    col = lax.broadcasted_iota(jnp.int32, (tile, tile), 1)
    head_row = lax.broadcasted_iota(jnp.int32, (HEAD_ROWS, 1), 0)
    outs = []
    for e in range(2):
        h = hp * 2 + e
        in_head = (lane >= FOX_HEAD_DIM) if e else (lane < FOX_HEAD_DIM)
        qm = jnp.where(in_head, q, jnp.zeros_like(q))
        cq = jnp.sum(jnp.where(lane == h, cq_all, 0.0), axis=-1, keepdims=True)

        def block(ki, carry, masked):
            m, l, acc = carry
            off = pl.multiple_of(ki * tile, tile)
            k = k_ref[0, pl.ds(off, tile), :]
            v = v_ref[0, pl.ds(off, tile), :]
            ck_all = ck_ref[0, :, pl.ds(off, tile)]
            ck = jnp.sum(jnp.where(head_row == h, ck_all, 0.0), axis=0, keepdims=True)
            s = _bdot_nt(qm, k) + (cq - ck)
            if masked:
                s = jnp.where(col <= row, s, NEG_BIG)
            m_new = jnp.maximum(m, jnp.max(s, axis=-1, keepdims=True))
            alpha = jnp.exp(m - m_new)
            p = jnp.exp(s - m_new)
            l_new = alpha * l + jnp.sum(p, axis=-1, keepdims=True)
            acc_new = alpha * acc + _bdot(p.astype(BF16), v)
            return m_new, l_new, acc_new

        init = (jnp.full((tile, 1), NEG_BIG, F32), jnp.zeros((tile, 1), F32), jnp.zeros((tile, LANES), F32))
        carry = lax.fori_loop(0, qi, functools.partial(block, masked=False), init)
        m, l, acc = block(qi, carry, True)
        outs.append(acc / l)
    o_ref[0] = jnp.where(lane < FOX_HEAD_DIM, outs[0], outs[1]).astype(o_ref.dtype)


def _fox_prompt(q, kb, vb, cum, cumt):
    b, s, _ = q.shape
    tile = ATT_TILE if s % ATT_TILE == 0 else s
    col = lambda: pl.BlockSpec((1, tile, LANES), lambda bi, hp, qi: (bi, qi, hp))
    full = lambda: pl.BlockSpec((1, s, LANES), lambda bi, hp, qi: (bi, 0, hp))
    return pl.pallas_call(
        functools.partial(_fox_prompt_kernel, tile=tile),
        grid=(b, FOX_HEADS // 2, s // tile),
        in_specs=[col(), full(), full(),
                  pl.BlockSpec((1, tile, LANES), lambda bi, hp, qi: (bi, qi, 0)),
                  pl.BlockSpec((1, HEAD_ROWS, s), lambda bi, hp, qi: (bi, 0, 0))],
        out_specs=col(),
        out_shape=jax.ShapeDtypeStruct((b, s, FOX_DIM), BF16),
        compiler_params=_params("arbitrary", "arbitrary", "arbitrary"),
        name="fox_prompt",
    )(q, kb, vb, cum, cumt)


def _suffix_kernel(lf_ref, mat_ref, out_ref):
    for h in range(FOX_HEADS):
        out_ref[:, h, :] = sum(_bdot(p, mat_ref[...]) for p in _split3(lf_ref[h]))


def _page_suffix(lf_hpp):
    heads, n_phys, page = lf_hpp.shape
    tp = SUFFIX_PAGE_TILE if n_phys % SUFFIX_PAGE_TILE == 0 else n_phys
    r = np.arange(page)
    later = r[:, None] > r[None, :]
    mat = jnp.asarray(np.concatenate([later, np.ones((page, page), bool)], axis=1), BF16)
    return pl.pallas_call(
        _suffix_kernel,
        grid=(n_phys // tp,),
        in_specs=[pl.BlockSpec((heads, tp, page), lambda i: (0, i, 0)), _const_spec((page, 2 * page))],
        out_specs=pl.BlockSpec((tp, heads, 2 * page), lambda i: (i, 0, 0)),
        out_shape=jax.ShapeDtypeStruct((n_phys, heads, 2 * page), F32),
        compiler_params=_params("arbitrary"),
        name="page_suffix",
    )(lf_hpp, mat)


def _fox_sample_kernel(pt_ref, q_ref, kn_ref, vn_ref, gq_ref, grow_ref, *rest, n_pg, t_new):
    k_refs = rest[:n_pg]
    v_refs = rest[n_pg:2 * n_pg]
    s_refs = rest[2 * n_pg:3 * n_pg]
    o_ref, m_sc, l_sc, acc_sc, run_sc = rest[3 * n_pg:]
    i = pl.program_id(1)
    last = pl.num_programs(1) - 1
    page = LANES

    @pl.when(i == 0)
    def _():
        m_sc[...] = jnp.full_like(m_sc, NEG_BIG)
        l_sc[...] = jnp.zeros_like(l_sc)
        acc_sc[...] = jnp.zeros_like(acc_sc)
        run_sc[...] = jnp.zeros_like(run_sc)

    run = run_sc[...]
    bias = [None] * n_pg
    for j in range(n_pg - 1, -1, -1):
        sf = s_refs[j][0]
        bias[j] = sf[:, :page] + run
        run = run + sf[:, page:]
    run_sc[...] = run

    t = t_new
    q = q_ref[0]
    gq = gq_ref[0]
    s_parts = []
    for h in range(FOX_HEADS):
        hs = slice(h * FOX_HEAD_DIM, (h + 1) * FOX_HEAD_DIM)
        kt = jnp.concatenate([k_refs[j][0, h] for j in range(n_pg)], axis=1).astype(BF16)
        s_parts.append(_bdot(q[:, hs], kt) + jnp.concatenate([bias[j][h:h + 1, :] for j in range(n_pg)], axis=1))
    s = jnp.concatenate(s_parts, axis=0) + jnp.concatenate([gq] * n_pg, axis=1)
    m_prev = m_sc[...]
    m_new = jnp.maximum(m_prev, jnp.max(s, axis=-1, keepdims=True))
    alpha = jnp.exp(m_prev - m_new)
    p32 = jnp.exp(s - jnp.concatenate([m_new] * n_pg, axis=1))
    l_sc[...] = alpha * l_sc[...] + jnp.sum(p32, axis=-1, keepdims=True)
    m_sc[...] = m_new
    pv = []
    for h in range(FOX_HEADS):
        vt = jnp.concatenate([v_refs[j][0, h] for j in range(n_pg)], axis=1).astype(BF16)
        pv.append(_bdot_nt(p32[h * t:(h + 1) * t].astype(BF16), vt))
    acc_sc[...] = alpha[:, :FOX_HEAD_DIM] * acc_sc[...] + jnp.concatenate(pv, axis=0)

    @pl.when(i == last)
    def _():
        kn = kn_ref[0]
        vn = vn_ref[0]
        grow = grow_ref[0]
        r = lax.broadcasted_iota(jnp.int32, (t, t), 0)
        c = lax.broadcasted_iota(jnp.int32, (t, t), 1)
        for h in range(FOX_HEADS):
            hs = slice(h * FOX_HEAD_DIM, (h + 1) * FOX_HEAD_DIM)
            rows = slice(h * t, (h + 1) * t)
            s = _bdot_nt(q[:, hs], kn[:, hs]) + (gq[rows, :1] - grow[h:h + 1, :])
            s = jnp.where(c <= r, s, NEG_BIG)
            m_prev = m_sc[rows, :1]
            m_new = jnp.maximum(m_prev, jnp.max(s, axis=-1, keepdims=True))
            alpha = jnp.exp(m_prev - m_new)
            p32 = jnp.exp(s - m_new)
            l = alpha * l_sc[rows, :1] + jnp.sum(p32, axis=-1, keepdims=True)
            acc = alpha * acc_sc[rows, :] + _bdot(p32.astype(BF16), vn[:, hs])
            o_ref[0, :, hs] = (acc / l).astype(o_ref.dtype)


def _fox_sample(page_table, q, kn, vn, gq, grow, cache_kt, cache_vt, suffix):
    b, t, _ = q.shape
    n_pages = page_table.shape[1]
    n_pg = PAGES_PER_STEP if n_pages % PAGES_PER_STEP == 0 else 1
    steps = n_pages // n_pg
    page = cache_kt.shape[-1]
    assert page == LANES

    def page_idx(bi, i, pt, j):
        return pt[bi, n_pages - n_pg * (i + 1) + j]

    tok = lambda width: pl.BlockSpec((1, t, width), lambda bi, i, pt: (bi, 0, 0))
    kv_specs = [pl.BlockSpec((1, FOX_HEADS, FOX_HEAD_DIM, page),
                             functools.partial(lambda bi, i, pt, j: (page_idx(bi, i, pt, j), 0, 0, 0), j=j))
                for j in range(n_pg)]
    sfx_specs = [pl.BlockSpec((1, FOX_HEADS, 2 * page),
                              functools.partial(lambda bi, i, pt, j: (page_idx(bi, i, pt, j), 0, 0), j=j))
                 for j in range(n_pg)]
    grid_spec = pltpu.PrefetchScalarGridSpec(
        num_scalar_prefetch=1,
        grid=(b, steps),
        in_specs=[tok(FOX_DIM), tok(FOX_DIM), tok(FOX_DIM),
                  pl.BlockSpec((1, FOX_HEADS * t, LANES), lambda bi, i, pt: (bi, 0, 0)),
                  pl.BlockSpec((1, HEAD_ROWS, t), lambda bi, i, pt: (bi, 0, 0))]
                 + kv_specs + kv_specs + sfx_specs,
        out_specs=tok(FOX_DIM),
        scratch_shapes=[pltpu.VMEM((FOX_HEADS * t, LANES), F32), pltpu.VMEM((FOX_HEADS * t, LANES), F32),
                        pltpu.VMEM((FOX_HEADS * t, FOX_HEAD_DIM), F32), pltpu.VMEM((FOX_HEADS, LANES), F32)],
    )
    return pl.pallas_call(
        functools.partial(_fox_sample_kernel, n_pg=n_pg, t_new=t),
        grid_spec=grid_spec,
        out_shape=jax.ShapeDtypeStruct((b, t, FOX_DIM), BF16),
        compiler_params=_params("arbitrary", "arbitrary"),
        name="fox_sample",
    )(page_table, q, kn, vn, gq, grow, *([cache_kt] * n_pg), *([cache_vt] * n_pg), *([suffix] * n_pg))


def _mix_mlp_kernel(h_ref, x1_ref, x2_ref, wo1_ref, wo2_ref, gf_ref, wu_ref, wd_ref, gl_ref, o_ref, *,
                    ff_chunk, final_norm):
    h1 = h_ref[...] + _bdot(x1_ref[...], wo1_ref[...]) + _bdot(x2_ref[...], wo2_ref[...])
    xn = _rms(h1, gf_ref[...]).astype(BF16)
    o_ref[...] = h1
    d_ff = wu_ref.shape[1]
    for c in range(d_ff // ff_chunk):
        u = jnp.maximum(_bdot(xn, wu_ref[:, c * ff_chunk:(c + 1) * ff_chunk]), 0.0)
        o_ref[...] += _bdot((u * u).astype(BF16), wd_ref[c * ff_chunk:(c + 1) * ff_chunk, :])
    if final_norm:
        o_ref[...] = _rms(o_ref[...], gl_ref[...])


def _mix_mlp(h, x1, x2, wo1, wo2, gf, wu, wd, gl, final_norm):
    rows, d = h.shape
    tm = _row_tile(rows)
    d_ff = wu.shape[1]
    row = lambda width: pl.BlockSpec((tm, width), lambda i: (i, 0))
    return pl.pallas_call(
        functools.partial(_mix_mlp_kernel, ff_chunk=min(d_ff, 1024), final_norm=final_norm),
        grid=(rows // tm,),
        in_specs=[row(d), row(x1.shape[1]), row(x2.shape[1]), _const_spec(wo1.shape), _const_spec(wo2.shape),
                  _const_spec((1, d)), _const_spec(wu.shape), _const_spec(wd.shape), _const_spec((1, d))],
        out_specs=row(d),
        out_shape=jax.ShapeDtypeStruct((rows, d), F32),
        compiler_params=_params("arbitrary"),
        name="mix_mlp",
    )(h, x1, x2, wo1, wo2, gf, wu, wd, gl)


def _in_odd_kernel(x_ref, g_ref, w_ref, lng_ref, lnb_ref, cos_ref, sin_ref,
                   u_ref, vn_ref, q_ref, k_ref, rv_ref, rg_ref):
    xn = _rms(x_ref[...], g_ref[...]).astype(BF16)
    proj = _bdot(xn, w_ref[...])
    gd, rd = GMLP_DIM, RET_DIM
    u_ref[...] = proj[:, :gd]
    gv = proj[:, gd:2 * gd]
    mu = jnp.mean(gv, axis=-1, keepdims=True)
    var = jnp.mean((gv - mu) ** 2, axis=-1, keepdims=True)
    vn_ref[...] = (gv - mu) * lax.rsqrt(var + EPS) * lng_ref[...] + lnb_ref[...]
    cos = cos_ref[...]
    sin = sin_ref[...]
    o_q, o_k, o_v, o_g = 2 * gd, 2 * gd + rd, 2 * gd + 2 * rd, 2 * gd + 3 * rd
    for hh in range(RET_HEADS):
        sl = slice(hh * RET_HEAD_DIM, (hh + 1) * RET_HEAD_DIM)
        qh = proj[:, o_q + hh * RET_HEAD_DIM:o_q + (hh + 1) * RET_HEAD_DIM]
        kh = proj[:, o_k + hh * RET_HEAD_DIM:o_k + (hh + 1) * RET_HEAD_DIM]
        q_ref[:, sl] = (qh * cos + pltpu.roll(qh, RET_HEAD_DIM // 2, 1) * sin).astype(BF16)
        k_ref[:, sl] = (kh * cos + pltpu.roll(kh, RET_HEAD_DIM // 2, 1) * sin) * (RET_HEAD_DIM ** -0.5)
    rv_ref[...] = proj[:, o_v:o_g].astype(BF16)
    rg_ref[...] = proj[:, o_g:]


def _in_odd(x, g, w, lng, lnb, cos_t, sin_t, seq_len):
    rows, d = x.shape
    tm = _row_tile(rows)
    if seq_len >= tm:
        tiles_per_seq = seq_len // tm
        rope_map = lambda i: (i % tiles_per_seq, 0)
    else:
        reps = tm // seq_len
        cos_t = jnp.tile(cos_t, (reps, 1))
        sin_t = jnp.tile(sin_t, (reps, 1))
        rope_map = lambda i: (0, 0)
    row = lambda width: pl.BlockSpec((tm, width), lambda i: (i, 0))
    outs = ((GMLP_DIM, F32), (GMLP_DIM, F32), (RET_DIM, BF16), (RET_DIM, F32), (RET_DIM, BF16), (RET_DIM, F32))
    return pl.pallas_call(
        _in_odd_kernel,
        grid=(rows // tm,),
        in_specs=[row(d), _const_spec((1, d)), _const_spec(w.shape), _const_spec((1, GMLP_DIM)),
                  _const_spec((1, GMLP_DIM)), pl.BlockSpec((tm, RET_HEAD_DIM), rope_map),
                  pl.BlockSpec((tm, RET_HEAD_DIM), rope_map)],
        out_specs=tuple(row(wd) for wd, _ in outs),
        out_shape=tuple(jax.ShapeDtypeStruct((rows, wd), dt) for wd, dt in outs),
        compiler_params=_params("arbitrary"),
        name="in_odd",
    )(x, g, w, lng, lnb, cos_t, sin_t)


def _gmlp_kernel(u_ref, vn_ref, ws_ref, bs_ref, o_ref, *, chunk):
    vn = vn_ref[0].astype(BF16)
    r = lax.broadcasted_iota(jnp.int32, (chunk, chunk), 0)
    c = lax.broadcasted_iota(jnp.int32, (chunk, chunk), 1)
    for gi in range(GMLP_GROUPS):
        sl = slice(gi * LANES, (gi + 1) * LANES)
        ws = jnp.where(c <= r, ws_ref[gi], 0.0).astype(BF16)
        z = _bdot(ws, vn[:, sl]) + bs_ref[gi]
        o_ref[0, :, sl] = (u_ref[0, :, sl] * z).astype(o_ref.dtype)


def _gmlp_mix(u, vn, ws, bs):
    b, t, _ = u.shape
    chunk = ws.shape[-1]
    blk = lambda: pl.BlockSpec((1, chunk, GMLP_DIM), lambda bi, ci: (bi, ci, 0))
    return pl.pallas_call(
        functools.partial(_gmlp_kernel, chunk=chunk),
        grid=(b, t // chunk),
        in_specs=[blk(), blk(), _const_spec(ws.shape), _const_spec(bs.shape)],
        out_specs=blk(),
        out_shape=jax.ShapeDtypeStruct((b, t, GMLP_DIM), BF16),
        compiler_params=_params("arbitrary", "arbitrary"),
        name="gmlp_mix",
    )(u, vn, ws, bs)


def _ret_kernel(q_ref, k_ref, v_ref, g_ref, s0_ref, dintra_ref, dq_ref, dk_ref, ds_ref, gn_ref,
                o_ref, sout_ref, st_sc):
    c = pl.program_id(1)

    @pl.when(c == 0)
    def _():
        st_sc[...] = s0_ref[0]

    for hh in range(RET_HEADS):
        sl = slice(hh * RET_HEAD_DIM, (hh + 1) * RET_HEAD_DIM)
        q = q_ref[0, :, sl]
        kf = k_ref[0, :, sl]
        v = v_ref[0, :, sl]
        st = st_sc[hh]
        sc = _bdot_nt(q, kf.astype(BF16)) * dintra_ref[hh]
        o = _bdot(sc.astype(BF16), v) + _bdot(q, st.astype(BF16)) * dq_ref[hh]
        st_sc[hh] = st * ds_ref[hh] + _bdot_tn((kf * dk_ref[hh]).astype(BF16), v)
        mu = jnp.mean(o, axis=-1, keepdims=True)
        var = jnp.mean((o - mu) ** 2, axis=-1, keepdims=True)
        on = (o - mu) * lax.rsqrt(var + EPS) * gn_ref[:, sl]
        gate = g_ref[0, :, sl]
        o_ref[0, :, sl] = (on * (gate * jax.nn.sigmoid(gate))).astype(o_ref.dtype)

    @pl.when(c == pl.num_programs(1) - 1)
    def _():
        sout_ref[0] = st_sc[...]


def _retention(q, k, v, gate, s0, gn):
    b, t, _ = q.shape
    lc = math.gcd(t, RET_CHUNK) if t >= RET_CHUNK else t
    log_g = jnp.log1p(-jnp.exp2(-5.0 - jnp.arange(RET_HEADS, dtype=F32)))
    idx = jnp.arange(lc, dtype=F32)
    diff = idx[:, None] - idx[None, :]
    dintra = jnp.where(diff[None] >= 0, jnp.exp(diff[None] * log_g[:, None, None]), 0.0)
    lanes = (RET_HEADS, lc, RET_HEAD_DIM)
    dq = jnp.broadcast_to(jnp.exp((idx[None, :] + 1.0) * log_g[:, None])[:, :, None], lanes)
    dk = jnp.broadcast_to(jnp.exp((lc - 1.0 - idx)[None, :] * log_g[:, None])[:, :, None], lanes)
    ds = jnp.broadcast_to(jnp.exp(lc * log_g)[:, None, None], (RET_HEADS, 1, RET_HEAD_DIM))
    blk = lambda: pl.BlockSpec((1, lc, RET_DIM), lambda bi, ci: (bi, ci, 0))
    st = lambda: pl.BlockSpec((1, RET_HEADS, RET_HEAD_DIM, RET_HEAD_DIM), lambda bi, ci: (bi, 0, 0, 0))
    return pl.pallas_call(
        _ret_kernel,
        grid=(b, t // lc),
        in_specs=[blk(), blk(), blk(), blk(), st(), _const_spec(dintra.shape), _const_spec(lanes),
                  _const_spec(lanes), _const_spec(ds.shape), _const_spec((1, RET_DIM))],
        out_specs=(blk(), st()),
        out_shape=(jax.ShapeDtypeStruct((b, t, RET_DIM), BF16),
                   jax.ShapeDtypeStruct((b, RET_HEADS, RET_HEAD_DIM, RET_HEAD_DIM), F32)),
        scratch_shapes=[pltpu.VMEM((RET_HEADS, RET_HEAD_DIM, RET_HEAD_DIM), F32)],
        compiler_params=_params("arbitrary", "arbitrary"),
        name="retention",
    )(q, k, v, gate, s0, dintra, dq, dk, ds, gn)


def _rope_tables(start_pos, t):
    half = RET_HEAD_DIM // 2
    pos = (start_pos + jnp.arange(t)).astype(F32)
    inv = ROPE_BASE ** (-jnp.arange(half, dtype=F32) / half)
    ang = pos[:, None] * inv[None, :]
    cos, sin = jnp.cos(ang), jnp.sin(ang)
    return jnp.concatenate([cos, cos], axis=1), jnp.concatenate([-sin, sin], axis=1)


def _block_diag(w):
    g, n, _ = w.shape
    out = jnp.zeros((g * n, g * n), w.dtype)
    for i in range(g):
        out = out.at[i * n:(i + 1) * n, i * n:(i + 1) * n].set(w[i])
    return out


def _run_group(x, start_pos, hist0, state0, attend, wts):
    b, t, d = x.shape
    rows = b * t
    h = x.reshape(rows, d)
    row1 = lambda a: a.reshape(1, -1)

    a, q, k, v, kb, vb, lf, cum, cumt = _in_even(h, row1(wts["norm_mix"][0]), wts["w_in_even"], wts["b_forget"], t)
    pool_out, new_hist = _pool_mix(a.reshape(b, t, POOL_DIM), hist0, wts["pool_w"], wts["pool_scale"], start_pos)
    att = attend(q.reshape(b, t, FOX_DIM), kb.reshape(b, t, FOX_DIM), vb.reshape(b, t, FOX_DIM), cum, cumt)
    h = _mix_mlp(h, pool_out.reshape(rows, POOL_DIM), att.reshape(rows, FOX_DIM),
                 wts["w_out_even"][:POOL_DIM], wts["w_out_even"][POOL_DIM:], row1(wts["norm_ffn"][0]),
                 wts["w_up"][0], wts["w_down"][0], row1(wts["norm_final"]), False)

    cos_t, sin_t = _rope_tables(start_pos, t)
    u, vn, rq, rk, rv, rg = _in_odd(h, row1(wts["norm_mix"][1]), wts["w_in_odd"], row1(wts["gmlp_ln_g"]),
                                    row1(wts["gmlp_ln_b"]), cos_t, sin_t, t)
    chunk = min(t, GMLP_CHUNK)
    assert t % chunk == 0
    ws = wts["gmlp_ws"][:, :chunk, :chunk]
    bs = jnp.broadcast_to(wts["gmlp_bs"][:, :chunk, None], (GMLP_GROUPS, chunk, LANES))
    r3 = lambda z, wd: z.reshape(b, t, wd)
    c_out = _gmlp_mix(r3(u, GMLP_DIM), r3(vn, GMLP_DIM), ws, bs)
    r_out, s_new = _retention(r3(rq, RET_DIM), r3(rk, RET_DIM), r3(rv, RET_DIM), r3(rg, RET_DIM), state0,
                              row1(wts["ret_gn_g"]))
    y = _mix_mlp(h, c_out.reshape(rows, GMLP_DIM), r_out.reshape(rows, RET_DIM),
                 wts["w_out_odd"][:GMLP_DIM], wts["w_out_odd"][GMLP_DIM:], row1(wts["norm_ffn"][1]),
                 wts["w_up"][1], wts["w_down"][1], row1(wts["norm_final"]), True)

    return (y.reshape(b, t, d), new_hist[None],
            k.reshape(1, b, t, FOX_HEADS, FOX_HEAD_DIM), v.reshape(1, b, t, FOX_HEADS, FOX_HEAD_DIM),
            lf[:, :FOX_HEADS].reshape(1, b, t, FOX_HEADS), vn.reshape(1, b, t, GMLP_DIM), s_new[None])


def kernel(x_prompt, x_sample, state_pool, cache_k, cache_v, cache_logf, page_table, state_ret, norm_mix, w_in_even, b_forget, pool_w, pool_scale, w_out_even, w_in_odd, gmlp_ln_g, gmlp_ln_b, gmlp_ws, gmlp_bs, ret_gn_g, w_out_odd, norm_ffn, w_up, w_down, norm_final):
    assert norm_mix.shape[0] == 2 and w_in_even.shape[0] == 1 and w_in_odd.shape[0] == 1
    d = x_prompt.shape[-1]
    ev_cols = w_in_even.shape[-1]
    ev_pad = POOL_DIM + 3 * FOX_DIM + LANES - ev_cols
    wts = {
        "norm_mix": norm_mix, "norm_ffn": norm_ffn, "norm_final": norm_final,
        "w_in_even": jnp.pad(w_in_even[0], ((0, 0), (0, ev_pad))).astype(BF16),
        "b_forget": jnp.pad(b_forget[0], (0, LANES - FOX_HEADS)).reshape(1, LANES),
        "pool_w": _block_diag(pool_w[0]).astype(BF16),
        "pool_scale": pool_scale[0].reshape(1, POOL_DIM),
        "w_out_even": w_out_even[0].astype(BF16),
        "w_in_odd": w_in_odd[0].astype(BF16),
        "gmlp_ln_g": gmlp_ln_g[0], "gmlp_ln_b": gmlp_ln_b[0], "gmlp_ws": gmlp_ws[0], "gmlp_bs": gmlp_bs[0],
        "ret_gn_g": ret_gn_g[0],
        "w_out_odd": w_out_odd[0].astype(BF16),
        "w_up": w_up.astype(BF16), "w_down": w_down.astype(BF16),
    }
    b, s, _ = x_prompt.shape
    db, t, _ = x_sample.shape
    n_pages = page_table.shape[1]
    page = cache_k.shape[2]
    past = n_pages * page

    def attend_prompt(q, kb, vb, cum, cumt):
        return _fox_prompt(q, kb, vb, cum.reshape(b, s, LANES), cumt)

    cache_kt = jnp.transpose(cache_k[0], (0, 2, 3, 1))
    cache_vt = jnp.transpose(cache_v[0], (0, 2, 3, 1))
    n_phys = cache_kt.shape[0]
    suffix = _page_suffix(jnp.transpose(cache_logf[0], (2, 0, 1)))

    def attend_sample(q, kb, vb, cum, cumt):
        grow = jnp.transpose(cum.reshape(db, t, LANES), (0, 2, 1))[:, :HEAD_ROWS, :]
        gq = jnp.broadcast_to(grow[:, :FOX_HEADS, :, None], (db, FOX_HEADS, t, LANES)).reshape(db, FOX_HEADS * t, LANES)
        return _fox_sample(page_table, q, kb, vb, gq, grow, cache_kt, cache_vt, suffix)

    y_p, pool_p, k_p, v_p, lf_p, _, ret_p = _run_group(
        x_prompt, 0, jnp.zeros((b, POOL_HIST, POOL_DIM), F32),
        jnp.zeros((b, RET_HEADS, RET_HEAD_DIM, RET_HEAD_DIM), F32), attend_prompt, wts)
    y_s, pool_s, k_s, v_s, lf_s, gv_s, ret_s = _run_group(
        x_sample, past, state_pool[0], state_ret[0], attend_sample, wts)
    return (y_p, y_s, pool_p, k_p, v_p, lf_p, ret_p, pool_s, k_s, v_s, lf_s, gv_s, ret_s)
```

```python
import functools
import math

import numpy as np
import jax
import jax.numpy as jnp
from jax import lax
from jax.experimental import pallas as pl
from jax.experimental.pallas import tpu as pltpu

F32 = jnp.float32
BF16 = jnp.bfloat16
EPS = 1e-6

POOL_WINDOWS = (2, 4, 8, 16)
POOL_GROUP_DIM = 64
POOL_DIM = 256
POOL_HIST = 15
FOX_HEAD_DIM = 64
FOX_HEADS = 12
FOX_DIM = FOX_HEADS * FOX_HEAD_DIM
GMLP_CHUNK = 128
GMLP_GROUPS = 4
GMLP_DIM = 512
RET_HEADS = 4
RET_DIM = 512
RET_HEAD_DIM = 128
RET_CHUNK = 128
ROPE_BASE = 10000.0

LANES = 128
SUBLANES = 8
HEAD_ROWS = 16
NEG_BIG = -1e30
LOG2E = math.log2(math.e)
VMEM_LIMIT = 56 * 1024 * 1024
ROW_TILE = 512
ATT_TILE = 512
PAGES_PER_STEP = 8
RET_BATCH_BLOCK = 8
SUFFIX_PAGE_TILE = 256


def _row_tile(rows):
    return ROW_TILE if rows % ROW_TILE == 0 else rows


def _params(*sem):
    return pltpu.CompilerParams(dimension_semantics=sem, vmem_limit_bytes=VMEM_LIMIT)


def _const_spec(shape):
    nd = len(shape)
    return pl.BlockSpec(shape, lambda *_: (0,) * nd, pipeline_mode=pl.Buffered(1))


def _rms(x, g):
    return x * lax.rsqrt(jnp.mean(x * x, axis=-1, keepdims=True) + EPS) * g


def _bdot(a, b):
    return jnp.dot(a, b, preferred_element_type=F32)


def _bdot_nt(a, b):
    return lax.dot_general(a, b, (((1,), (1,)), ((), ())), preferred_element_type=F32)


def _bdot_tn(a, b):
    return lax.dot_general(a, b, (((0,), (0,)), ((), ())), preferred_element_type=F32)


def _split3(x):
    hi = x.astype(BF16)
    r = x - hi.astype(F32)
    mid = r.astype(BF16)
    lo = (r - mid.astype(F32)).astype(BF16)
    return hi, mid, lo


def _log_sigmoid(x):
    return jnp.minimum(x, 0.0) - jnp.log1p(jnp.exp(-jnp.abs(x)))


def _aug_tables():
    pq = np.zeros((3 * LANES, FOX_DIM), np.float32)
    pk = np.zeros((3 * LANES, FOX_DIM), np.float32)
    cq = np.zeros((1, FOX_DIM), np.float32)
    ck = np.zeros((1, FOX_DIM), np.float32)
    for h in range(FOX_HEADS):
        base = (h // 2) * LANES + (FOX_HEAD_DIM if h % 2 == 0 else 0)
        for j in range(3):
            pq[j * LANES + h, base + j] = 1.0
            ck[0, base + j] = 1.0
            cq[0, base + 3 + j] = 1.0
            pk[j * LANES + h, base + 3 + j] = -1.0
    return jnp.asarray(pq, BF16), jnp.asarray(pk, BF16), jnp.asarray(cq), jnp.asarray(ck)


def _in_even_kernel(x_ref, g_ref, w_ref, bf_ref, tri_ref, pq_ref, pk_ref, cq_ref, ck_ref,
                    a_ref, q_ref, k_ref, v_ref, kb_ref, vb_ref, lf_ref, cum_ref, cumt_ref, aq_ref, ak_ref,
                    carry_ref, *, tiles_per_seq, transposed_kv):
    i = pl.program_id(0)
    xn = _rms(x_ref[...], g_ref[...]).astype(BF16)
    proj = _bdot(xn, w_ref[...])
    o_q, o_k, o_v, o_f = POOL_DIM, POOL_DIM + FOX_DIM, POOL_DIM + 2 * FOX_DIM, POOL_DIM + 3 * FOX_DIM
    a_ref[...] = proj[:, :o_q]
    q_ref[...] = (proj[:, o_q:o_k] * (FOX_HEAD_DIM ** -0.5 * LOG2E)).astype(BF16)
    k = proj[:, o_k:o_v]
    v = proj[:, o_v:o_f]
    if transposed_kv:
        k_ref[0] = k.T
        v_ref[0] = v.T
    else:
        k_ref[...] = k
        v_ref[...] = v
    kb_ref[...] = k.astype(BF16)
    vb_ref[...] = v.astype(BF16)
    lf = _log_sigmoid(proj[:, o_f:] + bf_ref[...])
    lf_ref[...] = lf
    tri = tri_ref[...]
    cum = sum(_bdot(tri, p) for p in _split3(lf))
    if tiles_per_seq > 1:
        @pl.when(i % tiles_per_seq == 0)
        def _():
            carry_ref[...] = jnp.zeros_like(carry_ref)
        cum = cum + carry_ref[...]
        carry_ref[...] = cum[-1:, :]
    cum = cum * LOG2E
    cum_ref[...] = cum
    cumt_ref[0] = cum.T[:HEAD_ROWS, :]
    parts = jnp.concatenate(_split3(cum), axis=1)
    aq_ref[...] = (_bdot(parts, pq_ref[...]) + cq_ref[...]).astype(BF16)
    ak_ref[...] = (_bdot(parts, pk_ref[...]) + ck_ref[...]).astype(BF16)


def _in_even(x, g, w, bf, seq_len, transposed_kv):
    rows, d = x.shape
    tm = _row_tile(rows)
    n_tiles = rows // tm
    r = np.arange(tm)
    if seq_len >= tm:
        assert seq_len % tm == 0
        tiles_per_seq = seq_len // tm
        tri = r[None, :] <= r[:, None]
        cumt_shape = (rows // seq_len, HEAD_ROWS, seq_len)
        cumt_map = lambda i: (i // tiles_per_seq, 0, i % tiles_per_seq)
    else:
        assert tm % seq_len == 0 and not transposed_kv
        tiles_per_seq = 1
        tri = (r[None, :] <= r[:, None]) & ((r[None, :] // seq_len) == (r[:, None] // seq_len))
        cumt_shape = (n_tiles, HEAD_ROWS, tm)
        cumt_map = lambda i: (i, 0, 0)
    tri = jnp.asarray(tri, BF16)
    pq, pk, cq, ck = _aug_tables()
    n = w.shape[1]
    row = lambda width: pl.BlockSpec((tm, width), lambda i: (i, 0))
    if transposed_kv:
        kv_shape = jax.ShapeDtypeStruct((rows // seq_len, FOX_DIM, seq_len), F32)
        kv_spec = lambda: pl.BlockSpec((1, FOX_DIM, tm), cumt_map)
    else:
        kv_shape = jax.ShapeDtypeStruct((rows, FOX_DIM), F32)
        kv_spec = lambda: row(FOX_DIM)
    out_shape = (
        jax.ShapeDtypeStruct((rows, POOL_DIM), F32),
        jax.ShapeDtypeStruct((rows, FOX_DIM), BF16),
        kv_shape,
        kv_shape,
        jax.ShapeDtypeStruct((rows, FOX_DIM), BF16),
        jax.ShapeDtypeStruct((rows, FOX_DIM), BF16),
        jax.ShapeDtypeStruct((rows, LANES), F32),
        jax.ShapeDtypeStruct((rows, LANES), F32),
        jax.ShapeDtypeStruct(cumt_shape, F32),
        jax.ShapeDtypeStruct((rows, FOX_DIM), BF16),
        jax.ShapeDtypeStruct((rows, FOX_DIM), BF16),
    )
    out_specs = (row(POOL_DIM), row(FOX_DIM), kv_spec(), kv_spec(), row(FOX_DIM), row(FOX_DIM),
                 row(LANES), row(LANES), pl.BlockSpec((1, HEAD_ROWS, tm), cumt_map), row(FOX_DIM), row(FOX_DIM))
    return pl.pallas_call(
        functools.partial(_in_even_kernel, tiles_per_seq=tiles_per_seq, transposed_kv=transposed_kv),
        grid=(n_tiles,),
        in_specs=[row(d), _const_spec((1, d)), _const_spec((d, n)), _const_spec((1, LANES)),
                  _const_spec((tm, tm)), _const_spec(pq.shape), _const_spec(pk.shape), _const_spec(cq.shape),
                  _const_spec(ck.shape)],
        out_specs=out_specs,
        out_shape=out_shape,
        scratch_shapes=[pltpu.VMEM((1, LANES), F32)],
        compiler_params=_params("arbitrary"),
        name="in_even",
    )(x, g, w, bf, tri, pq, pk, cq, ck)


def _pool_kernel(a_ref, hist_ref, w_ref, scale_ref, out_ref, newhist_ref, ext_ref, *, seq_len, start_pos):
    t = seq_len
    base = POOL_HIST + 1
    x0 = a_ref[0]
    ext_ref[0:1, :] = jnp.zeros((1, POOL_DIM), F32)
    ext_ref[1:base, :] = hist_ref[0]
    ext_ref[base:base + t, :] = x0

    def back(kk):
        return ext_ref[base - kk:base - kk + t, :]

    sums = []
    run = x0
    nxt = 1
    for w in POOL_WINDOWS:
        while nxt < w:
            run = run + back(nxt)
            nxt += 1
        sums.append(run)
    grp = lax.broadcasted_iota(jnp.int32, (1, POOL_DIM), 1) // POOL_GROUP_DIM
    sel = sums[-1]
    win = jnp.full((1, POOL_DIM), POOL_WINDOWS[-1], jnp.int32)
    for gi in range(len(POOL_WINDOWS) - 2, -1, -1):
        sel = jnp.where(grp == gi, sums[gi], sel)
        win = jnp.where(grp == gi, POOL_WINDOWS[gi], win)
    pos = start_pos + lax.broadcasted_iota(jnp.int32, (t, 1), 0)
    cnt = jnp.minimum(pos + 1, win).astype(F32)
    dd = sel / cnt - x0
    y = _bdot(dd.astype(BF16), w_ref[...]) * scale_ref[...]
    out_ref[0] = y.astype(out_ref.dtype)
    newhist_ref[0] = ext_ref[t + 1:t + base, :]


def _pool_mix(a, hist, w_bd, scale, start_pos):
    b, t, _ = a.shape
    return pl.pallas_call(
        functools.partial(_pool_kernel, seq_len=t, start_pos=start_pos),
        grid=(b,),
        in_specs=[pl.BlockSpec((1, t, POOL_DIM), lambda i: (i, 0, 0)),
                  pl.BlockSpec((1, POOL_HIST, POOL_DIM), lambda i: (i, 0, 0)),
                  _const_spec((POOL_DIM, POOL_DIM)), _const_spec((1, POOL_DIM))],
        out_specs=(pl.BlockSpec((1, t, POOL_DIM), lambda i: (i, 0, 0)),
                   pl.BlockSpec((1, POOL_HIST, POOL_DIM), lambda i: (i, 0, 0))),
        out_shape=(jax.ShapeDtypeStruct((b, t, POOL_DIM), BF16),
                   jax.ShapeDtypeStruct((b, POOL_HIST, POOL_DIM), F32)),
        scratch_shapes=[pltpu.VMEM((POOL_HIST + 1 + t, POOL_DIM), F32)],
        compiler_params=_params("arbitrary"),
        name="pool_mix",
    )(a, hist, w_bd, scale)


def _fox_prompt_kernel(q_ref, k_ref, v_ref, aq_ref, ak_ref, o_ref, *, tile):
    s_len = q_ref.shape[1]
    lane = lax.broadcasted_iota(jnp.int32, (1, LANES), 1)
    in_head = (lane < FOX_HEAD_DIM, lane >= FOX_HEAD_DIM)
    visible = (lax.broadcasted_iota(jnp.int32, (tile, tile), 1) <= lax.broadcasted_iota(jnp.int32, (tile, tile), 0))
    km = []
    for ki in range(s_len // tile):
        rows = slice(ki * tile, (ki + 1) * tile)
        km.append([jnp.where(in_head[e], k_ref[0, rows, :], ak_ref[0, rows, :]) for e in range(2)])
    for qi in range(s_len // tile):
        qrows = slice(qi * tile, (qi + 1) * tile)
        outs = []
        for e in range(2):
            qm = jnp.where(in_head[e], q_ref[0, qrows, :], aq_ref[0, qrows, :])
            m = jnp.full((tile, 1), NEG_BIG, F32)
            l = jnp.zeros((tile, 1), F32)
            acc = jnp.zeros((tile, LANES), F32)
            for ki in range(qi + 1):
                s = _bdot_nt(qm, km[ki][e])
                if ki == qi:
                    s = jnp.where(visible, s, NEG_BIG)
                m_new = jnp.maximum(m, jnp.max(s, axis=-1, keepdims=True))
                alpha = jnp.exp2(m - m_new)
                p = jnp.exp2(s - m_new)
                l = alpha * l + jnp.sum(p, axis=-1, keepdims=True)
                acc = alpha * acc + _bdot(p.astype(BF16), v_ref[0, ki * tile:(ki + 1) * tile, :])
                m = m_new
            outs.append(acc / l)
        o_ref[0, qrows, :] = jnp.where(in_head[0], outs[0], outs[1]).astype(o_ref.dtype)


def _fox_prompt(q, kb, vb, aq, ak):
    b, s, _ = q.shape
    tile = ATT_TILE if s % ATT_TILE == 0 else s
    assert s // tile <= 4, "the causal tile structure is unrolled in the kernel"
    full = lambda: pl.BlockSpec((1, s, LANES), lambda bi, hp: (bi, 0, hp))
    return pl.pallas_call(
        functools.partial(_fox_prompt_kernel, tile=tile),
        grid=(b, FOX_HEADS // 2),
        in_specs=[full(), full(), full(), full(), full()],
        out_specs=full(),
        out_shape=jax.ShapeDtypeStruct((b, s, FOX_DIM), BF16),
        compiler_params=_params("arbitrary", "arbitrary"),
        name="fox_prompt",
    )(q, kb, vb, aq, ak)


def _suffix_kernel(lf_ref, mat_ref, out_ref):
    for h in range(FOX_HEADS):
        out_ref[:, h, :] = sum(_bdot(p, mat_ref[...]) for p in _split3(lf_ref[h])) * LOG2E


def _page_suffix(lf_hpp):
    heads, n_phys, page = lf_hpp.shape
    tp = SUFFIX_PAGE_TILE if n_phys % SUFFIX_PAGE_TILE == 0 else n_phys
    r = np.arange(page)
    later = r[:, None] > r[None, :]
    mat = jnp.asarray(np.concatenate([later, np.ones((page, page), bool)], axis=1), BF16)
    return pl.pallas_call(
        _suffix_kernel,
        grid=(n_phys // tp,),
        in_specs=[pl.BlockSpec((heads, tp, page), lambda i: (0, i, 0)), _const_spec((page, 2 * page))],
        out_specs=pl.BlockSpec((tp, heads, 2 * page), lambda i: (i, 0, 0)),
        out_shape=jax.ShapeDtypeStruct((n_phys, heads, 2 * page), F32),
        compiler_params=_params("arbitrary"),
        name="page_suffix",
    )(lf_hpp, mat)


def _fox_sample_kernel(pt_ref, q_ref, kn_ref, vn_ref, gq_ref, grow_ref, *rest, n_pg, t_new):
    k_refs = rest[:n_pg]
    v_refs = rest[n_pg:2 * n_pg]
    s_refs = rest[2 * n_pg:3 * n_pg]
    o_ref, m_sc, l_sc, acc_sc, run_sc = rest[3 * n_pg:]
    i = pl.program_id(1)
    last = pl.num_programs(1) - 1
    page = LANES

    @pl.when(i == 0)
    def _():
        m_sc[...] = jnp.full_like(m_sc, NEG_BIG)
        l_sc[...] = jnp.zeros_like(l_sc)
        acc_sc[...] = jnp.zeros_like(acc_sc)
        run_sc[...] = jnp.zeros_like(run_sc)

    run = run_sc[...]
    bias = [None] * n_pg
    for j in range(n_pg - 1, -1, -1):
        sf = s_refs[j][0]
        bias[j] = sf[:, :page] + run
        run = run + sf[:, page:]
    run_sc[...] = run

    t = t_new
    q = q_ref[0]
    gq = gq_ref[0]
    s_parts = []
    for h in range(FOX_HEADS):
        hs = slice(h * FOX_HEAD_DIM, (h + 1) * FOX_HEAD_DIM)
        kt = jnp.concatenate([k_refs[j][0, h] for j in range(n_pg)], axis=1).astype(BF16)
        s_parts.append(_bdot(q[:, hs], kt) + jnp.concatenate([bias[j][h:h + 1, :] for j in range(n_pg)], axis=1))
    s = jnp.concatenate(s_parts, axis=0) + jnp.concatenate([gq] * n_pg, axis=1)
    m_prev = m_sc[...]
    m_new = jnp.maximum(m_prev, jnp.max(s, axis=-1, keepdims=True))
    alpha = jnp.exp2(m_prev - m_new)
    p32 = jnp.exp2(s - jnp.concatenate([m_new] * n_pg, axis=1))
    l_sc[...] = alpha * l_sc[...] + jnp.sum(p32, axis=-1, keepdims=True)
    m_sc[...] = m_new
    pv = []
    for h in range(FOX_HEADS):
        vt = jnp.concatenate([v_refs[j][0, h] for j in range(n_pg)], axis=1).astype(BF16)
        pv.append(_bdot_nt(p32[h * t:(h + 1) * t].astype(BF16), vt))
    acc_sc[...] = alpha[:, :FOX_HEAD_DIM] * acc_sc[...] + jnp.concatenate(pv, axis=0)

    @pl.when(i == last)
    def _():
        kn = kn_ref[0]
        vn = vn_ref[0]
        grow = grow_ref[0]
        r = lax.broadcasted_iota(jnp.int32, (t, t), 0)
        c = lax.broadcasted_iota(jnp.int32, (t, t), 1)
        for h in range(FOX_HEADS):
            hs = slice(h * FOX_HEAD_DIM, (h + 1) * FOX_HEAD_DIM)
            rows = slice(h * t, (h + 1) * t)
            s = _bdot_nt(q[:, hs], kn[:, hs]) + (gq[rows, :1] - grow[h:h + 1, :])
            s = jnp.where(c <= r, s, NEG_BIG)
            m_prev = m_sc[rows, :1]
            m_new = jnp.maximum(m_prev, jnp.max(s, axis=-1, keepdims=True))
            alpha = jnp.exp2(m_prev - m_new)
            p32 = jnp.exp2(s - m_new)
            l = alpha * l_sc[rows, :1] + jnp.sum(p32, axis=-1, keepdims=True)
            acc = alpha * acc_sc[rows, :] + _bdot(p32.astype(BF16), vn[:, hs])
            o_ref[0, :, hs] = (acc / l).astype(o_ref.dtype)


def _fox_sample(page_table, q, kn, vn, gq, grow, cache_kt, cache_vt, suffix):
    b, t, _ = q.shape
    n_pages = page_table.shape[1]
    n_pg = PAGES_PER_STEP if n_pages % PAGES_PER_STEP == 0 else 1
    steps = n_pages // n_pg
    page = cache_kt.shape[-1]
    assert page == LANES

    def page_idx(bi, i, pt, j):
        return pt[bi, n_pages - n_pg * (i + 1) + j]

    tok = lambda width: pl.BlockSpec((1, t, width), lambda bi, i, pt: (bi, 0, 0))
    kv_specs = [pl.BlockSpec((1, FOX_HEADS, FOX_HEAD_DIM, page),
                             functools.partial(lambda bi, i, pt, j: (page_idx(bi, i, pt, j), 0, 0, 0), j=j))
                for j in range(n_pg)]
    sfx_specs = [pl.BlockSpec((1, FOX_HEADS, 2 * page),
                              functools.partial(lambda bi, i, pt, j: (page_idx(bi, i, pt, j), 0, 0), j=j))
                 for j in range(n_pg)]
    grid_spec = pltpu.PrefetchScalarGridSpec(
        num_scalar_prefetch=1,
        grid=(b, steps),
        in_specs=[tok(FOX_DIM), tok(FOX_DIM), tok(FOX_DIM),
                  pl.BlockSpec((1, FOX_HEADS * t, LANES), lambda bi, i, pt: (bi, 0, 0)),
                  pl.BlockSpec((1, HEAD_ROWS, t), lambda bi, i, pt: (bi, 0, 0))]
                 + kv_specs + kv_specs + sfx_specs,
        out_specs=tok(FOX_DIM),
        scratch_shapes=[pltpu.VMEM((FOX_HEADS * t, LANES), F32), pltpu.VMEM((FOX_HEADS * t, LANES), F32),
                        pltpu.VMEM((FOX_HEADS * t, FOX_HEAD_DIM), F32), pltpu.VMEM((FOX_HEADS, LANES), F32)],
    )
    return pl.pallas_call(
        functools.partial(_fox_sample_kernel, n_pg=n_pg, t_new=t),
        grid_spec=grid_spec,
        out_shape=jax.ShapeDtypeStruct((b, t, FOX_DIM), BF16),
        compiler_params=_params("arbitrary", "arbitrary"),
        name="fox_sample",
    )(page_table, q, kn, vn, gq, grow, *([cache_kt] * n_pg), *([cache_vt] * n_pg), *([suffix] * n_pg))


def _mix_mlp_kernel(h_ref, x1_ref, x2_ref, wo1_ref, wo2_ref, gf_ref, wu_ref, wd_ref, gl_ref, o_ref, *,
                    ff_chunk, final_norm):
    h1 = h_ref[...] + _bdot(x1_ref[...], wo1_ref[...]) + _bdot(x2_ref[...], wo2_ref[...])
    xn = _rms(h1, gf_ref[...]).astype(BF16)
    o_ref[...] = h1
    d_ff = wu_ref.shape[1]
    for c in range(d_ff // ff_chunk):
        u = jnp.maximum(_bdot(xn, wu_ref[:, c * ff_chunk:(c + 1) * ff_chunk]), 0.0)
        o_ref[...] += _bdot((u * u).astype(BF16), wd_ref[c * ff_chunk:(c + 1) * ff_chunk, :])
    if final_norm:
        o_ref[...] = _rms(o_ref[...], gl_ref[...])


def _mix_mlp(h, x1, x2, wo1, wo2, gf, wu, wd, gl, final_norm):
    rows, d = h.shape
    tm = _row_tile(rows)
    d_ff = wu.shape[1]
    row = lambda width: pl.BlockSpec((tm, width), lambda i: (i, 0))
    return pl.pallas_call(
        functools.partial(_mix_mlp_kernel, ff_chunk=min(d_ff, 1024), final_norm=final_norm),
        grid=(rows // tm,),
        in_specs=[row(d), row(x1.shape[1]), row(x2.shape[1]), _const_spec(wo1.shape), _const_spec(wo2.shape),
                  _const_spec((1, d)), _const_spec(wu.shape), _const_spec(wd.shape), _const_spec((1, d))],
        out_specs=row(d),
        out_shape=jax.ShapeDtypeStruct((rows, d), F32),
        compiler_params=_params("arbitrary"),
        name="mix_mlp",
    )(h, x1, x2, wo1, wo2, gf, wu, wd, gl)


def _in_odd_kernel(x_ref, g_ref, w_ref, lng_ref, lnb_ref, cos_ref, sin_ref, ws_ref, bs_ref,
                   c_ref, vn_ref, q_ref, k_ref, rv_ref, rg_ref, *, chunk):
    xn = _rms(x_ref[...], g_ref[...]).astype(BF16)
    proj = _bdot(xn, w_ref[...])
    tm = proj.shape[0]
    gd, rd = GMLP_DIM, RET_DIM
    gv = proj[:, gd:2 * gd]
    mu = jnp.mean(gv, axis=-1, keepdims=True)
    var = jnp.mean((gv - mu) ** 2, axis=-1, keepdims=True)
    vn = (gv - mu) * lax.rsqrt(var + EPS) * lng_ref[...] + lnb_ref[...]
    vn_ref[...] = vn
    vnb = vn.astype(BF16)
    r = lax.broadcasted_iota(jnp.int32, (tm, tm), 0)
    c = lax.broadcasted_iota(jnp.int32, (tm, tm), 1)
    causal_in_chunk = (c <= r) & (c >= (r // chunk) * chunk)
    for gi in range(GMLP_GROUPS):
        sl = slice(gi * LANES, (gi + 1) * LANES)
        ws = jnp.where(causal_in_chunk, ws_ref[gi], 0.0).astype(BF16)
        z = _bdot(ws, vnb[:, sl]) + bs_ref[gi]
        c_ref[:, sl] = (proj[:, sl] * z).astype(c_ref.dtype)
    cos = cos_ref[...]
    sin = sin_ref[...]
    o_q, o_k, o_v, o_g = 2 * gd, 2 * gd + rd, 2 * gd + 2 * rd, 2 * gd + 3 * rd
    for hh in range(RET_HEADS):
        sl = slice(hh * RET_HEAD_DIM, (hh + 1) * RET_HEAD_DIM)
        qh = proj[:, o_q + hh * RET_HEAD_DIM:o_q + (hh + 1) * RET_HEAD_DIM]
        kh = proj[:, o_k + hh * RET_HEAD_DIM:o_k + (hh + 1) * RET_HEAD_DIM]
        q_ref[:, sl] = (qh * cos + pltpu.roll(qh, RET_HEAD_DIM // 2, 1) * sin).astype(BF16)
        k_ref[:, sl] = (kh * cos + pltpu.roll(kh, RET_HEAD_DIM // 2, 1) * sin) * (RET_HEAD_DIM ** -0.5)
    rv_ref[...] = proj[:, o_v:o_g].astype(BF16)
    rg_ref[...] = proj[:, o_g:]


def _in_odd(x, g, w, lng, lnb, cos_t, sin_t, gmlp_ws, gmlp_bs, seq_len):
    rows, d = x.shape
    tm = _row_tile(rows)
    chunk = min(seq_len, GMLP_CHUNK)
    assert seq_len % chunk == 0 and tm % chunk == 0
    reps = tm // chunk
    ws_t = jnp.tile(gmlp_ws[:, :chunk, :chunk], (1, reps, reps))
    bs_t = jnp.broadcast_to(jnp.tile(gmlp_bs[:, :chunk], (1, reps))[:, :, None], (GMLP_GROUPS, tm, LANES))
    if seq_len >= tm:
        tiles_per_seq = seq_len // tm
        rope_map = lambda i: (i % tiles_per_seq, 0)
    else:
        reps = tm // seq_len
        cos_t = jnp.tile(cos_t, (reps, 1))
        sin_t = jnp.tile(sin_t, (reps, 1))
        rope_map = lambda i: (0, 0)
    row = lambda width: pl.BlockSpec((tm, width), lambda i: (i, 0))
    outs = ((GMLP_DIM, BF16), (GMLP_DIM, F32), (RET_DIM, BF16), (RET_DIM, F32), (RET_DIM, BF16), (RET_DIM, F32))
    return pl.pallas_call(
        functools.partial(_in_odd_kernel, chunk=chunk),
        grid=(rows // tm,),
        in_specs=[row(d), _const_spec((1, d)), _const_spec(w.shape), _const_spec((1, GMLP_DIM)),
                  _const_spec((1, GMLP_DIM)), pl.BlockSpec((tm, RET_HEAD_DIM), rope_map),
                  pl.BlockSpec((tm, RET_HEAD_DIM), rope_map), _const_spec(ws_t.shape), _const_spec(bs_t.shape)],
        out_specs=tuple(row(wd) for wd, _ in outs),
        out_shape=tuple(jax.ShapeDtypeStruct((rows, wd), dt) for wd, dt in outs),
        compiler_params=_params("arbitrary"),
        name="in_odd",
    )(x, g, w, lng, lnb, cos_t, sin_t, ws_t, bs_t)


def _ret_kernel(q_ref, k_ref, v_ref, g_ref, s0_ref, dintra_ref, dq_ref, dk_ref, ds_ref, gn_ref,
                o_ref, sout_ref, st_sc, *, batch_block):
    c = pl.program_id(1)

    @pl.when(c == 0)
    def _():
        st_sc[...] = s0_ref[...]

    for bi in range(batch_block):
        for hh in range(RET_HEADS):
            sl = slice(hh * RET_HEAD_DIM, (hh + 1) * RET_HEAD_DIM)
            q = q_ref[bi, :, sl]
            kf = k_ref[bi, :, sl]
            v = v_ref[bi, :, sl]
            st = st_sc[bi, hh]
            sc = _bdot_nt(q, kf.astype(BF16)) * dintra_ref[hh]
            o = _bdot(sc.astype(BF16), v) + _bdot(q, st.astype(BF16)) * dq_ref[hh]
            st_sc[bi, hh] = st * ds_ref[hh] + _bdot_tn((kf * dk_ref[hh]).astype(BF16), v)
            mu = jnp.mean(o, axis=-1, keepdims=True)
            var = jnp.mean((o - mu) ** 2, axis=-1, keepdims=True)
            on = (o - mu) * lax.rsqrt(var + EPS) * gn_ref[:, sl]
            gate = g_ref[bi, :, sl]
            o_ref[bi, :, sl] = (on * (gate * jax.nn.sigmoid(gate))).astype(o_ref.dtype)

    @pl.when(c == pl.num_programs(1) - 1)
    def _():
        sout_ref[...] = st_sc[...]


def _retention(q, k, v, gate, s0, gn):
    b, t, _ = q.shape
    bb = RET_BATCH_BLOCK if b % RET_BATCH_BLOCK == 0 else 1
    lc = math.gcd(t, RET_CHUNK) if t >= RET_CHUNK else t
    log_g = jnp.log1p(-jnp.exp2(-5.0 - jnp.arange(RET_HEADS, dtype=F32)))
    idx = jnp.arange(lc, dtype=F32)
    diff = idx[:, None] - idx[None, :]
    dintra = jnp.where(diff[None] >= 0, jnp.exp(diff[None] * log_g[:, None, None]), 0.0)
    lanes = (RET_HEADS, lc, RET_HEAD_DIM)
    dq = jnp.broadcast_to(jnp.exp((idx[None, :] + 1.0) * log_g[:, None])[:, :, None], lanes)
    dk = jnp.broadcast_to(jnp.exp((lc - 1.0 - idx)[None, :] * log_g[:, None])[:, :, None], lanes)
    ds = jnp.broadcast_to(jnp.exp(lc * log_g)[:, None, None], (RET_HEADS, 1, RET_HEAD_DIM))
    blk = lambda: pl.BlockSpec((bb, lc, RET_DIM), lambda bi, ci: (bi, ci, 0))
    st = lambda: pl.BlockSpec((bb, RET_HEADS, RET_HEAD_DIM, RET_HEAD_DIM), lambda bi, ci: (bi, 0, 0, 0))
    return pl.pallas_call(
        functools.partial(_ret_kernel, batch_block=bb),
        grid=(b // bb, t // lc),
        in_specs=[blk(), blk(), blk(), blk(), st(), _const_spec(dintra.shape), _const_spec(lanes),
                  _const_spec(lanes), _const_spec(ds.shape), _const_spec((1, RET_DIM))],
        out_specs=(blk(), st()),
        out_shape=(jax.ShapeDtypeStruct((b, t, RET_DIM), BF16),
                   jax.ShapeDtypeStruct((b, RET_HEADS, RET_HEAD_DIM, RET_HEAD_DIM), F32)),
        scratch_shapes=[pltpu.VMEM((bb, RET_HEADS, RET_HEAD_DIM, RET_HEAD_DIM), F32)],
        compiler_params=_params("arbitrary", "arbitrary"),
        name="retention",
    )(q, k, v, gate, s0, dintra, dq, dk, ds, gn)


def _rope_tables(start_pos, t):
    half = RET_HEAD_DIM // 2
    pos = (start_pos + jnp.arange(t)).astype(F32)
    inv = ROPE_BASE ** (-jnp.arange(half, dtype=F32) / half)
    ang = pos[:, None] * inv[None, :]
    cos, sin = jnp.cos(ang), jnp.sin(ang)
    return jnp.concatenate([cos, cos], axis=1), jnp.concatenate([-sin, sin], axis=1)


def _block_diag(w):
    g, n, _ = w.shape
    out = jnp.zeros((g * n, g * n), w.dtype)
    for i in range(g):
        out = out.at[i * n:(i + 1) * n, i * n:(i + 1) * n].set(w[i])
    return out


def _run_group(x, start_pos, hist0, state0, attend, transposed_kv, wts):
    b, t, d = x.shape
    rows = b * t
    h = x.reshape(rows, d)
    row1 = lambda a: a.reshape(1, -1)
    r3 = lambda z, wd: z.reshape(b, t, wd)

    a, q, k, v, kb, vb, lf, cum, cumt, aq, ak = _in_even(h, row1(wts["norm_mix"][0]), wts["w_in_even"],
                                                         wts["b_forget"], t, transposed_kv)
    pool_out, new_hist = _pool_mix(a.reshape(b, t, POOL_DIM), hist0, wts["pool_w"], wts["pool_scale"], start_pos)
    att = attend(r3(q, FOX_DIM), r3(kb, FOX_DIM), r3(vb, FOX_DIM), cum, r3(aq, FOX_DIM), r3(ak, FOX_DIM))
    if transposed_kv:
        heads_last = lambda z: jnp.transpose(z.reshape(b, FOX_HEADS, FOX_HEAD_DIM, t), (0, 3, 1, 2))[None]
    else:
        heads_last = lambda z: z.reshape(1, b, t, FOX_HEADS, FOX_HEAD_DIM)
    h = _mix_mlp(h, pool_out.reshape(rows, POOL_DIM), att.reshape(rows, FOX_DIM),
                 wts["w_out_even"][:POOL_DIM], wts["w_out_even"][POOL_DIM:], row1(wts["norm_ffn"][0]),
                 wts["w_up"][0], wts["w_down"][0], row1(wts["norm_final"]), False)

    cos_t, sin_t = _rope_tables(start_pos, t)
    c_out, vn, rq, rk, rv, rg = _in_odd(h, row1(wts["norm_mix"][1]), wts["w_in_odd"], row1(wts["gmlp_ln_g"]),
                                        row1(wts["gmlp_ln_b"]), cos_t, sin_t, wts["gmlp_ws"], wts["gmlp_bs"], t)
    r_out, s_new = _retention(r3(rq, RET_DIM), r3(rk, RET_DIM), r3(rv, RET_DIM), r3(rg, RET_DIM), state0,
                              row1(wts["ret_gn_g"]))
    y = _mix_mlp(h, c_out, r_out.reshape(rows, RET_DIM),
                 wts["w_out_odd"][:GMLP_DIM], wts["w_out_odd"][GMLP_DIM:], row1(wts["norm_ffn"][1]),
                 wts["w_up"][1], wts["w_down"][1], row1(wts["norm_final"]), True)

    return (y.reshape(b, t, d), new_hist[None], heads_last(k), heads_last(v),
            lf[:, :FOX_HEADS].reshape(1, b, t, FOX_HEADS), vn.reshape(1, b, t, GMLP_DIM), s_new[None])


def kernel(x_prompt, x_sample, state_pool, cache_k, cache_v, cache_logf, page_table, state_ret, norm_mix, w_in_even, b_forget, pool_w, pool_scale, w_out_even, w_in_odd, gmlp_ln_g, gmlp_ln_b, gmlp_ws, gmlp_bs, ret_gn_g, w_out_odd, norm_ffn, w_up, w_down, norm_final):
    assert norm_mix.shape[0] == 2 and w_in_even.shape[0] == 1 and w_in_odd.shape[0] == 1
    d = x_prompt.shape[-1]
    ev_cols = w_in_even.shape[-1]
    ev_pad = POOL_DIM + 3 * FOX_DIM + LANES - ev_cols
    wts = {
        "norm_mix": norm_mix, "norm_ffn": norm_ffn, "norm_final": norm_final,
        "w_in_even": jnp.pad(w_in_even[0], ((0, 0), (0, ev_pad))).astype(BF16),
        "b_forget": jnp.pad(b_forget[0], (0, LANES - FOX_HEADS)).reshape(1, LANES),
        "pool_w": _block_diag(pool_w[0]).astype(BF16),
        "pool_scale": pool_scale[0].reshape(1, POOL_DIM),
        "w_out_even": w_out_even[0].astype(BF16),
        "w_in_odd": w_in_odd[0].astype(BF16),
        "gmlp_ln_g": gmlp_ln_g[0], "gmlp_ln_b": gmlp_ln_b[0], "gmlp_ws": gmlp_ws[0], "gmlp_bs": gmlp_bs[0],
        "ret_gn_g": ret_gn_g[0],
        "w_out_odd": w_out_odd[0].astype(BF16),
        "w_up": w_up.astype(BF16), "w_down": w_down.astype(BF16),
    }
    b, s, _ = x_prompt.shape
    db, t, _ = x_sample.shape
    n_pages = page_table.shape[1]
    page = cache_k.shape[2]
    past = n_pages * page

    def attend_prompt(q, kb, vb, cum, aq, ak):
        return _fox_prompt(q, kb, vb, aq, ak)

    cache_kt = jnp.transpose(cache_k[0], (0, 2, 3, 1))
    cache_vt = jnp.transpose(cache_v[0], (0, 2, 3, 1))
    n_phys = cache_kt.shape[0]
    suffix = _page_suffix(jnp.transpose(cache_logf[0], (2, 0, 1)))

    def attend_sample(q, kb, vb, cum, aq, ak):
        grow = jnp.transpose(cum.reshape(db, t, LANES), (0, 2, 1))[:, :HEAD_ROWS, :]
        gq = jnp.broadcast_to(grow[:, :FOX_HEADS, :, None], (db, FOX_HEADS, t, LANES)).reshape(db, FOX_HEADS * t, LANES)
        return _fox_sample(page_table, q, kb, vb, gq, grow, cache_kt, cache_vt, suffix)

    y_p, pool_p, k_p, v_p, lf_p, _, ret_p = _run_group(
        x_prompt, 0, jnp.zeros((b, POOL_HIST, POOL_DIM), F32),
        jnp.zeros((b, RET_HEADS, RET_HEAD_DIM, RET_HEAD_DIM), F32), attend_prompt, True, wts)
    y_s, pool_s, k_s, v_s, lf_s, gv_s, ret_s = _run_group(
        x_sample, past, state_pool[0], state_ret[0], attend_sample, False, wts)
    return (y_p, y_s, pool_p, k_p, v_p, lf_p, ret_p, pool_s, k_s, v_s, lf_s, gv_s, ret_s)
```

```python
import functools
import math

import numpy as np
import jax
import jax.numpy as jnp
from jax import lax
from jax.experimental import pallas as pl
from jax.experimental.pallas import tpu as pltpu

F32 = jnp.float32
BF16 = jnp.bfloat16
EPS = 1e-6

POOL_WINDOWS = (2, 4, 8, 16)
POOL_GROUP_DIM = 64
POOL_DIM = 256
POOL_HIST = 15
FOX_HEAD_DIM = 64
FOX_HEADS = 12
FOX_DIM = FOX_HEADS * FOX_HEAD_DIM
GMLP_CHUNK = 128
GMLP_GROUPS = 4
GMLP_DIM = 512
RET_HEADS = 4
RET_DIM = 512
RET_HEAD_DIM = 128
RET_CHUNK = 128
ROPE_BASE = 10000.0

LANES = 128
SUBLANES = 8
HEAD_ROWS = 16
SPLIT_STRIDE = 16
NEG_BIG = -1e30
LOG2E = math.log2(math.e)
VMEM_LIMIT = 56 * 1024 * 1024
ROW_TILE = 512
ATT_TILE = 512
PAGES_PER_STEP = 16
RET_BATCH_BLOCK = 8
SUFFIX_PAGE_TILE = 256


def _row_tile(rows):
    return ROW_TILE if rows % ROW_TILE == 0 else rows


def _params(*sem):
    return pltpu.CompilerParams(dimension_semantics=sem, vmem_limit_bytes=VMEM_LIMIT)


def _const_spec(shape):
    nd = len(shape)
    return pl.BlockSpec(shape, lambda *_: (0,) * nd, pipeline_mode=pl.Buffered(1))


def _rms(x, g):
    return x * lax.rsqrt(jnp.mean(x * x, axis=-1, keepdims=True) + EPS) * g


def _bdot(a, b):
    return jnp.dot(a, b, preferred_element_type=F32)


def _bdot_nt(a, b):
    return lax.dot_general(a, b, (((1,), (1,)), ((), ())), preferred_element_type=F32)


def _bdot_tn(a, b):
    return lax.dot_general(a, b, (((0,), (0,)), ((), ())), preferred_element_type=F32)


def _split3(x):
    hi = x.astype(BF16)
    r = x - hi.astype(F32)
    mid = r.astype(BF16)
    lo = (r - mid.astype(F32)).astype(BF16)
    return hi, mid, lo


def _log_sigmoid(x):
    return jnp.minimum(x, 0.0) - jnp.log1p(jnp.exp(-jnp.abs(x)))


def _aug_tables():
    perm = np.zeros((LANES, 2 * FOX_DIM), np.float32)
    const = np.zeros((1, 2 * FOX_DIM), np.float32)
    for h in range(FOX_HEADS):
        base = (h // 2) * LANES + (FOX_HEAD_DIM if h % 2 == 0 else 0)
        for j in range(3):
            perm[SPLIT_STRIDE * j + h, base + j] = 1.0
            const[0, FOX_DIM + base + j] = 1.0
            const[0, base + 3 + j] = 1.0
            perm[SPLIT_STRIDE * j + h, FOX_DIM + base + 3 + j] = -1.0
    return jnp.asarray(perm, BF16), jnp.asarray(const)


def _in_even_kernel(x_ref, g_ref, w_ref, bf_ref, tri_ref, perm_ref, const_ref,
                    a_ref, q_ref, k_ref, v_ref, kb_ref, vb_ref, lf_ref, cum_ref, cumt_ref, aq_ref, ak_ref,
                    carry_ref, *, tiles_per_seq, transposed_kv):
    i = pl.program_id(0)
    xn = _rms(x_ref[...], g_ref[...]).astype(BF16)
    proj = _bdot(xn, w_ref[...])
    o_q, o_k, o_v, o_f = POOL_DIM, POOL_DIM + FOX_DIM, POOL_DIM + 2 * FOX_DIM, POOL_DIM + 3 * FOX_DIM
    a_ref[...] = proj[:, :o_q]
    q_ref[...] = (proj[:, o_q:o_k] * (FOX_HEAD_DIM ** -0.5 * LOG2E)).astype(BF16)
    k = proj[:, o_k:o_v]
    v = proj[:, o_v:o_f]
    if transposed_kv:
        k_ref[0] = k.T
        v_ref[0] = v.T
    else:
        k_ref[...] = k
        v_ref[...] = v
    kb_ref[...] = k.astype(BF16)
    vb_ref[...] = v.astype(BF16)
    lf = _log_sigmoid(proj[:, o_f:] + bf_ref[...])
    lf_ref[...] = lf
    tri3 = _bdot(tri_ref[...], jnp.concatenate(_split3(lf), axis=1))
    cum = tri3[:, :LANES] + tri3[:, LANES:2 * LANES] + tri3[:, 2 * LANES:]
    if tiles_per_seq > 1:
        @pl.when(i % tiles_per_seq == 0)
        def _():
            carry_ref[...] = jnp.zeros_like(carry_ref)
        cum = cum + carry_ref[...]
        carry_ref[...] = cum[-1:, :]
    cum = cum * LOG2E
    cum_ref[...] = cum
    cumt_ref[0] = cum.T[:HEAD_ROWS, :]
    is_head = lax.broadcasted_iota(jnp.int32, (1, LANES), 1) < FOX_HEADS
    packed = sum(pltpu.roll(jnp.where(is_head, p.astype(F32), 0.0), SPLIT_STRIDE * j, 1)
                 for j, p in enumerate(_split3(cum)))
    aug = (_bdot(packed.astype(BF16), perm_ref[...]) + const_ref[...]).astype(BF16)
    aq_ref[...] = aug[:, :FOX_DIM]
    ak_ref[...] = aug[:, FOX_DIM:]


def _in_even(x, g, w, bf, seq_len, transposed_kv):
    rows, d = x.shape
    tm = _row_tile(rows)
    n_tiles = rows // tm
    r = np.arange(tm)
    if seq_len >= tm:
        assert seq_len % tm == 0
        tiles_per_seq = seq_len // tm
        tri = r[None, :] <= r[:, None]
        cumt_shape = (rows // seq_len, HEAD_ROWS, seq_len)
        cumt_map = lambda i: (i // tiles_per_seq, 0, i % tiles_per_seq)
    else:
        assert tm % seq_len == 0 and not transposed_kv
        tiles_per_seq = 1
        tri = (r[None, :] <= r[:, None]) & ((r[None, :] // seq_len) == (r[:, None] // seq_len))
        cumt_shape = (n_tiles, HEAD_ROWS, tm)
        cumt_map = lambda i: (i, 0, 0)
    tri = jnp.asarray(tri, BF16)
    perm, const = _aug_tables()
    n = w.shape[1]
    row = lambda width: pl.BlockSpec((tm, width), lambda i: (i, 0))
    if transposed_kv:
        kv_shape = jax.ShapeDtypeStruct((rows // seq_len, FOX_DIM, seq_len), F32)
        kv_spec = lambda: pl.BlockSpec((1, FOX_DIM, tm), cumt_map)
    else:
        kv_shape = jax.ShapeDtypeStruct((rows, FOX_DIM), F32)
        kv_spec = lambda: row(FOX_DIM)
    out_shape = (
        jax.ShapeDtypeStruct((rows, POOL_DIM), F32),
        jax.ShapeDtypeStruct((rows, FOX_DIM), BF16),
        kv_shape,
        kv_shape,
        jax.ShapeDtypeStruct((rows, FOX_DIM), BF16),
        jax.ShapeDtypeStruct((rows, FOX_DIM), BF16),
        jax.ShapeDtypeStruct((rows, LANES), F32),
        jax.ShapeDtypeStruct((rows, LANES), F32),
        jax.ShapeDtypeStruct(cumt_shape, F32),
        jax.ShapeDtypeStruct((rows, FOX_DIM), BF16),
        jax.ShapeDtypeStruct((rows, FOX_DIM), BF16),
    )
    out_specs = (row(POOL_DIM), row(FOX_DIM), kv_spec(), kv_spec(), row(FOX_DIM), row(FOX_DIM),
                 row(LANES), row(LANES), pl.BlockSpec((1, HEAD_ROWS, tm), cumt_map), row(FOX_DIM), row(FOX_DIM))
    return pl.pallas_call(
        functools.partial(_in_even_kernel, tiles_per_seq=tiles_per_seq, transposed_kv=transposed_kv),
        grid=(n_tiles,),
        in_specs=[row(d), _const_spec((1, d)), _const_spec((d, n)), _const_spec((1, LANES)),
                  _const_spec((tm, tm)), _const_spec(perm.shape), _const_spec(const.shape)],
        out_specs=out_specs,
        out_shape=out_shape,
        scratch_shapes=[pltpu.VMEM((1, LANES), F32)],
        compiler_params=_params("arbitrary"),
        name="in_even",
    )(x, g, w, bf, tri, perm, const)


def _pool_kernel(a_ref, hist_ref, w_ref, scale_ref, out_ref, newhist_ref, ext_ref, *, seq_len, start_pos):
    t = seq_len
    base = POOL_HIST + 1
    x0 = a_ref[0]
    ext_ref[0:1, :] = jnp.zeros((1, POOL_DIM), F32)
    ext_ref[1:base, :] = hist_ref[0]
    ext_ref[base:base + t, :] = x0

    def back(kk):
        return ext_ref[base - kk:base - kk + t, :]

    sums = []
    run = x0
    nxt = 1
    for w in POOL_WINDOWS:
        while nxt < w:
            run = run + back(nxt)
            nxt += 1
        sums.append(run)
    grp = lax.broadcasted_iota(jnp.int32, (1, POOL_DIM), 1) // POOL_GROUP_DIM
    sel = sums[-1]
    win = jnp.full((1, POOL_DIM), POOL_WINDOWS[-1], jnp.int32)
    for gi in range(len(POOL_WINDOWS) - 2, -1, -1):
        sel = jnp.where(grp == gi, sums[gi], sel)
        win = jnp.where(grp == gi, POOL_WINDOWS[gi], win)
    pos = start_pos + lax.broadcasted_iota(jnp.int32, (t, 1), 0)
    cnt = jnp.minimum(pos + 1, win).astype(F32)
    dd = sel / cnt - x0
    y = _bdot(dd.astype(BF16), w_ref[...]) * scale_ref[...]
    out_ref[0] = y.astype(out_ref.dtype)
    newhist_ref[0] = ext_ref[t + 1:t + base, :]


def _pool_mix(a, hist, w_bd, scale, start_pos):
    b, t, _ = a.shape
    return pl.pallas_call(
        functools.partial(_pool_kernel, seq_len=t, start_pos=start_pos),
        grid=(b,),
        in_specs=[pl.BlockSpec((1, t, POOL_DIM), lambda i: (i, 0, 0)),
                  pl.BlockSpec((1, POOL_HIST, POOL_DIM), lambda i: (i, 0, 0)),
                  _const_spec((POOL_DIM, POOL_DIM)), _const_spec((1, POOL_DIM))],
        out_specs=(pl.BlockSpec((1, t, POOL_DIM), lambda i: (i, 0, 0)),
                   pl.BlockSpec((1, POOL_HIST, POOL_DIM), lambda i: (i, 0, 0))),
        out_shape=(jax.ShapeDtypeStruct((b, t, POOL_DIM), BF16),
                   jax.ShapeDtypeStruct((b, POOL_HIST, POOL_DIM), F32)),
        scratch_shapes=[pltpu.VMEM((POOL_HIST + 1 + t, POOL_DIM), F32)],
        compiler_params=_params("arbitrary"),
        name="pool_mix",
    )(a, hist, w_bd, scale)


def _fox_prompt_kernel(q_ref, k_ref, v_ref, aq_ref, ak_ref, o_ref, *, tile):
    s_len = q_ref.shape[1]
    lane = lax.broadcasted_iota(jnp.int32, (1, LANES), 1)
    in_head = (lane < FOX_HEAD_DIM, lane >= FOX_HEAD_DIM)
    visible = (lax.broadcasted_iota(jnp.int32, (tile, tile), 1) <= lax.broadcasted_iota(jnp.int32, (tile, tile), 0))
    km = []
    for ki in range(s_len // tile):
        rows = slice(ki * tile, (ki + 1) * tile)
        km.append([jnp.where(in_head[e], k_ref[0, rows, :], ak_ref[0, rows, :]) for e in range(2)])
    for qi in range(s_len // tile):
        qrows = slice(qi * tile, (qi + 1) * tile)
        outs = []
        for e in range(2):
            qm = jnp.where(in_head[e], q_ref[0, qrows, :], aq_ref[0, qrows, :])
            m = jnp.full((tile, 1), NEG_BIG, F32)
            l = jnp.zeros((tile, 1), F32)
            acc = jnp.zeros((tile, LANES), F32)
            for ki in range(qi + 1):
                s = _bdot_nt(qm, km[ki][e])
                if ki == qi:
                    s = jnp.where(visible, s, NEG_BIG)
                m_new = jnp.maximum(m, jnp.max(s, axis=-1, keepdims=True))
                alpha = jnp.exp2(m - m_new)
                p = jnp.exp2(s - m_new)
                l = alpha * l + jnp.sum(p, axis=-1, keepdims=True)
                acc = alpha * acc + _bdot(p.astype(BF16), v_ref[0, ki * tile:(ki + 1) * tile, :])
                m = m_new
            outs.append(acc / l)
        o_ref[0, qrows, :] = jnp.where(in_head[0], outs[0], outs[1]).astype(o_ref.dtype)


def _fox_prompt(q, kb, vb, aq, ak):
    b, s, _ = q.shape
    tile = ATT_TILE if s % ATT_TILE == 0 else s
    assert s // tile <= 4, "the causal tile structure is unrolled in the kernel"
    full = lambda: pl.BlockSpec((1, s, LANES), lambda bi, hp: (bi, 0, hp))
    return pl.pallas_call(
        functools.partial(_fox_prompt_kernel, tile=tile),
        grid=(b, FOX_HEADS // 2),
        in_specs=[full(), full(), full(), full(), full()],
        out_specs=full(),
        out_shape=jax.ShapeDtypeStruct((b, s, FOX_DIM), BF16),
        compiler_params=_params("arbitrary", "arbitrary"),
        name="fox_prompt",
    )(q, kb, vb, aq, ak)


def _suffix_kernel(lf_ref, mat_ref, out_ref):
    for h in range(FOX_HEADS):
        out_ref[:, h, :] = sum(_bdot(p, mat_ref[...]) for p in _split3(lf_ref[h])) * LOG2E


def _page_suffix(lf_hpp):
    heads, n_phys, page = lf_hpp.shape
    tp = SUFFIX_PAGE_TILE if n_phys % SUFFIX_PAGE_TILE == 0 else n_phys
    r = np.arange(page)
    later = r[:, None] > r[None, :]
    mat = jnp.asarray(np.concatenate([later, np.ones((page, page), bool)], axis=1), BF16)
    return pl.pallas_call(
        _suffix_kernel,
        grid=(n_phys // tp,),
        in_specs=[pl.BlockSpec((heads, tp, page), lambda i: (0, i, 0)), _const_spec((page, 2 * page))],
        out_specs=pl.BlockSpec((tp, heads, 2 * page), lambda i: (i, 0, 0)),
        out_shape=jax.ShapeDtypeStruct((n_phys, heads, 2 * page), F32),
        compiler_params=_params("arbitrary"),
        name="page_suffix",
    )(lf_hpp, mat)


def _fox_sample_kernel(pt_ref, q_ref, kn_ref, vn_ref, gq_ref, grow_ref, *rest, n_pg, t_new):
    k_refs = rest[:n_pg]
    v_refs = rest[n_pg:2 * n_pg]
    s_refs = rest[2 * n_pg:3 * n_pg]
    o_ref, m_sc, l_sc, acc_sc, run_sc = rest[3 * n_pg:]
    i = pl.program_id(1)
    last = pl.num_programs(1) - 1
    page = LANES

    @pl.when(i == 0)
    def _():
        m_sc[...] = jnp.full_like(m_sc, NEG_BIG)
        l_sc[...] = jnp.zeros_like(l_sc)
        acc_sc[...] = jnp.zeros_like(acc_sc)
        run_sc[...] = jnp.zeros_like(run_sc)

    run = run_sc[...]
    bias = [None] * n_pg
    for j in range(n_pg - 1, -1, -1):
        sf = s_refs[j][0]
        bias[j] = sf[:, :page] + run
        run = run + sf[:, page:]
    run_sc[...] = run

    t = t_new
    q = q_ref[0]
    gq = gq_ref[0]
    s_parts = []
    for h in range(FOX_HEADS):
        hs = slice(h * FOX_HEAD_DIM, (h + 1) * FOX_HEAD_DIM)
        kt = jnp.concatenate([k_refs[j][0, h] for j in range(n_pg)], axis=1).astype(BF16)
        s_parts.append(_bdot(q[:, hs], kt) + jnp.concatenate([bias[j][h:h + 1, :] for j in range(n_pg)], axis=1))
    s = jnp.concatenate(s_parts, axis=0) + jnp.concatenate([gq] * n_pg, axis=1)
    m_prev = m_sc[...]
    m_new = jnp.maximum(m_prev, jnp.max(s, axis=-1, keepdims=True))
    alpha = jnp.exp2(m_prev - m_new)
    p32 = jnp.exp2(s - jnp.concatenate([m_new] * n_pg, axis=1))
    l_sc[...] = alpha * l_sc[...] + jnp.sum(p32, axis=-1, keepdims=True)
    m_sc[...] = m_new
    pv = []
    for h in range(FOX_HEADS):
        vt = jnp.concatenate([v_refs[j][0, h] for j in range(n_pg)], axis=1).astype(BF16)
        pv.append(_bdot_nt(p32[h * t:(h + 1) * t].astype(BF16), vt))
    acc_sc[...] = alpha[:, :FOX_HEAD_DIM] * acc_sc[...] + jnp.concatenate(pv, axis=0)

    @pl.when(i == last)
    def _():
        kn = kn_ref[0]
        vn = vn_ref[0]
        grow = grow_ref[0]
        r = lax.broadcasted_iota(jnp.int32, (t, t), 0)
        c = lax.broadcasted_iota(jnp.int32, (t, t), 1)
        for h in range(FOX_HEADS):
            hs = slice(h * FOX_HEAD_DIM, (h + 1) * FOX_HEAD_DIM)
            rows = slice(h * t, (h + 1) * t)
            s = _bdot_nt(q[:, hs], kn[:, hs]) + (gq[rows, :1] - grow[h:h + 1, :])
            s = jnp.where(c <= r, s, NEG_BIG)
            m_prev = m_sc[rows, :1]
            m_new = jnp.maximum(m_prev, jnp.max(s, axis=-1, keepdims=True))
            alpha = jnp.exp2(m_prev - m_new)
            p32 = jnp.exp2(s - m_new)
            l = alpha * l_sc[rows, :1] + jnp.sum(p32, axis=-1, keepdims=True)
            acc = alpha * acc_sc[rows, :] + _bdot(p32.astype(BF16), vn[:, hs])
            o_ref[0, :, hs] = (acc / l).astype(o_ref.dtype)


def _fox_sample(page_table, q, kn, vn, gq, grow, cache_kt, cache_vt, suffix):
    b, t, _ = q.shape
    n_pages = page_table.shape[1]
    n_pg = PAGES_PER_STEP if n_pages % PAGES_PER_STEP == 0 else 1
    steps = n_pages // n_pg
    page = cache_kt.shape[-1]
    assert page == LANES

    def page_idx(bi, i, pt, j):
        return pt[bi, n_pages - n_pg * (i + 1) + j]

    tok = lambda width: pl.BlockSpec((1, t, width), lambda bi, i, pt: (bi, 0, 0))
    kv_specs = [pl.BlockSpec((1, FOX_HEADS, FOX_HEAD_DIM, page),
                             functools.partial(lambda bi, i, pt, j: (page_idx(bi, i, pt, j), 0, 0, 0), j=j))
                for j in range(n_pg)]
    sfx_specs = [pl.BlockSpec((1, FOX_HEADS, 2 * page),
                              functools.partial(lambda bi, i, pt, j: (page_idx(bi, i, pt, j), 0, 0), j=j))
                 for j in range(n_pg)]
    grid_spec = pltpu.PrefetchScalarGridSpec(
        num_scalar_prefetch=1,
        grid=(b, steps),
        in_specs=[tok(FOX_DIM), tok(FOX_DIM), tok(FOX_DIM),
                  pl.BlockSpec((1, FOX_HEADS * t, LANES), lambda bi, i, pt: (bi, 0, 0)),
                  pl.BlockSpec((1, HEAD_ROWS, t), lambda bi, i, pt: (bi, 0, 0))]
                 + kv_specs + kv_specs + sfx_specs,
        out_specs=tok(FOX_DIM),
        scratch_shapes=[pltpu.VMEM((FOX_HEADS * t, LANES), F32), pltpu.VMEM((FOX_HEADS * t, LANES), F32),
                        pltpu.VMEM((FOX_HEADS * t, FOX_HEAD_DIM), F32), pltpu.VMEM((FOX_HEADS, LANES), F32)],
    )
    return pl.pallas_call(
        functools.partial(_fox_sample_kernel, n_pg=n_pg, t_new=t),
        grid_spec=grid_spec,
        out_shape=jax.ShapeDtypeStruct((b, t, FOX_DIM), BF16),
        compiler_params=_params("arbitrary", "arbitrary"),
        name="fox_sample",
    )(page_table, q, kn, vn, gq, grow, *([cache_kt] * n_pg), *([cache_vt] * n_pg), *([suffix] * n_pg))


def _mix_mlp_kernel(h_ref, x1_ref, x2_ref, wo1_ref, wo2_ref, gf_ref, wu_ref, wd_ref, gl_ref, o_ref, *,
                    ff_chunk, final_norm):
    h1 = h_ref[...] + _bdot(x1_ref[...], wo1_ref[...]) + _bdot(x2_ref[...], wo2_ref[...])
    xn = _rms(h1, gf_ref[...]).astype(BF16)
    o_ref[...] = h1
    d_ff = wu_ref.shape[1]
    for c in range(d_ff // ff_chunk):
        u = jnp.maximum(_bdot(xn, wu_ref[:, c * ff_chunk:(c + 1) * ff_chunk]), 0.0)
        o_ref[...] += _bdot((u * u).astype(BF16), wd_ref[c * ff_chunk:(c + 1) * ff_chunk, :])
    if final_norm:
        o_ref[...] = _rms(o_ref[...], gl_ref[...])


def _mix_mlp(h, x1, x2, wo1, wo2, gf, wu, wd, gl, final_norm):
    rows, d = h.shape
    tm = _row_tile(rows)
    d_ff = wu.shape[1]
    row = lambda width: pl.BlockSpec((tm, width), lambda i: (i, 0))
    return pl.pallas_call(
        functools.partial(_mix_mlp_kernel, ff_chunk=min(d_ff, 1024), final_norm=final_norm),
        grid=(rows // tm,),
        in_specs=[row(d), row(x1.shape[1]), row(x2.shape[1]), _const_spec(wo1.shape), _const_spec(wo2.shape),
                  _const_spec((1, d)), _const_spec(wu.shape), _const_spec(wd.shape), _const_spec((1, d))],
        out_specs=row(d),
        out_shape=jax.ShapeDtypeStruct((rows, d), F32),
        compiler_params=_params("arbitrary"),
        name="mix_mlp",
    )(h, x1, x2, wo1, wo2, gf, wu, wd, gl)


def _in_odd_kernel(x_ref, g_ref, w_ref, lng_ref, lnb_ref, cos_ref, sin_ref, ws_ref, bs_ref,
                   c_ref, vn_ref, q_ref, k_ref, rv_ref, rg_ref, *, chunk):
    xn = _rms(x_ref[...], g_ref[...]).astype(BF16)
    proj = _bdot(xn, w_ref[...])
    tm = proj.shape[0]
    gd, rd = GMLP_DIM, RET_DIM
    gv = proj[:, gd:2 * gd]
    mu = jnp.mean(gv, axis=-1, keepdims=True)
    var = jnp.mean((gv - mu) ** 2, axis=-1, keepdims=True)
    vn = (gv - mu) * lax.rsqrt(var + EPS) * lng_ref[...] + lnb_ref[...]
    vn_ref[...] = vn
    vnb = vn.astype(BF16)
    wrows = ws_ref.shape[1]
    r = lax.broadcasted_iota(jnp.int32, (wrows, wrows), 0)
    c = lax.broadcasted_iota(jnp.int32, (wrows, wrows), 1)
    causal_in_chunk = (c <= r) & (c >= (r // chunk) * chunk)
    for gi in range(GMLP_GROUPS):
        sl = slice(gi * LANES, (gi + 1) * LANES)
        ws = jnp.where(causal_in_chunk, ws_ref[gi], 0.0).astype(BF16)
        for ti in range(tm // wrows):
            rows = slice(ti * wrows, (ti + 1) * wrows)
            z = _bdot(ws, vnb[rows, sl]) + bs_ref[gi]
            c_ref[rows, sl] = (proj[rows, sl] * z).astype(c_ref.dtype)
    cos = cos_ref[...]
    sin = sin_ref[...]
    o_q, o_k, o_v, o_g = 2 * gd, 2 * gd + rd, 2 * gd + 2 * rd, 2 * gd + 3 * rd
    for hh in range(RET_HEADS):
        sl = slice(hh * RET_HEAD_DIM, (hh + 1) * RET_HEAD_DIM)
        qh = proj[:, o_q + hh * RET_HEAD_DIM:o_q + (hh + 1) * RET_HEAD_DIM]
        kh = proj[:, o_k + hh * RET_HEAD_DIM:o_k + (hh + 1) * RET_HEAD_DIM]
        q_ref[:, sl] = (qh * cos + pltpu.roll(qh, RET_HEAD_DIM // 2, 1) * sin).astype(BF16)
        k_ref[:, sl] = (kh * cos + pltpu.roll(kh, RET_HEAD_DIM // 2, 1) * sin) * (RET_HEAD_DIM ** -0.5)
    rv_ref[...] = proj[:, o_v:o_g].astype(BF16)
    rg_ref[...] = proj[:, o_g:]


def _in_odd(x, g, w, lng, lnb, cos_t, sin_t, gmlp_ws, gmlp_bs, seq_len):
    rows, d = x.shape
    tm = _row_tile(rows)
    chunk = min(seq_len, GMLP_CHUNK)
    assert seq_len % chunk == 0 and tm % chunk == 0
    wrows = chunk if chunk % LANES == 0 else tm
    reps = wrows // chunk
    ws_t = jnp.tile(gmlp_ws[:, :chunk, :chunk], (1, reps, reps))
    bs_t = jnp.broadcast_to(jnp.tile(gmlp_bs[:, :chunk], (1, reps))[:, :, None], (GMLP_GROUPS, wrows, LANES))
    if seq_len >= tm:
        tiles_per_seq = seq_len // tm
        rope_map = lambda i: (i % tiles_per_seq, 0)
    else:
        reps = tm // seq_len
        cos_t = jnp.tile(cos_t, (reps, 1))
        sin_t = jnp.tile(sin_t, (reps, 1))
        rope_map = lambda i: (0, 0)
    row = lambda width: pl.BlockSpec((tm, width), lambda i: (i, 0))
    outs = ((GMLP_DIM, BF16), (GMLP_DIM, F32), (RET_DIM, BF16), (RET_DIM, F32), (RET_DIM, BF16), (RET_DIM, F32))
    return pl.pallas_call(
        functools.partial(_in_odd_kernel, chunk=chunk),
        grid=(rows // tm,),
        in_specs=[row(d), _const_spec((1, d)), _const_spec(w.shape), _const_spec((1, GMLP_DIM)),
                  _const_spec((1, GMLP_DIM)), pl.BlockSpec((tm, RET_HEAD_DIM), rope_map),
                  pl.BlockSpec((tm, RET_HEAD_DIM), rope_map), _const_spec(ws_t.shape), _const_spec(bs_t.shape)],
        out_specs=tuple(row(wd) for wd, _ in outs),
        out_shape=tuple(jax.ShapeDtypeStruct((rows, wd), dt) for wd, dt in outs),
        compiler_params=_params("arbitrary"),
        name="in_odd",
    )(x, g, w, lng, lnb, cos_t, sin_t, ws_t, bs_t)


def _ret_kernel(q_ref, k_ref, v_ref, g_ref, s0_ref, dintra_ref, dq_ref, dk_ref, ds_ref, gn_ref,
                o_ref, sout_ref, st_sc, *, batch_block):
    c = pl.program_id(1)

    @pl.when(c == 0)
    def _():
        st_sc[...] = s0_ref[...]

    for bi in range(batch_block):
        for hh in range(RET_HEADS):
            sl = slice(hh * RET_HEAD_DIM, (hh + 1) * RET_HEAD_DIM)
            q = q_ref[bi, :, sl]
            kf = k_ref[bi, :, sl]
            v = v_ref[bi, :, sl]
            st = st_sc[bi, hh]
            sc = _bdot_nt(q, kf.astype(BF16)) * dintra_ref[hh]
            o = _bdot(sc.astype(BF16), v) + _bdot(q, st.astype(BF16)) * dq_ref[hh]
            st_sc[bi, hh] = st * ds_ref[hh] + _bdot_tn((kf * dk_ref[hh]).astype(BF16), v)
            mu = jnp.mean(o, axis=-1, keepdims=True)
            var = jnp.mean((o - mu) ** 2, axis=-1, keepdims=True)
            on = (o - mu) * lax.rsqrt(var + EPS) * gn_ref[:, sl]
            gate = g_ref[bi, :, sl]
            o_ref[bi, :, sl] = (on * (gate * jax.nn.sigmoid(gate))).astype(o_ref.dtype)

    @pl.when(c == pl.num_programs(1) - 1)
    def _():
        sout_ref[...] = st_sc[...]


def _retention(q, k, v, gate, s0, gn):
    b, t, _ = q.shape
    bb = RET_BATCH_BLOCK if b % RET_BATCH_BLOCK == 0 else 1
    lc = math.gcd(t, RET_CHUNK) if t >= RET_CHUNK else t
    log_g = jnp.log1p(-jnp.exp2(-5.0 - jnp.arange(RET_HEADS, dtype=F32)))
    idx = jnp.arange(lc, dtype=F32)
    diff = idx[:, None] - idx[None, :]
    dintra = jnp.where(diff[None] >= 0, jnp.exp(diff[None] * log_g[:, None, None]), 0.0)
    lanes = (RET_HEADS, lc, RET_HEAD_DIM)
    dq = jnp.broadcast_to(jnp.exp((idx[None, :] + 1.0) * log_g[:, None])[:, :, None], lanes)
    dk = jnp.broadcast_to(jnp.exp((lc - 1.0 - idx)[None, :] * log_g[:, None])[:, :, None], lanes)
    ds = jnp.broadcast_to(jnp.exp(lc * log_g)[:, None, None], (RET_HEADS, 1, RET_HEAD_DIM))
    blk = lambda: pl.BlockSpec((bb, lc, RET_DIM), lambda bi, ci: (bi, ci, 0))
    st = lambda: pl.BlockSpec((bb, RET_HEADS, RET_HEAD_DIM, RET_HEAD_DIM), lambda bi, ci: (bi, 0, 0, 0))
    return pl.pallas_call(
        functools.partial(_ret_kernel, batch_block=bb),
        grid=(b // bb, t // lc),
        in_specs=[blk(), blk(), blk(), blk(), st(), _const_spec(dintra.shape), _const_spec(lanes),
                  _const_spec(lanes), _const_spec(ds.shape), _const_spec((1, RET_DIM))],
        out_specs=(blk(), st()),
        out_shape=(jax.ShapeDtypeStruct((b, t, RET_DIM), BF16),
                   jax.ShapeDtypeStruct((b, RET_HEADS, RET_HEAD_DIM, RET_HEAD_DIM), F32)),
        scratch_shapes=[pltpu.VMEM((bb, RET_HEADS, RET_HEAD_DIM, RET_HEAD_DIM), F32)],
        compiler_params=_params("arbitrary", "arbitrary"),
        name="retention",
    )(q, k, v, gate, s0, dintra, dq, dk, ds, gn)


def _rope_tables(start_pos, t):
    half = RET_HEAD_DIM // 2
    pos = (start_pos + jnp.arange(t)).astype(F32)
    inv = ROPE_BASE ** (-jnp.arange(half, dtype=F32) / half)
    ang = pos[:, None] * inv[None, :]
    cos, sin = jnp.cos(ang), jnp.sin(ang)
    return jnp.concatenate([cos, cos], axis=1), jnp.concatenate([-sin, sin], axis=1)


def _block_diag(w):
    g, n, _ = w.shape
    out = jnp.zeros((g * n, g * n), w.dtype)
    for i in range(g):
        out = out.at[i * n:(i + 1) * n, i * n:(i + 1) * n].set(w[i])
    return out


def _run_group(x, start_pos, hist0, state0, attend, transposed_kv, wts):
    b, t, d = x.shape
    rows = b * t
    h = x.reshape(rows, d)
    row1 = lambda a: a.reshape(1, -1)
    r3 = lambda z, wd: z.reshape(b, t, wd)

    a, q, k, v, kb, vb, lf, cum, cumt, aq, ak = _in_even(h, row1(wts["norm_mix"][0]), wts["w_in_even"],
                                                         wts["b_forget"], t, transposed_kv)
    pool_out, new_hist = _pool_mix(a.reshape(b, t, POOL_DIM), hist0, wts["pool_w"], wts["pool_scale"], start_pos)
    att = attend(r3(q, FOX_DIM), r3(kb, FOX_DIM), r3(vb, FOX_DIM), cum, r3(aq, FOX_DIM), r3(ak, FOX_DIM))
    if transposed_kv:
        heads_last = lambda z: jnp.transpose(z.reshape(b, FOX_HEADS, FOX_HEAD_DIM, t), (0, 3, 1, 2))[None]
    else:
        heads_last = lambda z: z.reshape(1, b, t, FOX_HEADS, FOX_HEAD_DIM)
    h = _mix_mlp(h, pool_out.reshape(rows, POOL_DIM), att.reshape(rows, FOX_DIM),
                 wts["w_out_even"][:POOL_DIM], wts["w_out_even"][POOL_DIM:], row1(wts["norm_ffn"][0]),
                 wts["w_up"][0], wts["w_down"][0], row1(wts["norm_final"]), False)

    cos_t, sin_t = _rope_tables(start_pos, t)
    c_out, vn, rq, rk, rv, rg = _in_odd(h, row1(wts["norm_mix"][1]), wts["w_in_odd"], row1(wts["gmlp_ln_g"]),
                                        row1(wts["gmlp_ln_b"]), cos_t, sin_t, wts["gmlp_ws"], wts["gmlp_bs"], t)
    r_out, s_new = _retention(r3(rq, RET_DIM), r3(rk, RET_DIM), r3(rv, RET_DIM), r3(rg, RET_DIM), state0,
                              row1(wts["ret_gn_g"]))
    y = _mix_mlp(h, c_out, r_out.reshape(rows, RET_DIM),
                 wts["w_out_odd"][:GMLP_DIM], wts["w_out_odd"][GMLP_DIM:], row1(wts["norm_ffn"][1]),
                 wts["w_up"][1], wts["w_down"][1], row1(wts["norm_final"]), True)

    return (y.reshape(b, t, d), new_hist[None], heads_last(k), heads_last(v),
            lf[:, :FOX_HEADS].reshape(1, b, t, FOX_HEADS), vn.reshape(1, b, t, GMLP_DIM), s_new[None])


def kernel(x_prompt, x_sample, state_pool, cache_k, cache_v, cache_logf, page_table, state_ret, norm_mix, w_in_even, b_forget, pool_w, pool_scale, w_out_even, w_in_odd, gmlp_ln_g, gmlp_ln_b, gmlp_ws, gmlp_bs, ret_gn_g, w_out_odd, norm_ffn, w_up, w_down, norm_final):
    assert norm_mix.shape[0] == 2 and w_in_even.shape[0] == 1 and w_in_odd.shape[0] == 1
    d = x_prompt.shape[-1]
    ev_cols = w_in_even.shape[-1]
    ev_pad = POOL_DIM + 3 * FOX_DIM + LANES - ev_cols
    wts = {
        "norm_mix": norm_mix, "norm_ffn": norm_ffn, "norm_final": norm_final,
        "w_in_even": jnp.pad(w_in_even[0], ((0, 0), (0, ev_pad))).astype(BF16),
        "b_forget": jnp.pad(b_forget[0], (0, LANES - FOX_HEADS)).reshape(1, LANES),
        "pool_w": _block_diag(pool_w[0]).astype(BF16),
        "pool_scale": pool_scale[0].reshape(1, POOL_DIM),
        "w_out_even": w_out_even[0].astype(BF16),
        "w_in_odd": w_in_odd[0].astype(BF16),
        "gmlp_ln_g": gmlp_ln_g[0], "gmlp_ln_b": gmlp_ln_b[0], "gmlp_ws": gmlp_ws[0], "gmlp_bs": gmlp_bs[0],
        "ret_gn_g": ret_gn_g[0],
        "w_out_odd": w_out_odd[0].astype(BF16),
        "w_up": w_up.astype(BF16), "w_down": w_down.astype(BF16),
    }
    b, s, _ = x_prompt.shape
    db, t, _ = x_sample.shape
    n_pages = page_table.shape[1]
    page = cache_k.shape[2]
    past = n_pages * page

    def attend_prompt(q, kb, vb, cum, aq, ak):
        return _fox_prompt(q, kb, vb, aq, ak)

    cache_kt = jnp.transpose(cache_k[0], (0, 2, 3, 1))
    cache_vt = jnp.transpose(cache_v[0], (0, 2, 3, 1))
    n_phys = cache_kt.shape[0]
    suffix = _page_suffix(jnp.transpose(cache_logf[0], (2, 0, 1)))

    def attend_sample(q, kb, vb, cum, aq, ak):
        grow = jnp.transpose(cum.reshape(db, t, LANES), (0, 2, 1))[:, :HEAD_ROWS, :]
        gq = jnp.broadcast_to(grow[:, :FOX_HEADS, :, None], (db, FOX_HEADS, t, LANES)).reshape(db, FOX_HEADS * t, LANES)
        return _fox_sample(page_table, q, kb, vb, gq, grow, cache_kt, cache_vt, suffix)

    y_p, pool_p, k_p, v_p, lf_p, _, ret_p = _run_group(
        x_prompt, 0, jnp.zeros((b, POOL_HIST, POOL_DIM), F32),
        jnp.zeros((b, RET_HEADS, RET_HEAD_DIM, RET_HEAD_DIM), F32), attend_prompt, True, wts)
    y_s, pool_s, k_s, v_s, lf_s, gv_s, ret_s = _run_group(
        x_sample, past, state_pool[0], state_ret[0], attend_sample, False, wts)
    return (y_p, y_s, pool_p, k_p, v_p, lf_p, ret_p, pool_s, k_s, v_s, lf_s, gv_s, ret_s)
```

```python
import functools
import math

import numpy as np
import jax
import jax.numpy as jnp
from jax import lax
from jax.experimental import pallas as pl
from jax.experimental.pallas import tpu as pltpu

F32 = jnp.float32
BF16 = jnp.bfloat16
EPS = 1e-6

POOL_WINDOWS = (2, 4, 8, 16)
POOL_GROUP_DIM = 64
POOL_DIM = 256
POOL_HIST = 15
FOX_HEAD_DIM = 64
FOX_HEADS = 12
FOX_DIM = FOX_HEADS * FOX_HEAD_DIM
GMLP_CHUNK = 128
GMLP_GROUPS = 4
GMLP_DIM = 512
RET_HEADS = 4
RET_DIM = 512
RET_HEAD_DIM = 128
RET_CHUNK = 128
ROPE_BASE = 10000.0

LANES = 128
SUBLANES = 8
HEAD_ROWS = 16
SPLIT_STRIDE = 16
NEG_BIG = -1e30
LOG2E = math.log2(math.e)
VMEM_LIMIT = 56 * 1024 * 1024
ROW_TILE = 512
ATT_TILE = 512
PAGES_PER_STEP = 16
FUSED_ROW_TILE = 256
RET_BATCH_BLOCK = 8
SUFFIX_PAGE_TILE = 256


def _row_tile(rows):
    return ROW_TILE if rows % ROW_TILE == 0 else rows


def _params(*sem):
    return pltpu.CompilerParams(dimension_semantics=sem, vmem_limit_bytes=VMEM_LIMIT)


def _const_spec(shape):
    nd = len(shape)
    return pl.BlockSpec(shape, lambda *_: (0,) * nd, pipeline_mode=pl.Buffered(1))


def _rms(x, g):
    return x * lax.rsqrt(jnp.mean(x * x, axis=-1, keepdims=True) + EPS) * g


def _bdot(a, b):
    return jnp.dot(a, b, preferred_element_type=F32)


def _bdot_nt(a, b):
    return lax.dot_general(a, b, (((1,), (1,)), ((), ())), preferred_element_type=F32)


def _bdot_tn(a, b):
    return lax.dot_general(a, b, (((0,), (0,)), ((), ())), preferred_element_type=F32)


def _split3(x):
    hi = x.astype(BF16)
    r = x - hi.astype(F32)
    mid = r.astype(BF16)
    lo = (r - mid.astype(F32)).astype(BF16)
    return hi, mid, lo


def _log_sigmoid(x):
    return jnp.minimum(x, 0.0) - jnp.log1p(jnp.exp(-jnp.abs(x)))


def _aug_tables():
    perm = np.zeros((LANES, 2 * FOX_DIM), np.float32)
    const = np.zeros((1, 2 * FOX_DIM), np.float32)
    for h in range(FOX_HEADS):
        base = (h // 2) * LANES + (FOX_HEAD_DIM if h % 2 == 0 else 0)
        for j in range(3):
            perm[SPLIT_STRIDE * j + h, base + j] = 1.0
            const[0, FOX_DIM + base + j] = 1.0
            const[0, base + 3 + j] = 1.0
            perm[SPLIT_STRIDE * j + h, FOX_DIM + base + 3 + j] = -1.0
    return jnp.asarray(perm, BF16), jnp.asarray(const)


def _in_even_kernel(x_ref, g_ref, w_ref, bf_ref, tri_ref, perm_ref, const_ref,
                    a_ref, q_ref, k_ref, v_ref, kb_ref, vb_ref, lf_ref, cum_ref, cumt_ref, aq_ref, ak_ref,
                    carry_ref, *, tiles_per_seq, transposed_kv):
    i = pl.program_id(0)
    xn = _rms(x_ref[...], g_ref[...]).astype(BF16)
    proj = _bdot(xn, w_ref[...])
    o_q, o_k, o_v, o_f = POOL_DIM, POOL_DIM + FOX_DIM, POOL_DIM + 2 * FOX_DIM, POOL_DIM + 3 * FOX_DIM
    a_ref[...] = proj[:, :o_q]
    q_ref[...] = (proj[:, o_q:o_k] * (FOX_HEAD_DIM ** -0.5 * LOG2E)).astype(BF16)
    k = proj[:, o_k:o_v]
    v = proj[:, o_v:o_f]
    if transposed_kv:
        k_ref[0] = k.T
        v_ref[0] = v.T
    else:
        k_ref[...] = k
        v_ref[...] = v
    kb_ref[...] = k.astype(BF16)
    vb_ref[...] = v.astype(BF16)
    lf = _log_sigmoid(proj[:, o_f:] + bf_ref[...])
    lf_ref[...] = lf
    tri3 = _bdot(tri_ref[...], jnp.concatenate(_split3(lf), axis=1))
    cum = tri3[:, :LANES] + tri3[:, LANES:2 * LANES] + tri3[:, 2 * LANES:]
    if tiles_per_seq > 1:
        @pl.when(i % tiles_per_seq == 0)
        def _():
            carry_ref[...] = jnp.zeros_like(carry_ref)
        cum = cum + carry_ref[...]
        carry_ref[...] = cum[-1:, :]
    cum = cum * LOG2E
    cum_ref[...] = cum
    cumt_ref[0] = cum.T[:HEAD_ROWS, :]
    is_head = lax.broadcasted_iota(jnp.int32, (1, LANES), 1) < FOX_HEADS
    packed = sum(pltpu.roll(jnp.where(is_head, p.astype(F32), 0.0), SPLIT_STRIDE * j, 1)
                 for j, p in enumerate(_split3(cum)))
    aug = (_bdot(packed.astype(BF16), perm_ref[...]) + const_ref[...]).astype(BF16)
    aq_ref[...] = aug[:, :FOX_DIM]
    ak_ref[...] = aug[:, FOX_DIM:]


def _in_even(x, g, w, bf, seq_len, transposed_kv):
    rows, d = x.shape
    tm = _row_tile(rows)
    n_tiles = rows // tm
    r = np.arange(tm)
    if seq_len >= tm:
        assert seq_len % tm == 0
        tiles_per_seq = seq_len // tm
        tri = r[None, :] <= r[:, None]
        cumt_shape = (rows // seq_len, HEAD_ROWS, seq_len)
        cumt_map = lambda i: (i // tiles_per_seq, 0, i % tiles_per_seq)
    else:
        assert tm % seq_len == 0 and not transposed_kv
        tiles_per_seq = 1
        tri = (r[None, :] <= r[:, None]) & ((r[None, :] // seq_len) == (r[:, None] // seq_len))
        cumt_shape = (n_tiles, HEAD_ROWS, tm)
        cumt_map = lambda i: (i, 0, 0)
    tri = jnp.asarray(tri, BF16)
    perm, const = _aug_tables()
    n = w.shape[1]
    row = lambda width: pl.BlockSpec((tm, width), lambda i: (i, 0))
    if transposed_kv:
        kv_shape = jax.ShapeDtypeStruct((rows // seq_len, FOX_DIM, seq_len), F32)
        kv_spec = lambda: pl.BlockSpec((1, FOX_DIM, tm), cumt_map)
    else:
        kv_shape = jax.ShapeDtypeStruct((rows, FOX_DIM), F32)
        kv_spec = lambda: row(FOX_DIM)
    out_shape = (
        jax.ShapeDtypeStruct((rows, POOL_DIM), F32),
        jax.ShapeDtypeStruct((rows, FOX_DIM), BF16),
        kv_shape,
        kv_shape,
        jax.ShapeDtypeStruct((rows, FOX_DIM), BF16),
        jax.ShapeDtypeStruct((rows, FOX_DIM), BF16),
        jax.ShapeDtypeStruct((rows, LANES), F32),
        jax.ShapeDtypeStruct((rows, LANES), F32),
        jax.ShapeDtypeStruct(cumt_shape, F32),
        jax.ShapeDtypeStruct((rows, FOX_DIM), BF16),
        jax.ShapeDtypeStruct((rows, FOX_DIM), BF16),
    )
    out_specs = (row(POOL_DIM), row(FOX_DIM), kv_spec(), kv_spec(), row(FOX_DIM), row(FOX_DIM),
                 row(LANES), row(LANES), pl.BlockSpec((1, HEAD_ROWS, tm), cumt_map), row(FOX_DIM), row(FOX_DIM))
    return pl.pallas_call(
        functools.partial(_in_even_kernel, tiles_per_seq=tiles_per_seq, transposed_kv=transposed_kv),
        grid=(n_tiles,),
        in_specs=[row(d), _const_spec((1, d)), _const_spec((d, n)), _const_spec((1, LANES)),
                  _const_spec((tm, tm)), _const_spec(perm.shape), _const_spec(const.shape)],
        out_specs=out_specs,
        out_shape=out_shape,
        scratch_shapes=[pltpu.VMEM((1, LANES), F32)],
        compiler_params=_params("arbitrary"),
        name="in_even",
    )(x, g, w, bf, tri, perm, const)


def _pool_kernel(a_ref, hist_ref, w_ref, scale_ref, out_ref, newhist_ref, ext_ref, *, seq_len, start_pos):
    t = seq_len
    base = POOL_HIST + 1
    x0 = a_ref[0]
    ext_ref[0:1, :] = jnp.zeros((1, POOL_DIM), F32)
    ext_ref[1:base, :] = hist_ref[0]
    ext_ref[base:base + t, :] = x0

    def back(kk):
        return ext_ref[base - kk:base - kk + t, :]

    sums = []
    run = x0
    nxt = 1
    for w in POOL_WINDOWS:
        while nxt < w:
            run = run + back(nxt)
            nxt += 1
        sums.append(run)
    grp = lax.broadcasted_iota(jnp.int32, (1, POOL_DIM), 1) // POOL_GROUP_DIM
    sel = sums[-1]
    win = jnp.full((1, POOL_DIM), POOL_WINDOWS[-1], jnp.int32)
    for gi in range(len(POOL_WINDOWS) - 2, -1, -1):
        sel = jnp.where(grp == gi, sums[gi], sel)
        win = jnp.where(grp == gi, POOL_WINDOWS[gi], win)
    pos = start_pos + lax.broadcasted_iota(jnp.int32, (t, 1), 0)
    cnt = jnp.minimum(pos + 1, win).astype(F32)
    dd = sel / cnt - x0
    y = _bdot(dd.astype(BF16), w_ref[...]) * scale_ref[...]
    out_ref[0] = y.astype(out_ref.dtype)
    newhist_ref[0] = ext_ref[t + 1:t + base, :]


def _pool_mix(a, hist, w_bd, scale, start_pos):
    b, t, _ = a.shape
    return pl.pallas_call(
        functools.partial(_pool_kernel, seq_len=t, start_pos=start_pos),
        grid=(b,),
        in_specs=[pl.BlockSpec((1, t, POOL_DIM), lambda i: (i, 0, 0)),
                  pl.BlockSpec((1, POOL_HIST, POOL_DIM), lambda i: (i, 0, 0)),
                  _const_spec((POOL_DIM, POOL_DIM)), _const_spec((1, POOL_DIM))],
        out_specs=(pl.BlockSpec((1, t, POOL_DIM), lambda i: (i, 0, 0)),
                   pl.BlockSpec((1, POOL_HIST, POOL_DIM), lambda i: (i, 0, 0))),
        out_shape=(jax.ShapeDtypeStruct((b, t, POOL_DIM), BF16),
                   jax.ShapeDtypeStruct((b, POOL_HIST, POOL_DIM), F32)),
        scratch_shapes=[pltpu.VMEM((POOL_HIST + 1 + t, POOL_DIM), F32)],
        compiler_params=_params("arbitrary"),
        name="pool_mix",
    )(a, hist, w_bd, scale)


def _fox_prompt_kernel(q_ref, k_ref, v_ref, aq_ref, ak_ref, o_ref, *, tile):
    s_len = q_ref.shape[1]
    lane = lax.broadcasted_iota(jnp.int32, (1, LANES), 1)
    in_head = (lane < FOX_HEAD_DIM, lane >= FOX_HEAD_DIM)
    visible = (lax.broadcasted_iota(jnp.int32, (tile, tile), 1) <= lax.broadcasted_iota(jnp.int32, (tile, tile), 0))
    km = []
    for ki in range(s_len // tile):
        rows = slice(ki * tile, (ki + 1) * tile)
        km.append([jnp.where(in_head[e], k_ref[0, rows, :], ak_ref[0, rows, :]) for e in range(2)])
    for qi in range(s_len // tile):
        qrows = slice(qi * tile, (qi + 1) * tile)
        outs = []
        for e in range(2):
            qm = jnp.where(in_head[e], q_ref[0, qrows, :], aq_ref[0, qrows, :])
            m = jnp.full((tile, 1), NEG_BIG, F32)
            l = jnp.zeros((tile, 1), F32)
            acc = jnp.zeros((tile, LANES), F32)
            for ki in range(qi + 1):
                s = _bdot_nt(qm, km[ki][e])
                if ki == qi:
                    s = jnp.where(visible, s, NEG_BIG)
                m_new = jnp.maximum(m, jnp.max(s, axis=-1, keepdims=True))
                alpha = jnp.exp2(m - m_new)
                p = jnp.exp2(s - m_new)
                l = alpha * l + jnp.sum(p, axis=-1, keepdims=True)
                acc = alpha * acc + _bdot(p.astype(BF16), v_ref[0, ki * tile:(ki + 1) * tile, :])
                m = m_new
            outs.append(acc / l)
        o_ref[0, qrows, :] = jnp.where(in_head[0], outs[0], outs[1]).astype(o_ref.dtype)


def _fox_prompt(q, kb, vb, aq, ak):
    b, s, _ = q.shape
    tile = ATT_TILE if s % ATT_TILE == 0 else s
    assert s // tile <= 4, "the causal tile structure is unrolled in the kernel"
    full = lambda: pl.BlockSpec((1, s, LANES), lambda bi, hp: (bi, 0, hp))
    return pl.pallas_call(
        functools.partial(_fox_prompt_kernel, tile=tile),
        grid=(b, FOX_HEADS // 2),
        in_specs=[full(), full(), full(), full(), full()],
        out_specs=full(),
        out_shape=jax.ShapeDtypeStruct((b, s, FOX_DIM), BF16),
        compiler_params=_params("arbitrary", "arbitrary"),
        name="fox_prompt",
    )(q, kb, vb, aq, ak)


def _suffix_kernel(lf_ref, mat_ref, out_ref):
    for h in range(FOX_HEADS):
        out_ref[:, h, :] = sum(_bdot(p, mat_ref[...]) for p in _split3(lf_ref[h])) * LOG2E


def _page_suffix(lf_hpp):
    heads, n_phys, page = lf_hpp.shape
    tp = SUFFIX_PAGE_TILE if n_phys % SUFFIX_PAGE_TILE == 0 else n_phys
    r = np.arange(page)
    later = r[:, None] > r[None, :]
    mat = jnp.asarray(np.concatenate([later, np.ones((page, page), bool)], axis=1), BF16)
    return pl.pallas_call(
        _suffix_kernel,
        grid=(n_phys // tp,),
        in_specs=[pl.BlockSpec((heads, tp, page), lambda i: (0, i, 0)), _const_spec((page, 2 * page))],
        out_specs=pl.BlockSpec((tp, heads, 2 * page), lambda i: (i, 0, 0)),
        out_shape=jax.ShapeDtypeStruct((n_phys, heads, 2 * page), F32),
        compiler_params=_params("arbitrary"),
        name="page_suffix",
    )(lf_hpp, mat)


def _fox_sample_mlp_kernel(pt_ref, q_ref, kn_ref, vn_ref, gq_ref, grow_ref, *rest, n_pg, t_new, steps_per_tile,
                           final_norm):
    k_refs = rest[:n_pg]
    v_refs = rest[n_pg:2 * n_pg]
    s_refs = rest[2 * n_pg:3 * n_pg]
    (h_ref, x1_ref, x2_ref, wo1_ref, wo2_ref, gf_ref, wu_ref, wd_ref, gl_ref,
     o_ref, y_ref, m_sc, l_sc, acc_sc, run_sc, xn_sc) = rest[3 * n_pg:]
    i = pl.program_id(1)
    last = pl.num_programs(1) - 1
    page = LANES
    chunk = (pl.program_id(0) * pl.num_programs(1) + i) % steps_per_tile

    @pl.when(i == 0)
    def _():
        m_sc[...] = jnp.full_like(m_sc, NEG_BIG)
        l_sc[...] = jnp.zeros_like(l_sc)
        acc_sc[...] = jnp.zeros_like(acc_sc)
        run_sc[...] = jnp.zeros_like(run_sc)

    @pl.when(chunk == 0)
    def _():
        h1 = h_ref[...] + _bdot(x1_ref[...], wo1_ref[...]) + _bdot(x2_ref[...], wo2_ref[...])
        y_ref[...] = h1
        xn_sc[...] = _rms(h1, gf_ref[...]).astype(BF16)

    run = run_sc[...]
    bias = [None] * n_pg
    for j in range(n_pg - 1, -1, -1):
        sf = s_refs[j][0]
        bias[j] = sf[:, :page] + run
        run = run + sf[:, page:]
    run_sc[...] = run

    t = t_new
    q = q_ref[0]
    gq = gq_ref[0]
    s_parts = []
    for h in range(FOX_HEADS):
        hs = slice(h * FOX_HEAD_DIM, (h + 1) * FOX_HEAD_DIM)
        kt = jnp.concatenate([k_refs[j][0, h] for j in range(n_pg)], axis=1).astype(BF16)
        s_parts.append(_bdot(q[:, hs], kt) + jnp.concatenate([bias[j][h:h + 1, :] for j in range(n_pg)], axis=1))
    s = jnp.concatenate(s_parts, axis=0) + jnp.concatenate([gq] * n_pg, axis=1)
    u = jnp.maximum(_bdot(xn_sc[...], wu_ref[chunk]), 0.0)
    u = (u * u).astype(BF16)
    m_prev = m_sc[...]
    m_new = jnp.maximum(m_prev, jnp.max(s, axis=-1, keepdims=True))
    alpha = jnp.exp2(m_prev - m_new)
    p32 = jnp.exp2(s - jnp.concatenate([m_new] * n_pg, axis=1))
    l_sc[...] = alpha * l_sc[...] + jnp.sum(p32, axis=-1, keepdims=True)
    m_sc[...] = m_new
    pv = []
    for h in range(FOX_HEADS):
        vt = jnp.concatenate([v_refs[j][0, h] for j in range(n_pg)], axis=1).astype(BF16)
        pv.append(_bdot_nt(p32[h * t:(h + 1) * t].astype(BF16), vt))
    acc_sc[...] = alpha[:, :FOX_HEAD_DIM] * acc_sc[...] + jnp.concatenate(pv, axis=0)

    y_ref[...] += _bdot(u, wd_ref[chunk])
    if final_norm:
        @pl.when(chunk == steps_per_tile - 1)
        def _():
            y_ref[...] = _rms(y_ref[...], gl_ref[...])

    @pl.when(i == last)
    def _():
        kn = kn_ref[0]
        vn = vn_ref[0]
        grow = grow_ref[0]
        r = lax.broadcasted_iota(jnp.int32, (t, t), 0)
        c = lax.broadcasted_iota(jnp.int32, (t, t), 1)
        for h in range(FOX_HEADS):
            hs = slice(h * FOX_HEAD_DIM, (h + 1) * FOX_HEAD_DIM)
            rows = slice(h * t, (h + 1) * t)
            s = _bdot_nt(q[:, hs], kn[:, hs]) + (gq[rows, :1] - grow[h:h + 1, :])
            s = jnp.where(c <= r, s, NEG_BIG)
            m_prev = m_sc[rows, :1]
            m_new = jnp.maximum(m_prev, jnp.max(s, axis=-1, keepdims=True))
            alpha = jnp.exp2(m_prev - m_new)
            p32 = jnp.exp2(s - m_new)
            l = alpha * l_sc[rows, :1] + jnp.sum(p32, axis=-1, keepdims=True)
            acc = alpha * acc_sc[rows, :] + _bdot(p32.astype(BF16), vn[:, hs])
            o_ref[0, :, hs] = (acc / l).astype(o_ref.dtype)


def _fox_sample_mlp(page_table, q, kn, vn, gq, grow, cache_kt, cache_vt, suffix,
                    h, x1, x2, wo1, wo2, gf, wu, wd, gl, final_norm):
    b, t, _ = q.shape
    n_pages = page_table.shape[1]
    n_pg = PAGES_PER_STEP if n_pages % PAGES_PER_STEP == 0 else 1
    steps = n_pages // n_pg
    page = cache_kt.shape[-1]
    assert page == LANES
    rows, d = h.shape
    d_ff = wu.shape[1]
    tm = FUSED_ROW_TILE if rows % FUSED_ROW_TILE == 0 else rows
    tiles = rows // tm
    assert (b * steps) % tiles == 0, "grid steps must split evenly over the MLP row tiles"
    spt = (b * steps) // tiles
    assert d_ff % spt == 0 and (d_ff // spt) % LANES == 0
    ck = d_ff // spt
    wu_c = jnp.transpose(wu.reshape(d, spt, ck), (1, 0, 2))
    wd_c = wd.reshape(spt, ck, d)
    row = lambda width: pl.BlockSpec((tm, width), lambda bi, i, pt: ((bi * steps + i) // spt, 0))

    def page_idx(bi, i, pt, j):
        return pt[bi, n_pages - n_pg * (i + 1) + j]

    tok = lambda width: pl.BlockSpec((1, t, width), lambda bi, i, pt: (bi, 0, 0))
    kv_specs = [pl.BlockSpec((1, FOX_HEADS, FOX_HEAD_DIM, page),
                             functools.partial(lambda bi, i, pt, j: (page_idx(bi, i, pt, j), 0, 0, 0), j=j))
                for j in range(n_pg)]
    sfx_specs = [pl.BlockSpec((1, FOX_HEADS, 2 * page),
                              functools.partial(lambda bi, i, pt, j: (page_idx(bi, i, pt, j), 0, 0), j=j))
                 for j in range(n_pg)]
    grid_spec = pltpu.PrefetchScalarGridSpec(
        num_scalar_prefetch=1,
        grid=(b, steps),
        in_specs=[tok(FOX_DIM), tok(FOX_DIM), tok(FOX_DIM),
                  pl.BlockSpec((1, FOX_HEADS * t, LANES), lambda bi, i, pt: (bi, 0, 0)),
                  pl.BlockSpec((1, HEAD_ROWS, t), lambda bi, i, pt: (bi, 0, 0))]
                 + kv_specs + kv_specs + sfx_specs
                 + [row(d), row(x1.shape[1]), row(x2.shape[1]), _const_spec(wo1.shape), _const_spec(wo2.shape),
                    _const_spec((1, d)), _const_spec(wu_c.shape), _const_spec(wd_c.shape), _const_spec((1, d))],
        out_specs=(tok(FOX_DIM), row(d)),
        scratch_shapes=[pltpu.VMEM((FOX_HEADS * t, LANES), F32), pltpu.VMEM((FOX_HEADS * t, LANES), F32),
                        pltpu.VMEM((FOX_HEADS * t, FOX_HEAD_DIM), F32), pltpu.VMEM((FOX_HEADS, LANES), F32),
                        pltpu.VMEM((tm, d), BF16)],
    )
    return pl.pallas_call(
        functools.partial(_fox_sample_mlp_kernel, n_pg=n_pg, t_new=t, steps_per_tile=spt, final_norm=final_norm),
        grid_spec=grid_spec,
        out_shape=(jax.ShapeDtypeStruct((b, t, FOX_DIM), BF16), jax.ShapeDtypeStruct((rows, d), F32)),
        compiler_params=_params("arbitrary", "arbitrary"),
        name="fox_sample_mlp",
    )(page_table, q, kn, vn, gq, grow, *([cache_kt] * n_pg), *([cache_vt] * n_pg), *([suffix] * n_pg),
      h, x1, x2, wo1, wo2, gf, wu_c, wd_c, gl)


def _mix_mlp_kernel(h_ref, x1_ref, x2_ref, wo1_ref, wo2_ref, gf_ref, wu_ref, wd_ref, gl_ref, o_ref, *,
                    ff_chunk, final_norm):
    h1 = h_ref[...] + _bdot(x1_ref[...], wo1_ref[...]) + _bdot(x2_ref[...], wo2_ref[...])
    xn = _rms(h1, gf_ref[...]).astype(BF16)
    o_ref[...] = h1
    d_ff = wu_ref.shape[1]
    for c in range(d_ff // ff_chunk):
        u = jnp.maximum(_bdot(xn, wu_ref[:, c * ff_chunk:(c + 1) * ff_chunk]), 0.0)
        o_ref[...] += _bdot((u * u).astype(BF16), wd_ref[c * ff_chunk:(c + 1) * ff_chunk, :])
    if final_norm:
        o_ref[...] = _rms(o_ref[...], gl_ref[...])


def _mix_mlp(h, x1, x2, wo1, wo2, gf, wu, wd, gl, final_norm):
    rows, d = h.shape
    tm = _row_tile(rows)
    d_ff = wu.shape[1]
    row = lambda width: pl.BlockSpec((tm, width), lambda i: (i, 0))
    return pl.pallas_call(
        functools.partial(_mix_mlp_kernel, ff_chunk=min(d_ff, 1024), final_norm=final_norm),
        grid=(rows // tm,),
        in_specs=[row(d), row(x1.shape[1]), row(x2.shape[1]), _const_spec(wo1.shape), _const_spec(wo2.shape),
                  _const_spec((1, d)), _const_spec(wu.shape), _const_spec(wd.shape), _const_spec((1, d))],
        out_specs=row(d),
        out_shape=jax.ShapeDtypeStruct((rows, d), F32),
        compiler_params=_params("arbitrary"),
        name="mix_mlp",
    )(h, x1, x2, wo1, wo2, gf, wu, wd, gl)


def _in_odd_kernel(x_ref, g_ref, w_ref, lng_ref, lnb_ref, cos_ref, sin_ref, ws_ref, bs_ref,
                   c_ref, vn_ref, q_ref, k_ref, rv_ref, rg_ref, *, chunk):
    xn = _rms(x_ref[...], g_ref[...]).astype(BF16)
    proj = _bdot(xn, w_ref[...])
    tm = proj.shape[0]
    gd, rd = GMLP_DIM, RET_DIM
    gv = proj[:, gd:2 * gd]
    mu = jnp.mean(gv, axis=-1, keepdims=True)
    var = jnp.mean((gv - mu) ** 2, axis=-1, keepdims=True)
    vn = (gv - mu) * lax.rsqrt(var + EPS) * lng_ref[...] + lnb_ref[...]
    vn_ref[...] = vn
    vnb = vn.astype(BF16)
    wrows = ws_ref.shape[1]
    r = lax.broadcasted_iota(jnp.int32, (wrows, wrows), 0)
    c = lax.broadcasted_iota(jnp.int32, (wrows, wrows), 1)
    causal_in_chunk = (c <= r) & (c >= (r // chunk) * chunk)
    for gi in range(GMLP_GROUPS):
        sl = slice(gi * LANES, (gi + 1) * LANES)
        ws = jnp.where(causal_in_chunk, ws_ref[gi], 0.0).astype(BF16)
        for ti in range(tm // wrows):
            rows = slice(ti * wrows, (ti + 1) * wrows)
            z = _bdot(ws, vnb[rows, sl]) + bs_ref[gi]
            c_ref[rows, sl] = (proj[rows, sl] * z).astype(c_ref.dtype)
    cos = cos_ref[...]
    sin = sin_ref[...]
    o_q, o_k, o_v, o_g = 2 * gd, 2 * gd + rd, 2 * gd + 2 * rd, 2 * gd + 3 * rd
    for hh in range(RET_HEADS):
        sl = slice(hh * RET_HEAD_DIM, (hh + 1) * RET_HEAD_DIM)
        qh = proj[:, o_q + hh * RET_HEAD_DIM:o_q + (hh + 1) * RET_HEAD_DIM]
        kh = proj[:, o_k + hh * RET_HEAD_DIM:o_k + (hh + 1) * RET_HEAD_DIM]
        q_ref[:, sl] = (qh * cos + pltpu.roll(qh, RET_HEAD_DIM // 2, 1) * sin).astype(BF16)
        k_ref[:, sl] = (kh * cos + pltpu.roll(kh, RET_HEAD_DIM // 2, 1) * sin) * (RET_HEAD_DIM ** -0.5)
    rv_ref[...] = proj[:, o_v:o_g].astype(BF16)
    rg_ref[...] = proj[:, o_g:]


def _in_odd(x, g, w, lng, lnb, cos_t, sin_t, gmlp_ws, gmlp_bs, seq_len):
    rows, d = x.shape
    tm = _row_tile(rows)
    chunk = min(seq_len, GMLP_CHUNK)
    assert seq_len % chunk == 0 and tm % chunk == 0
    wrows = chunk if chunk % LANES == 0 else tm
    reps = wrows // chunk
    ws_t = jnp.tile(gmlp_ws[:, :chunk, :chunk], (1, reps, reps))
    bs_t = jnp.broadcast_to(jnp.tile(gmlp_bs[:, :chunk], (1, reps))[:, :, None], (GMLP_GROUPS, wrows, LANES))
    if seq_len >= tm:
        tiles_per_seq = seq_len // tm
        rope_map = lambda i: (i % tiles_per_seq, 0)
    else:
        reps = tm // seq_len
        cos_t = jnp.tile(cos_t, (reps, 1))
        sin_t = jnp.tile(sin_t, (reps, 1))
        rope_map = lambda i: (0, 0)
    row = lambda width: pl.BlockSpec((tm, width), lambda i: (i, 0))
    outs = ((GMLP_DIM, BF16), (GMLP_DIM, F32), (RET_DIM, BF16), (RET_DIM, F32), (RET_DIM, BF16), (RET_DIM, F32))
    return pl.pallas_call(
        functools.partial(_in_odd_kernel, chunk=chunk),
        grid=(rows // tm,),
        in_specs=[row(d), _const_spec((1, d)), _const_spec(w.shape), _const_spec((1, GMLP_DIM)),
                  _const_spec((1, GMLP_DIM)), pl.BlockSpec((tm, RET_HEAD_DIM), rope_map),
                  pl.BlockSpec((tm, RET_HEAD_DIM), rope_map), _const_spec(ws_t.shape), _const_spec(bs_t.shape)],
        out_specs=tuple(row(wd) for wd, _ in outs),
        out_shape=tuple(jax.ShapeDtypeStruct((rows, wd), dt) for wd, dt in outs),
        compiler_params=_params("arbitrary"),
        name="in_odd",
    )(x, g, w, lng, lnb, cos_t, sin_t, ws_t, bs_t)


def _ret_kernel(q_ref, k_ref, v_ref, g_ref, s0_ref, dintra_ref, dq_ref, dk_ref, ds_ref, gn_ref,
                o_ref, sout_ref, st_sc, *, batch_block):
    c = pl.program_id(1)

    @pl.when(c == 0)
    def _():
        st_sc[...] = s0_ref[...]

    for bi in range(batch_block):
        for hh in range(RET_HEADS):
            sl = slice(hh * RET_HEAD_DIM, (hh + 1) * RET_HEAD_DIM)
            q = q_ref[bi, :, sl]
            kf = k_ref[bi, :, sl]
            v = v_ref[bi, :, sl]
            st = st_sc[bi, hh]
            sc = _bdot_nt(q, kf.astype(BF16)) * dintra_ref[hh]
            o = _bdot(sc.astype(BF16), v) + _bdot(q, st.astype(BF16)) * dq_ref[hh]
            st_sc[bi, hh] = st * ds_ref[hh] + _bdot_tn((kf * dk_ref[hh]).astype(BF16), v)
            mu = jnp.mean(o, axis=-1, keepdims=True)
            var = jnp.mean((o - mu) ** 2, axis=-1, keepdims=True)
            on = (o - mu) * lax.rsqrt(var + EPS) * gn_ref[:, sl]
            gate = g_ref[bi, :, sl]
            o_ref[bi, :, sl] = (on * (gate * jax.nn.sigmoid(gate))).astype(o_ref.dtype)

    @pl.when(c == pl.num_programs(1) - 1)
    def _():
        sout_ref[...] = st_sc[...]


def _retention(q, k, v, gate, s0, gn):
    b, t, _ = q.shape
    bb = RET_BATCH_BLOCK if b % RET_BATCH_BLOCK == 0 else 1
    lc = math.gcd(t, RET_CHUNK) if t >= RET_CHUNK else t
    log_g = jnp.log1p(-jnp.exp2(-5.0 - jnp.arange(RET_HEADS, dtype=F32)))
    idx = jnp.arange(lc, dtype=F32)
    diff = idx[:, None] - idx[None, :]
    dintra = jnp.where(diff[None] >= 0, jnp.exp(diff[None] * log_g[:, None, None]), 0.0)
    lanes = (RET_HEADS, lc, RET_HEAD_DIM)
    dq = jnp.broadcast_to(jnp.exp((idx[None, :] + 1.0) * log_g[:, None])[:, :, None], lanes)
    dk = jnp.broadcast_to(jnp.exp((lc - 1.0 - idx)[None, :] * log_g[:, None])[:, :, None], lanes)
    ds = jnp.broadcast_to(jnp.exp(lc * log_g)[:, None, None], (RET_HEADS, 1, RET_HEAD_DIM))
    blk = lambda: pl.BlockSpec((bb, lc, RET_DIM), lambda bi, ci: (bi, ci, 0))
    st = lambda: pl.BlockSpec((bb, RET_HEADS, RET_HEAD_DIM, RET_HEAD_DIM), lambda bi, ci: (bi, 0, 0, 0))
    return pl.pallas_call(
        functools.partial(_ret_kernel, batch_block=bb),
        grid=(b // bb, t // lc),
        in_specs=[blk(), blk(), blk(), blk(), st(), _const_spec(dintra.shape), _const_spec(lanes),
                  _const_spec(lanes), _const_spec(ds.shape), _const_spec((1, RET_DIM))],
        out_specs=(blk(), st()),
        out_shape=(jax.ShapeDtypeStruct((b, t, RET_DIM), BF16),
                   jax.ShapeDtypeStruct((b, RET_HEADS, RET_HEAD_DIM, RET_HEAD_DIM), F32)),
        scratch_shapes=[pltpu.VMEM((bb, RET_HEADS, RET_HEAD_DIM, RET_HEAD_DIM), F32)],
        compiler_params=_params("arbitrary", "arbitrary"),
        name="retention",
    )(q, k, v, gate, s0, dintra, dq, dk, ds, gn)


def _rope_tables(start_pos, t):
    half = RET_HEAD_DIM // 2
    pos = (start_pos + jnp.arange(t)).astype(F32)
    inv = ROPE_BASE ** (-jnp.arange(half, dtype=F32) / half)
    ang = pos[:, None] * inv[None, :]
    cos, sin = jnp.cos(ang), jnp.sin(ang)
    return jnp.concatenate([cos, cos], axis=1), jnp.concatenate([-sin, sin], axis=1)


def _block_diag(w):
    g, n, _ = w.shape
    out = jnp.zeros((g * n, g * n), w.dtype)
    for i in range(g):
        out = out.at[i * n:(i + 1) * n, i * n:(i + 1) * n].set(w[i])
    return out


def _even_front(x, start_pos, hist0, transposed_kv, wts):
    b, t, d = x.shape
    e = dict(zip(("a", "q", "k", "v", "kb", "vb", "lf", "cum", "cumt", "aq", "ak"),
                 _in_even(x.reshape(b * t, d), wts["norm_mix"][0].reshape(1, -1), wts["w_in_even"], wts["b_forget"],
                          t, transposed_kv)))
    e["pool_out"], e["new_hist"] = _pool_mix(e["a"].reshape(b, t, POOL_DIM), hist0, wts["pool_w"], wts["pool_scale"],
                                             start_pos)
    return e


def _odd_front(h, b, t, start_pos, state0, wts):
    row1 = lambda a: a.reshape(1, -1)
    r3 = lambda z: z.reshape(b, t, z.shape[-1])
    cos_t, sin_t = _rope_tables(start_pos, t)
    c_out, vn, rq, rk, rv, rg = _in_odd(h, row1(wts["norm_mix"][1]), wts["w_in_odd"], row1(wts["gmlp_ln_g"]),
                                        row1(wts["gmlp_ln_b"]), cos_t, sin_t, wts["gmlp_ws"], wts["gmlp_bs"], t)
    r_out, s_new = _retention(r3(rq), r3(rk), r3(rv), r3(rg), state0, row1(wts["ret_gn_g"]))
    return c_out, r_out.reshape(b * t, RET_DIM), vn, s_new


def _mlp_args(layer, wts):
    w_out, split = (wts["w_out_even"], POOL_DIM) if layer == 0 else (wts["w_out_odd"], GMLP_DIM)
    return (w_out[:split], w_out[split:], wts["norm_ffn"][layer].reshape(1, -1), wts["w_up"][layer],
            wts["w_down"][layer], wts["norm_final"].reshape(1, -1))


def kernel(x_prompt, x_sample, state_pool, cache_k, cache_v, cache_logf, page_table, state_ret, norm_mix, w_in_even, b_forget, pool_w, pool_scale, w_out_even, w_in_odd, gmlp_ln_g, gmlp_ln_b, gmlp_ws, gmlp_bs, ret_gn_g, w_out_odd, norm_ffn, w_up, w_down, norm_final):
    assert norm_mix.shape[0] == 2 and w_in_even.shape[0] == 1 and w_in_odd.shape[0] == 1
    d = x_prompt.shape[-1]
    ev_cols = w_in_even.shape[-1]
    ev_pad = POOL_DIM + 3 * FOX_DIM + LANES - ev_cols
    wts = {
        "norm_mix": norm_mix, "norm_ffn": norm_ffn, "norm_final": norm_final,
        "w_in_even": jnp.pad(w_in_even[0], ((0, 0), (0, ev_pad))).astype(BF16),
        "b_forget": jnp.pad(b_forget[0], (0, LANES - FOX_HEADS)).reshape(1, LANES),
        "pool_w": _block_diag(pool_w[0]).astype(BF16),
        "pool_scale": pool_scale[0].reshape(1, POOL_DIM),
        "w_out_even": w_out_even[0].astype(BF16),
        "w_in_odd": w_in_odd[0].astype(BF16),
        "gmlp_ln_g": gmlp_ln_g[0], "gmlp_ln_b": gmlp_ln_b[0], "gmlp_ws": gmlp_ws[0], "gmlp_bs": gmlp_bs[0],
        "ret_gn_g": ret_gn_g[0],
        "w_out_odd": w_out_odd[0].astype(BF16),
        "w_up": w_up.astype(BF16), "w_down": w_down.astype(BF16),
    }
    b, s, _ = x_prompt.shape
    db, t, _ = x_sample.shape
    n_pages = page_table.shape[1]
    page = cache_k.shape[2]
    past = n_pages * page

    cache_kt = jnp.transpose(cache_k[0], (0, 2, 3, 1))
    cache_vt = jnp.transpose(cache_v[0], (0, 2, 3, 1))
    suffix = _page_suffix(jnp.transpose(cache_logf[0], (2, 0, 1)))

    se = _even_front(x_sample, past, state_pool[0], False, wts)
    pe = _even_front(x_prompt, 0, jnp.zeros((b, POOL_HIST, POOL_DIM), F32), True, wts)
    p3 = lambda z: z.reshape(b, s, z.shape[-1])
    att_p = _fox_prompt(p3(pe["q"]), p3(pe["kb"]), p3(pe["vb"]), p3(pe["aq"]), p3(pe["ak"]))

    assert db % 2 == 0
    half = db // 2
    s3 = lambda z: z.reshape(db, t, z.shape[-1])
    grow = jnp.transpose(s3(se["cum"]), (0, 2, 1))[:, :HEAD_ROWS, :]
    gq = jnp.broadcast_to(grow[:, :FOX_HEADS, :, None], (db, FOX_HEADS, t, LANES)).reshape(db, FOX_HEADS * t, LANES)
    sq, skb, svb = s3(se["q"]), s3(se["kb"]), s3(se["vb"])

    def attend_half(sel, h, x1, x2, layer):
        return _fox_sample_mlp(page_table[sel], sq[sel], skb[sel], svb[sel], gq[sel], grow[sel], cache_kt, cache_vt,
                               suffix, h, x1, x2, *_mlp_args(layer, wts), layer == 1)

    att_s0, h_p = attend_half(slice(0, half), x_prompt.reshape(b * s, d), pe["pool_out"].reshape(b * s, POOL_DIM),
                              att_p.reshape(b * s, FOX_DIM), 0)
    c_p, r_p, _, ret_p = _odd_front(h_p, b, s, 0, jnp.zeros((b, RET_HEADS, RET_HEAD_DIM, RET_HEAD_DIM), F32), wts)
    att_s1, y_p = attend_half(slice(half, db), h_p, c_p, r_p, 1)

    att_s = jnp.concatenate([att_s0, att_s1], axis=0)
    h_s = _mix_mlp(x_sample.reshape(db * t, d), se["pool_out"].reshape(db * t, POOL_DIM),
                   att_s.reshape(db * t, FOX_DIM), *_mlp_args(0, wts), False)
    c_s, r_s, gv_s, ret_s = _odd_front(h_s, db, t, past, state_ret[0], wts)
    y_s = _mix_mlp(h_s, c_s, r_s, *_mlp_args(1, wts), True)

    kv_p = lambda z: jnp.transpose(z.reshape(b, FOX_HEADS, FOX_HEAD_DIM, s), (0, 3, 1, 2))[None]
    kv_s = lambda z: z.reshape(1, db, t, FOX_HEADS, FOX_HEAD_DIM)
    lf_out = lambda e, n, m: e["lf"][:, :FOX_HEADS].reshape(1, n, m, FOX_HEADS)
    return (y_p.reshape(b, s, d), y_s.reshape(db, t, d), pe["new_hist"][None], kv_p(pe["k"]), kv_p(pe["v"]),
            lf_out(pe, b, s), ret_p[None], se["new_hist"][None], kv_s(se["k"]), kv_s(se["v"]), lf_out(se, db, t),
            gv_s.reshape(1, db, t, GMLP_DIM), ret_s[None])
```

```python
import functools
import math

import numpy as np
import jax
import jax.numpy as jnp
from jax import lax
from jax.experimental import pallas as pl
from jax.experimental.pallas import tpu as pltpu

F32 = jnp.float32
BF16 = jnp.bfloat16
EPS = 1e-6

POOL_WINDOWS = (2, 4, 8, 16)
POOL_GROUP_DIM = 64
POOL_DIM = 256
POOL_HIST = 15
FOX_HEAD_DIM = 64
FOX_HEADS = 12
FOX_DIM = FOX_HEADS * FOX_HEAD_DIM
GMLP_CHUNK = 128
GMLP_GROUPS = 4
GMLP_DIM = 512
RET_HEADS = 4
RET_DIM = 512
RET_HEAD_DIM = 128
RET_CHUNK = 128
ROPE_BASE = 10000.0

LANES = 128
SUBLANES = 8
HEAD_ROWS = 16
SPLIT_STRIDE = 16
NEG_BIG = -1e30
LOG2E = math.log2(math.e)
VMEM_LIMIT = 56 * 1024 * 1024
ROW_TILE = 512
ATT_TILE = 512
PAGES_PER_STEP = 8
FUSED_ROW_TILE = 512
RET_BATCH_BLOCK = 8
SUFFIX_PAGE_TILE = 256


def _row_tile(rows):
    return ROW_TILE if rows % ROW_TILE == 0 else rows


def _params(*sem):
    return pltpu.CompilerParams(dimension_semantics=sem, vmem_limit_bytes=VMEM_LIMIT)


def _const_spec(shape):
    nd = len(shape)
    return pl.BlockSpec(shape, lambda *_: (0,) * nd, pipeline_mode=pl.Buffered(1))


def _rms(x, g):
    return x * lax.rsqrt(jnp.mean(x * x, axis=-1, keepdims=True) + EPS) * g


def _bdot(a, b):
    return jnp.dot(a, b, preferred_element_type=F32)


def _bdot_nt(a, b):
    return lax.dot_general(a, b, (((1,), (1,)), ((), ())), preferred_element_type=F32)


def _bdot_tn(a, b):
    return lax.dot_general(a, b, (((0,), (0,)), ((), ())), preferred_element_type=F32)


def _split3(x):
    hi = x.astype(BF16)
    r = x - hi.astype(F32)
    mid = r.astype(BF16)
    lo = (r - mid.astype(F32)).astype(BF16)
    return hi, mid, lo


def _log_sigmoid(x):
    return jnp.minimum(x, 0.0) - jnp.log1p(jnp.exp(-jnp.abs(x)))


def _aug_tables():
    perm = np.zeros((LANES, 2 * FOX_DIM), np.float32)
    const = np.zeros((1, 2 * FOX_DIM), np.float32)
    for h in range(FOX_HEADS):
        base = (h // 2) * LANES + (FOX_HEAD_DIM if h % 2 == 0 else 0)
        for j in range(3):
            perm[SPLIT_STRIDE * j + h, base + j] = 1.0
            const[0, FOX_DIM + base + j] = 1.0
            const[0, base + 3 + j] = 1.0
            perm[SPLIT_STRIDE * j + h, FOX_DIM + base + 3 + j] = -1.0
    return jnp.asarray(perm, BF16), jnp.asarray(const)


def _in_even_kernel(x_ref, g_ref, w_ref, bf_ref, tri_ref, perm_ref, const_ref,
                    a_ref, q_ref, k_ref, v_ref, kb_ref, vb_ref, lf_ref, cum_ref, cumt_ref, aq_ref, ak_ref,
                    carry_ref, *, tiles_per_seq, transposed_kv):
    i = pl.program_id(0)
    xn = _rms(x_ref[...], g_ref[...]).astype(BF16)
    proj = _bdot(xn, w_ref[...])
    o_q, o_k, o_v, o_f = POOL_DIM, POOL_DIM + FOX_DIM, POOL_DIM + 2 * FOX_DIM, POOL_DIM + 3 * FOX_DIM
    a_ref[...] = proj[:, :o_q]
    q_ref[...] = (proj[:, o_q:o_k] * (FOX_HEAD_DIM ** -0.5 * LOG2E)).astype(BF16)
    k = proj[:, o_k:o_v]
    v = proj[:, o_v:o_f]
    if transposed_kv:
        k_ref[0] = k.T
        v_ref[0] = v.T
    else:
        k_ref[...] = k
        v_ref[...] = v
    kb_ref[...] = k.astype(BF16)
    vb_ref[...] = v.astype(BF16)
    lf = _log_sigmoid(proj[:, o_f:] + bf_ref[...])
    lf_ref[...] = lf
    tri3 = _bdot(tri_ref[...], jnp.concatenate(_split3(lf), axis=1))
    cum = tri3[:, :LANES] + tri3[:, LANES:2 * LANES] + tri3[:, 2 * LANES:]
    if tiles_per_seq > 1:
        @pl.when(i % tiles_per_seq == 0)
        def _():
            carry_ref[...] = jnp.zeros_like(carry_ref)
        cum = cum + carry_ref[...]
        carry_ref[...] = cum[-1:, :]
    cum = cum * LOG2E
    cum_ref[...] = cum
    cumt_ref[0] = cum.T[:HEAD_ROWS, :]
    is_head = lax.broadcasted_iota(jnp.int32, (1, LANES), 1) < FOX_HEADS
    packed = sum(pltpu.roll(jnp.where(is_head, p.astype(F32), 0.0), SPLIT_STRIDE * j, 1)
                 for j, p in enumerate(_split3(cum)))
    aug = (_bdot(packed.astype(BF16), perm_ref[...]) + const_ref[...]).astype(BF16)
    aq_ref[...] = aug[:, :FOX_DIM]
    ak_ref[...] = aug[:, FOX_DIM:]


def _in_even(x, g, w, bf, seq_len, transposed_kv):
    rows, d = x.shape
    tm = _row_tile(rows)
    n_tiles = rows // tm
    r = np.arange(tm)
    if seq_len >= tm:
        assert seq_len % tm == 0
        tiles_per_seq = seq_len // tm
        tri = r[None, :] <= r[:, None]
        cumt_shape = (rows // seq_len, HEAD_ROWS, seq_len)
        cumt_map = lambda i: (i // tiles_per_seq, 0, i % tiles_per_seq)
    else:
        assert tm % seq_len == 0 and not transposed_kv
        tiles_per_seq = 1
        tri = (r[None, :] <= r[:, None]) & ((r[None, :] // seq_len) == (r[:, None] // seq_len))
        cumt_shape = (n_tiles, HEAD_ROWS, tm)
        cumt_map = lambda i: (i, 0, 0)
    tri = jnp.asarray(tri, BF16)
    perm, const = _aug_tables()
    n = w.shape[1]
    row = lambda width: pl.BlockSpec((tm, width), lambda i: (i, 0))
    if transposed_kv:
        kv_shape = jax.ShapeDtypeStruct((rows // seq_len, FOX_DIM, seq_len), F32)
        kv_spec = lambda: pl.BlockSpec((1, FOX_DIM, tm), cumt_map)
    else:
        kv_shape = jax.ShapeDtypeStruct((rows, FOX_DIM), F32)
        kv_spec = lambda: row(FOX_DIM)
    out_shape = (
        jax.ShapeDtypeStruct((rows, POOL_DIM), F32),
        jax.ShapeDtypeStruct((rows, FOX_DIM), BF16),
        kv_shape,
        kv_shape,
        jax.ShapeDtypeStruct((rows, FOX_DIM), BF16),
        jax.ShapeDtypeStruct((rows, FOX_DIM), BF16),
        jax.ShapeDtypeStruct((rows, LANES), F32),
        jax.ShapeDtypeStruct((rows, LANES), F32),
        jax.ShapeDtypeStruct(cumt_shape, F32),
        jax.ShapeDtypeStruct((rows, FOX_DIM), BF16),
        jax.ShapeDtypeStruct((rows, FOX_DIM), BF16),
    )
    out_specs = (row(POOL_DIM), row(FOX_DIM), kv_spec(), kv_spec(), row(FOX_DIM), row(FOX_DIM),
                 row(LANES), row(LANES), pl.BlockSpec((1, HEAD_ROWS, tm), cumt_map), row(FOX_DIM), row(FOX_DIM))
    return pl.pallas_call(
        functools.partial(_in_even_kernel, tiles_per_seq=tiles_per_seq, transposed_kv=transposed_kv),
        grid=(n_tiles,),
        in_specs=[row(d), _const_spec((1, d)), _const_spec((d, n)), _const_spec((1, LANES)),
                  _const_spec((tm, tm)), _const_spec(perm.shape), _const_spec(const.shape)],
        out_specs=out_specs,
        out_shape=out_shape,
        scratch_shapes=[pltpu.VMEM((1, LANES), F32)],
        compiler_params=_params("arbitrary"),
        name="in_even",
    )(x, g, w, bf, tri, perm, const)


def _pool_kernel(a_ref, hist_ref, w_ref, scale_ref, out_ref, newhist_ref, ext_ref, *, seq_len, start_pos):
    t = seq_len
    base = POOL_HIST + 1
    x0 = a_ref[0]
    ext_ref[0:1, :] = jnp.zeros((1, POOL_DIM), F32)
    ext_ref[1:base, :] = hist_ref[0]
    ext_ref[base:base + t, :] = x0

    def back(kk):
        return ext_ref[base - kk:base - kk + t, :]

    sums = []
    run = x0
    nxt = 1
    for w in POOL_WINDOWS:
        while nxt < w:
            run = run + back(nxt)
            nxt += 1
        sums.append(run)
    grp = lax.broadcasted_iota(jnp.int32, (1, POOL_DIM), 1) // POOL_GROUP_DIM
    sel = sums[-1]
    win = jnp.full((1, POOL_DIM), POOL_WINDOWS[-1], jnp.int32)
    for gi in range(len(POOL_WINDOWS) - 2, -1, -1):
        sel = jnp.where(grp == gi, sums[gi], sel)
        win = jnp.where(grp == gi, POOL_WINDOWS[gi], win)
    pos = start_pos + lax.broadcasted_iota(jnp.int32, (t, 1), 0)
    cnt = jnp.minimum(pos + 1, win).astype(F32)
    dd = sel / cnt - x0
    y = _bdot(dd.astype(BF16), w_ref[...]) * scale_ref[...]
    out_ref[0] = y.astype(out_ref.dtype)
    newhist_ref[0] = ext_ref[t + 1:t + base, :]


def _pool_mix(a, hist, w_bd, scale, start_pos):
    b, t, _ = a.shape
    return pl.pallas_call(
        functools.partial(_pool_kernel, seq_len=t, start_pos=start_pos),
        grid=(b,),
        in_specs=[pl.BlockSpec((1, t, POOL_DIM), lambda i: (i, 0, 0)),
                  pl.BlockSpec((1, POOL_HIST, POOL_DIM), lambda i: (i, 0, 0)),
                  _const_spec((POOL_DIM, POOL_DIM)), _const_spec((1, POOL_DIM))],
        out_specs=(pl.BlockSpec((1, t, POOL_DIM), lambda i: (i, 0, 0)),
                   pl.BlockSpec((1, POOL_HIST, POOL_DIM), lambda i: (i, 0, 0))),
        out_shape=(jax.ShapeDtypeStruct((b, t, POOL_DIM), BF16),
                   jax.ShapeDtypeStruct((b, POOL_HIST, POOL_DIM), F32)),
        scratch_shapes=[pltpu.VMEM((POOL_HIST + 1 + t, POOL_DIM), F32)],
        compiler_params=_params("arbitrary"),
        name="pool_mix",
    )(a, hist, w_bd, scale)


def _fox_prompt_kernel(q_ref, k_ref, v_ref, aq_ref, ak_ref, o_ref, *, tile):
    s_len = q_ref.shape[1]
    lane = lax.broadcasted_iota(jnp.int32, (1, LANES), 1)
    in_head = (lane < FOX_HEAD_DIM, lane >= FOX_HEAD_DIM)
    visible = (lax.broadcasted_iota(jnp.int32, (tile, tile), 1) <= lax.broadcasted_iota(jnp.int32, (tile, tile), 0))
    km = []
    for ki in range(s_len // tile):
        rows = slice(ki * tile, (ki + 1) * tile)
        km.append([jnp.where(in_head[e], k_ref[0, rows, :], ak_ref[0, rows, :]) for e in range(2)])
    for qi in range(s_len // tile):
        qrows = slice(qi * tile, (qi + 1) * tile)
        outs = []
        for e in range(2):
            qm = jnp.where(in_head[e], q_ref[0, qrows, :], aq_ref[0, qrows, :])
            m = jnp.full((tile, 1), NEG_BIG, F32)
            l = jnp.zeros((tile, 1), F32)
            acc = jnp.zeros((tile, LANES), F32)
            for ki in range(qi + 1):
                s = _bdot_nt(qm, km[ki][e])
                if ki == qi:
                    s = jnp.where(visible, s, NEG_BIG)
                m_new = jnp.maximum(m, jnp.max(s, axis=-1, keepdims=True))
                alpha = jnp.exp2(m - m_new)
                p = jnp.exp2(s - m_new)
                l = alpha * l + jnp.sum(p, axis=-1, keepdims=True)
                acc = alpha * acc + _bdot(p.astype(BF16), v_ref[0, ki * tile:(ki + 1) * tile, :])
                m = m_new
            outs.append(acc / l)
        o_ref[0, qrows, :] = jnp.where(in_head[0], outs[0], outs[1]).astype(o_ref.dtype)


def _fox_prompt(q, kb, vb, aq, ak):
    b, s, _ = q.shape
    tile = ATT_TILE if s % ATT_TILE == 0 else s
    assert s // tile <= 4, "the causal tile structure is unrolled in the kernel"
    full = lambda: pl.BlockSpec((1, s, LANES), lambda bi, hp: (bi, 0, hp))
    return pl.pallas_call(
        functools.partial(_fox_prompt_kernel, tile=tile),
        grid=(b, FOX_HEADS // 2),
        in_specs=[full(), full(), full(), full(), full()],
        out_specs=full(),
        out_shape=jax.ShapeDtypeStruct((b, s, FOX_DIM), BF16),
        compiler_params=_params("arbitrary", "arbitrary"),
        name="fox_prompt",
    )(q, kb, vb, aq, ak)


def _suffix_kernel(lf_ref, mat_ref, out_ref):
    for h in range(FOX_HEADS):
        out_ref[:, h, :] = sum(_bdot(p, mat_ref[...]) for p in _split3(lf_ref[h])) * LOG2E


def _page_suffix(lf_hpp):
    heads, n_phys, page = lf_hpp.shape
    tp = SUFFIX_PAGE_TILE if n_phys % SUFFIX_PAGE_TILE == 0 else n_phys
    r = np.arange(page)
    later = r[:, None] > r[None, :]
    mat = jnp.asarray(np.concatenate([later, np.ones((page, page), bool)], axis=1), BF16)
    return pl.pallas_call(
        _suffix_kernel,
        grid=(n_phys // tp,),
        in_specs=[pl.BlockSpec((heads, tp, page), lambda i: (0, i, 0)), _const_spec((page, 2 * page))],
        out_specs=pl.BlockSpec((tp, heads, 2 * page), lambda i: (i, 0, 0)),
        out_shape=jax.ShapeDtypeStruct((n_phys, heads, 2 * page), F32),
        compiler_params=_params("arbitrary"),
        name="page_suffix",
    )(lf_hpp, mat)


def _page_copies(pt_ref, hbm, bufs, sems, slot, row, first_page, n_pg):
    copies = []
    for j in range(n_pg):
        pg = 0 if pt_ref is None else pt_ref[row, first_page + j]
        for kind in range(3):
            copies.append(pltpu.make_async_copy(hbm[kind].at[pg], bufs[kind].at[slot, j], sems.at[slot, kind]))
    return copies


def _fox_sample_mlp_kernel(pt_ref, q_ref, kn_ref, vn_ref, gq_ref, grow_ref, kt_hbm, vt_hbm, sfx_hbm,
                           h_ref, x1_ref, x2_ref, wo1_ref, wo2_ref, gf_ref, wu_ref, wd_ref, gl_ref,
                           o_ref, y_ref, m_sc, l_sc, acc_sc, run_sc, xn_sc, kbuf, vbuf, sbuf, sems, *,
                           n_pg, n_pages, t_new, steps_per_tile, final_norm):
    n_seq, steps = pl.num_programs(0), pl.num_programs(1)
    bi, i = pl.program_id(0), pl.program_id(1)
    last = steps - 1
    page = LANES
    g = bi * steps + i
    slot = g % 2
    chunk = g % steps_per_tile
    hbm, bufs = (kt_hbm, vt_hbm, sfx_hbm), (kbuf, vbuf, sbuf)
    first_page = lambda step: n_pages - n_pg * (step + 1)

    @pl.when(g == 0)
    def _():
        for cp in _page_copies(pt_ref, hbm, bufs, sems, 0, 0, first_page(0), n_pg):
            cp.start()

    @pl.when(i == 0)
    def _():
        m_sc[...] = jnp.full_like(m_sc, NEG_BIG)
        l_sc[...] = jnp.zeros_like(l_sc)
        acc_sc[...] = jnp.zeros_like(acc_sc)
        run_sc[...] = jnp.zeros_like(run_sc)

    @pl.when(chunk == 0)
    def _():
        h1 = h_ref[...] + _bdot(x1_ref[...], wo1_ref[...]) + _bdot(x2_ref[...], wo2_ref[...])
        y_ref[...] = h1
        xn_sc[...] = _rms(h1, gf_ref[...]).astype(BF16)

    for cp in _page_copies(None, hbm, bufs, sems, slot, 0, 0, n_pg):
        cp.wait()
    wrap_i = i == last
    next_i = jnp.where(wrap_i, 0, i + 1)
    next_bi = jnp.where(wrap_i, jnp.where(bi == n_seq - 1, 0, bi + 1), bi)
    for cp in _page_copies(pt_ref, hbm, bufs, sems, 1 - slot, next_bi, first_page(next_i), n_pg):
        cp.start()

    run = run_sc[...]
    bias = [None] * n_pg
    for j in range(n_pg - 1, -1, -1):
        sf = sbuf[slot, j]
        bias[j] = sf[:, :page] + run
        run = run + sf[:, page:]
    run_sc[...] = run

    t = t_new
    q = q_ref[0]
    gq = gq_ref[0]
    s_parts = []
    for h in range(FOX_HEADS):
        hs = slice(h * FOX_HEAD_DIM, (h + 1) * FOX_HEAD_DIM)
        kt = jnp.concatenate([kbuf[slot, j, h] for j in range(n_pg)], axis=1).astype(BF16)
        s_parts.append(_bdot(q[:, hs], kt) + jnp.concatenate([bias[j][h:h + 1, :] for j in range(n_pg)], axis=1))
    s = jnp.concatenate(s_parts, axis=0) + jnp.concatenate([gq] * n_pg, axis=1)
    u = jnp.maximum(_bdot(xn_sc[...], wu_ref[chunk]), 0.0)
    u = (u * u).astype(BF16)
    m_prev = m_sc[...]
    m_new = jnp.maximum(m_prev, jnp.max(s, axis=-1, keepdims=True))
    alpha = jnp.exp2(m_prev - m_new)
    p32 = jnp.exp2(s - jnp.concatenate([m_new] * n_pg, axis=1))
    l_sc[...] = alpha * l_sc[...] + jnp.sum(p32, axis=-1, keepdims=True)
    m_sc[...] = m_new
    pv = []
    for h in range(FOX_HEADS):
        vt = jnp.concatenate([vbuf[slot, j, h] for j in range(n_pg)], axis=1).astype(BF16)
        pv.append(_bdot_nt(p32[h * t:(h + 1) * t].astype(BF16), vt))
    acc_sc[...] = alpha[:, :FOX_HEAD_DIM] * acc_sc[...] + jnp.concatenate(pv, axis=0)

    y_ref[...] += _bdot(u, wd_ref[chunk])
    if final_norm:
        @pl.when(chunk == steps_per_tile - 1)
        def _():
            y_ref[...] = _rms(y_ref[...], gl_ref[...])

    @pl.when(i == last)
    def _():
        kn = kn_ref[0]
        vn = vn_ref[0]
        grow = grow_ref[0]
        r = lax.broadcasted_iota(jnp.int32, (t, t), 0)
        c = lax.broadcasted_iota(jnp.int32, (t, t), 1)
        for h in range(FOX_HEADS):
            hs = slice(h * FOX_HEAD_DIM, (h + 1) * FOX_HEAD_DIM)
            rows = slice(h * t, (h + 1) * t)
            s = _bdot_nt(q[:, hs], kn[:, hs]) + (gq[rows, :1] - grow[h:h + 1, :])
            s = jnp.where(c <= r, s, NEG_BIG)
            m_prev = m_sc[rows, :1]
            m_new = jnp.maximum(m_prev, jnp.max(s, axis=-1, keepdims=True))
            alpha = jnp.exp2(m_prev - m_new)
            p32 = jnp.exp2(s - m_new)
            l = alpha * l_sc[rows, :1] + jnp.sum(p32, axis=-1, keepdims=True)
            acc = alpha * acc_sc[rows, :] + _bdot(p32.astype(BF16), vn[:, hs])
            o_ref[0, :, hs] = (acc / l).astype(o_ref.dtype)

    @pl.when(g == n_seq * steps - 1)
    def _():
        for cp in _page_copies(None, hbm, bufs, sems, 1 - slot, 0, 0, n_pg):
            cp.wait()


def _fox_sample_mlp(page_table, q, kn, vn, gq, grow, cache_kt, cache_vt, suffix,
                    h, x1, x2, wo1, wo2, gf, wu, wd, gl, final_norm):
    b, t, _ = q.shape
    n_pages = page_table.shape[1]
    n_pg = PAGES_PER_STEP if n_pages % PAGES_PER_STEP == 0 else 1
    steps = n_pages // n_pg
    page = cache_kt.shape[-1]
    assert page == LANES
    rows, d = h.shape
    d_ff = wu.shape[1]
    tm = FUSED_ROW_TILE if rows % FUSED_ROW_TILE == 0 else rows
    tiles = rows // tm
    assert (b * steps) % tiles == 0, "grid steps must split evenly over the MLP row tiles"
    spt = (b * steps) // tiles
    assert d_ff % spt == 0 and (d_ff // spt) % LANES == 0
    ck = d_ff // spt
    wu_c = jnp.transpose(wu.reshape(d, spt, ck), (1, 0, 2))
    wd_c = wd.reshape(spt, ck, d)
    row = lambda width: pl.BlockSpec((tm, width), lambda bi, i, pt: ((bi * steps + i) // spt, 0))

    tok = lambda width: pl.BlockSpec((1, t, width), lambda bi, i, pt: (bi, 0, 0))
    in_hbm = lambda: pl.BlockSpec(memory_space=pl.ANY)
    grid_spec = pltpu.PrefetchScalarGridSpec(
        num_scalar_prefetch=1,
        grid=(b, steps),
        in_specs=[tok(FOX_DIM), tok(FOX_DIM), tok(FOX_DIM),
                  pl.BlockSpec((1, FOX_HEADS * t, LANES), lambda bi, i, pt: (bi, 0, 0)),
                  pl.BlockSpec((1, HEAD_ROWS, t), lambda bi, i, pt: (bi, 0, 0)),
                  in_hbm(), in_hbm(), in_hbm(),
                  row(d), row(x1.shape[1]), row(x2.shape[1]), _const_spec(wo1.shape), _const_spec(wo2.shape),
                  _const_spec((1, d)), _const_spec(wu_c.shape), _const_spec(wd_c.shape), _const_spec((1, d))],
        out_specs=(tok(FOX_DIM), row(d)),
        scratch_shapes=[pltpu.VMEM((FOX_HEADS * t, LANES), F32), pltpu.VMEM((FOX_HEADS * t, LANES), F32),
                        pltpu.VMEM((FOX_HEADS * t, FOX_HEAD_DIM), F32), pltpu.VMEM((FOX_HEADS, LANES), F32),
                        pltpu.VMEM((tm, d), BF16),
                        pltpu.VMEM((2, n_pg, FOX_HEADS, FOX_HEAD_DIM, page), F32),
                        pltpu.VMEM((2, n_pg, FOX_HEADS, FOX_HEAD_DIM, page), F32),
                        pltpu.VMEM((2, n_pg, FOX_HEADS, 2 * page), F32),
                        pltpu.SemaphoreType.DMA((2, 3))],
    )
    return pl.pallas_call(
        functools.partial(_fox_sample_mlp_kernel, n_pg=n_pg, n_pages=n_pages, t_new=t, steps_per_tile=spt,
                          final_norm=final_norm),
        grid_spec=grid_spec,
        out_shape=(jax.ShapeDtypeStruct((b, t, FOX_DIM), BF16), jax.ShapeDtypeStruct((rows, d), F32)),
        compiler_params=_params("arbitrary", "arbitrary"),
        name="fox_sample_mlp",
    )(page_table, q, kn, vn, gq, grow, cache_kt, cache_vt, suffix, h, x1, x2, wo1, wo2, gf, wu_c, wd_c, gl)


def _mix_mlp_kernel(h_ref, x1_ref, x2_ref, wo1_ref, wo2_ref, gf_ref, wu_ref, wd_ref, gl_ref, o_ref, *,
                    ff_chunk, final_norm):
    h1 = h_ref[...] + _bdot(x1_ref[...], wo1_ref[...]) + _bdot(x2_ref[...], wo2_ref[...])
    xn = _rms(h1, gf_ref[...]).astype(BF16)
    o_ref[...] = h1
    d_ff = wu_ref.shape[1]
    for c in range(d_ff // ff_chunk):
        u = jnp.maximum(_bdot(xn, wu_ref[:, c * ff_chunk:(c + 1) * ff_chunk]), 0.0)
        o_ref[...] += _bdot((u * u).astype(BF16), wd_ref[c * ff_chunk:(c + 1) * ff_chunk, :])
    if final_norm:
        o_ref[...] = _rms(o_ref[...], gl_ref[...])


def _mix_mlp(h, x1, x2, wo1, wo2, gf, wu, wd, gl, final_norm):
    rows, d = h.shape
    tm = _row_tile(rows)
    d_ff = wu.shape[1]
    row = lambda width: pl.BlockSpec((tm, width), lambda i: (i, 0))
    return pl.pallas_call(
        functools.partial(_mix_mlp_kernel, ff_chunk=min(d_ff, 1024), final_norm=final_norm),
        grid=(rows // tm,),
        in_specs=[row(d), row(x1.shape[1]), row(x2.shape[1]), _const_spec(wo1.shape), _const_spec(wo2.shape),
                  _const_spec((1, d)), _const_spec(wu.shape), _const_spec(wd.shape), _const_spec((1, d))],
        out_specs=row(d),
        out_shape=jax.ShapeDtypeStruct((rows, d), F32),
        compiler_params=_params("arbitrary"),
        name="mix_mlp",
    )(h, x1, x2, wo1, wo2, gf, wu, wd, gl)


def _in_odd_kernel(x_ref, g_ref, w_ref, lng_ref, lnb_ref, cos_ref, sin_ref, ws_ref, bs_ref,
                   c_ref, vn_ref, q_ref, k_ref, rv_ref, rg_ref, *, chunk):
    xn = _rms(x_ref[...], g_ref[...]).astype(BF16)
    proj = _bdot(xn, w_ref[...])
    tm = proj.shape[0]
    gd, rd = GMLP_DIM, RET_DIM
    gv = proj[:, gd:2 * gd]
    mu = jnp.mean(gv, axis=-1, keepdims=True)
    var = jnp.mean((gv - mu) ** 2, axis=-1, keepdims=True)
    vn = (gv - mu) * lax.rsqrt(var + EPS) * lng_ref[...] + lnb_ref[...]
    vn_ref[...] = vn
    vnb = vn.astype(BF16)
    wrows = ws_ref.shape[1]
    r = lax.broadcasted_iota(jnp.int32, (wrows, wrows), 0)
    c = lax.broadcasted_iota(jnp.int32, (wrows, wrows), 1)
    causal_in_chunk = (c <= r) & (c >= (r // chunk) * chunk)
    for gi in range(GMLP_GROUPS):
        sl = slice(gi * LANES, (gi + 1) * LANES)
        ws = jnp.where(causal_in_chunk, ws_ref[gi], 0.0).astype(BF16)
        for ti in range(tm // wrows):
            rows = slice(ti * wrows, (ti + 1) * wrows)
            z = _bdot(ws, vnb[rows, sl]) + bs_ref[gi]
            c_ref[rows, sl] = (proj[rows, sl] * z).astype(c_ref.dtype)
    cos = cos_ref[...]
    sin = sin_ref[...]
    o_q, o_k, o_v, o_g = 2 * gd, 2 * gd + rd, 2 * gd + 2 * rd, 2 * gd + 3 * rd
    for hh in range(RET_HEADS):
        sl = slice(hh * RET_HEAD_DIM, (hh + 1) * RET_HEAD_DIM)
        qh = proj[:, o_q + hh * RET_HEAD_DIM:o_q + (hh + 1) * RET_HEAD_DIM]
        kh = proj[:, o_k + hh * RET_HEAD_DIM:o_k + (hh + 1) * RET_HEAD_DIM]
        q_ref[:, sl] = (qh * cos + pltpu.roll(qh, RET_HEAD_DIM // 2, 1) * sin).astype(BF16)
        k_ref[:, sl] = (kh * cos + pltpu.roll(kh, RET_HEAD_DIM // 2, 1) * sin) * (RET_HEAD_DIM ** -0.5)
    rv_ref[...] = proj[:, o_v:o_g].astype(BF16)
    rg_ref[...] = proj[:, o_g:]


def _in_odd(x, g, w, lng, lnb, cos_t, sin_t, gmlp_ws, gmlp_bs, seq_len):
    rows, d = x.shape
    tm = _row_tile(rows)
    chunk = min(seq_len, GMLP_CHUNK)
    assert seq_len % chunk == 0 and tm % chunk == 0
    wrows = chunk if chunk % LANES == 0 else tm
    reps = wrows // chunk
    ws_t = jnp.tile(gmlp_ws[:, :chunk, :chunk], (1, reps, reps))
    bs_t = jnp.broadcast_to(jnp.tile(gmlp_bs[:, :chunk], (1, reps))[:, :, None], (GMLP_GROUPS, wrows, LANES))
    if seq_len >= tm:
        tiles_per_seq = seq_len // tm
        rope_map = lambda i: (i % tiles_per_seq, 0)
    else:
        reps = tm // seq_len
        cos_t = jnp.tile(cos_t, (reps, 1))
        sin_t = jnp.tile(sin_t, (reps, 1))
        rope_map = lambda i: (0, 0)
    row = lambda width: pl.BlockSpec((tm, width), lambda i: (i, 0))
    outs = ((GMLP_DIM, BF16), (GMLP_DIM, F32), (RET_DIM, BF16), (RET_DIM, F32), (RET_DIM, BF16), (RET_DIM, F32))
    return pl.pallas_call(
        functools.partial(_in_odd_kernel, chunk=chunk),
        grid=(rows // tm,),
        in_specs=[row(d), _const_spec((1, d)), _const_spec(w.shape), _const_spec((1, GMLP_DIM)),
                  _const_spec((1, GMLP_DIM)), pl.BlockSpec((tm, RET_HEAD_DIM), rope_map),
                  pl.BlockSpec((tm, RET_HEAD_DIM), rope_map), _const_spec(ws_t.shape), _const_spec(bs_t.shape)],
        out_specs=tuple(row(wd) for wd, _ in outs),
        out_shape=tuple(jax.ShapeDtypeStruct((rows, wd), dt) for wd, dt in outs),
        compiler_params=_params("arbitrary"),
        name="in_odd",
    )(x, g, w, lng, lnb, cos_t, sin_t, ws_t, bs_t)


def _ret_kernel(q_ref, k_ref, v_ref, g_ref, s0_ref, dintra_ref, dq_ref, dk_ref, ds_ref, gn_ref,
                o_ref, sout_ref, st_sc, *, batch_block):
    c = pl.program_id(1)

    @pl.when(c == 0)
    def _():
        st_sc[...] = s0_ref[...]

    for bi in range(batch_block):
        for hh in range(RET_HEADS):
            sl = slice(hh * RET_HEAD_DIM, (hh + 1) * RET_HEAD_DIM)
            q = q_ref[bi, :, sl]
            kf = k_ref[bi, :, sl]
            v = v_ref[bi, :, sl]
            st = st_sc[bi, hh]
            sc = _bdot_nt(q, kf.astype(BF16)) * dintra_ref[hh]
            o = _bdot(sc.astype(BF16), v) + _bdot(q, st.astype(BF16)) * dq_ref[hh]
            st_sc[bi, hh] = st * ds_ref[hh] + _bdot_tn((kf * dk_ref[hh]).astype(BF16), v)
            mu = jnp.mean(o, axis=-1, keepdims=True)
            var = jnp.mean((o - mu) ** 2, axis=-1, keepdims=True)
            on = (o - mu) * lax.rsqrt(var + EPS) * gn_ref[:, sl]
            gate = g_ref[bi, :, sl]
            o_ref[bi, :, sl] = (on * (gate * jax.nn.sigmoid(gate))).astype(o_ref.dtype)

    @pl.when(c == pl.num_programs(1) - 1)
    def _():
        sout_ref[...] = st_sc[...]


def _retention(q, k, v, gate, s0, gn):
    b, t, _ = q.shape
    bb = RET_BATCH_BLOCK if b % RET_BATCH_BLOCK == 0 else 1
    lc = math.gcd(t, RET_CHUNK) if t >= RET_CHUNK else t
    log_g = jnp.log1p(-jnp.exp2(-5.0 - jnp.arange(RET_HEADS, dtype=F32)))
    idx = jnp.arange(lc, dtype=F32)
    diff = idx[:, None] - idx[None, :]
    dintra = jnp.where(diff[None] >= 0, jnp.exp(diff[None] * log_g[:, None, None]), 0.0)
    lanes = (RET_HEADS, lc, RET_HEAD_DIM)
    dq = jnp.broadcast_to(jnp.exp((idx[None, :] + 1.0) * log_g[:, None])[:, :, None], lanes)
    dk = jnp.broadcast_to(jnp.exp((lc - 1.0 - idx)[None, :] * log_g[:, None])[:, :, None], lanes)
    ds = jnp.broadcast_to(jnp.exp(lc * log_g)[:, None, None], (RET_HEADS, 1, RET_HEAD_DIM))
    blk = lambda: pl.BlockSpec((bb, lc, RET_DIM), lambda bi, ci: (bi, ci, 0))
    st = lambda: pl.BlockSpec((bb, RET_HEADS, RET_HEAD_DIM, RET_HEAD_DIM), lambda bi, ci: (bi, 0, 0, 0))
    return pl.pallas_call(
        functools.partial(_ret_kernel, batch_block=bb),
        grid=(b // bb, t // lc),
        in_specs=[blk(), blk(), blk(), blk(), st(), _const_spec(dintra.shape), _const_spec(lanes),
                  _const_spec(lanes), _const_spec(ds.shape), _const_spec((1, RET_DIM))],
        out_specs=(blk(), st()),
        out_shape=(jax.ShapeDtypeStruct((b, t, RET_DIM), BF16),
                   jax.ShapeDtypeStruct((b, RET_HEADS, RET_HEAD_DIM, RET_HEAD_DIM), F32)),
        scratch_shapes=[pltpu.VMEM((bb, RET_HEADS, RET_HEAD_DIM, RET_HEAD_DIM), F32)],
        compiler_params=_params("arbitrary", "arbitrary"),
        name="retention",
    )(q, k, v, gate, s0, dintra, dq, dk, ds, gn)


def _rope_tables(start_pos, t):
    half = RET_HEAD_DIM // 2
    pos = (start_pos + jnp.arange(t)).astype(F32)
    inv = ROPE_BASE ** (-jnp.arange(half, dtype=F32) / half)
    ang = pos[:, None] * inv[None, :]
    cos, sin = jnp.cos(ang), jnp.sin(ang)
    return jnp.concatenate([cos, cos], axis=1), jnp.concatenate([-sin, sin], axis=1)


def _block_diag(w):
    g, n, _ = w.shape
    out = jnp.zeros((g * n, g * n), w.dtype)
    for i in range(g):
        out = out.at[i * n:(i + 1) * n, i * n:(i + 1) * n].set(w[i])
    return out


def _even_front(x, start_pos, hist0, transposed_kv, wts):
    b, t, d = x.shape
    e = dict(zip(("a", "q", "k", "v", "kb", "vb", "lf", "cum", "cumt", "aq", "ak"),
                 _in_even(x.reshape(b * t, d), wts["norm_mix"][0].reshape(1, -1), wts["w_in_even"], wts["b_forget"],
                          t, transposed_kv)))
    e["pool_out"], e["new_hist"] = _pool_mix(e["a"].reshape(b, t, POOL_DIM), hist0, wts["pool_w"], wts["pool_scale"],
                                             start_pos)
    return e


def _odd_front(h, b, t, start_pos, state0, wts):
    row1 = lambda a: a.reshape(1, -1)
    r3 = lambda z: z.reshape(b, t, z.shape[-1])
    cos_t, sin_t = _rope_tables(start_pos, t)
    c_out, vn, rq, rk, rv, rg = _in_odd(h, row1(wts["norm_mix"][1]), wts["w_in_odd"], row1(wts["gmlp_ln_g"]),
                                        row1(wts["gmlp_ln_b"]), cos_t, sin_t, wts["gmlp_ws"], wts["gmlp_bs"], t)
    r_out, s_new = _retention(r3(rq), r3(rk), r3(rv), r3(rg), state0, row1(wts["ret_gn_g"]))
    return c_out, r_out.reshape(b * t, RET_DIM), vn, s_new


def _mlp_args(layer, wts):
    w_out, split = (wts["w_out_even"], POOL_DIM) if layer == 0 else (wts["w_out_odd"], GMLP_DIM)
    return (w_out[:split], w_out[split:], wts["norm_ffn"][layer].reshape(1, -1), wts["w_up"][layer],
            wts["w_down"][layer], wts["norm_final"].reshape(1, -1))


def kernel(x_prompt, x_sample, state_pool, cache_k, cache_v, cache_logf, page_table, state_ret, norm_mix, w_in_even, b_forget, pool_w, pool_scale, w_out_even, w_in_odd, gmlp_ln_g, gmlp_ln_b, gmlp_ws, gmlp_bs, ret_gn_g, w_out_odd, norm_ffn, w_up, w_down, norm_final):
    assert norm_mix.shape[0] == 2 and w_in_even.shape[0] == 1 and w_in_odd.shape[0] == 1
    d = x_prompt.shape[-1]
    ev_cols = w_in_even.shape[-1]
    ev_pad = POOL_DIM + 3 * FOX_DIM + LANES - ev_cols
    wts = {
        "norm_mix": norm_mix, "norm_ffn": norm_ffn, "norm_final": norm_final,
        "w_in_even": jnp.pad(w_in_even[0], ((0, 0), (0, ev_pad))).astype(BF16),
        "b_forget": jnp.pad(b_forget[0], (0, LANES - FOX_HEADS)).reshape(1, LANES),
        "pool_w": _block_diag(pool_w[0]).astype(BF16),
        "pool_scale": pool_scale[0].reshape(1, POOL_DIM),
        "w_out_even": w_out_even[0].astype(BF16),
        "w_in_odd": w_in_odd[0].astype(BF16),
        "gmlp_ln_g": gmlp_ln_g[0], "gmlp_ln_b": gmlp_ln_b[0], "gmlp_ws": gmlp_ws[0], "gmlp_bs": gmlp_bs[0],
        "ret_gn_g": ret_gn_g[0],
        "w_out_odd": w_out_odd[0].astype(BF16),
        "w_up": w_up.astype(BF16), "w_down": w_down.astype(BF16),
    }
    b, s, _ = x_prompt.shape
    db, t, _ = x_sample.shape
    n_pages = page_table.shape[1]
    page = cache_k.shape[2]
    past = n_pages * page

    cache_kt = jnp.transpose(cache_k[0], (0, 2, 3, 1))
    cache_vt = jnp.transpose(cache_v[0], (0, 2, 3, 1))
    suffix = _page_suffix(jnp.transpose(cache_logf[0], (2, 0, 1)))

    se = _even_front(x_sample, past, state_pool[0], False, wts)
    pe = _even_front(x_prompt, 0, jnp.zeros((b, POOL_HIST, POOL_DIM), F32), True, wts)
    p3 = lambda z: z.reshape(b, s, z.shape[-1])
    att_p = _fox_prompt(p3(pe["q"]), p3(pe["kb"]), p3(pe["vb"]), p3(pe["aq"]), p3(pe["ak"]))

    assert db % 2 == 0
    half = db // 2
    s3 = lambda z: z.reshape(db, t, z.shape[-1])
    grow = jnp.transpose(s3(se["cum"]), (0, 2, 1))[:, :HEAD_ROWS, :]
    gq = jnp.broadcast_to(grow[:, :FOX_HEADS, :, None], (db, FOX_HEADS, t, LANES)).reshape(db, FOX_HEADS * t, LANES)
    sq, skb, svb = s3(se["q"]), s3(se["kb"]), s3(se["vb"])

    def attend_half(sel, h, x1, x2, layer):
        return _fox_sample_mlp(page_table[sel], sq[sel], skb[sel], svb[sel], gq[sel], grow[sel], cache_kt, cache_vt,
                               suffix, h, x1, x2, *_mlp_args(layer, wts), layer == 1)

    att_s0, h_p = attend_half(slice(0, half), x_prompt.reshape(b * s, d), pe["pool_out"].reshape(b * s, POOL_DIM),
                              att_p.reshape(b * s, FOX_DIM), 0)
    c_p, r_p, _, ret_p = _odd_front(h_p, b, s, 0, jnp.zeros((b, RET_HEADS, RET_HEAD_DIM, RET_HEAD_DIM), F32), wts)
    att_s1, y_p = attend_half(slice(half, db), h_p, c_p, r_p, 1)

    att_s = jnp.concatenate([att_s0, att_s1], axis=0)
    h_s = _mix_mlp(x_sample.reshape(db * t, d), se["pool_out"].reshape(db * t, POOL_DIM),
                   att_s.reshape(db * t, FOX_DIM), *_mlp_args(0, wts), False)
    c_s, r_s, gv_s, ret_s = _odd_front(h_s, db, t, past, state_ret[0], wts)
    y_s = _mix_mlp(h_s, c_s, r_s, *_mlp_args(1, wts), True)

    kv_p = lambda z: jnp.transpose(z.reshape(b, FOX_HEADS, FOX_HEAD_DIM, s), (0, 3, 1, 2))[None]
    kv_s = lambda z: z.reshape(1, db, t, FOX_HEADS, FOX_HEAD_DIM)
    lf_out = lambda e, n, m: e["lf"][:, :FOX_HEADS].reshape(1, n, m, FOX_HEADS)
    return (y_p.reshape(b, s, d), y_s.reshape(db, t, d), pe["new_hist"][None], kv_p(pe["k"]), kv_p(pe["v"]),
            lf_out(pe, b, s), ret_p[None], se["new_hist"][None], kv_s(se["k"]), kv_s(se["v"]), lf_out(se, db, t),
            gv_s.reshape(1, db, t, GMLP_DIM), ret_s[None])
```

```python
import functools
import math

import numpy as np
import jax
import jax.numpy as jnp
from jax import lax
from jax.experimental import pallas as pl
from jax.experimental.pallas import tpu as pltpu

F32 = jnp.float32
BF16 = jnp.bfloat16
EPS = 1e-6

POOL_WINDOWS = (2, 4, 8, 16)
POOL_GROUP_DIM = 64
POOL_DIM = 256
POOL_HIST = 15
FOX_HEAD_DIM = 64
FOX_HEADS = 12
FOX_DIM = FOX_HEADS * FOX_HEAD_DIM
GMLP_CHUNK = 128
GMLP_GROUPS = 4
GMLP_DIM = 512
RET_HEADS = 4
RET_DIM = 512
RET_HEAD_DIM = 128
RET_CHUNK = 128
ROPE_BASE = 10000.0

LANES = 128
SUBLANES = 8
HEAD_ROWS = 16
SPLIT_STRIDE = 16
NEG_BIG = -1e30
LOG2E = math.log2(math.e)
VMEM_LIMIT = 56 * 1024 * 1024
ROW_TILE = 512
ATT_TILE = 512
PAGES_PER_STEP = 16
FUSED_ROW_TILE = 256
RET_BATCH_BLOCK = 8
SUFFIX_PAGE_TILE = 256


def _row_tile(rows):
    return ROW_TILE if rows % ROW_TILE == 0 else rows


def _params(*sem):
    return pltpu.CompilerParams(dimension_semantics=sem, vmem_limit_bytes=VMEM_LIMIT)


def _const_spec(shape):
    nd = len(shape)
    return pl.BlockSpec(shape, lambda *_: (0,) * nd, pipeline_mode=pl.Buffered(1))


def _rms(x, g):
    return x * lax.rsqrt(jnp.mean(x * x, axis=-1, keepdims=True) + EPS) * g


def _bdot(a, b):
    return jnp.dot(a, b, preferred_element_type=F32)


def _bdot_nt(a, b):
    return lax.dot_general(a, b, (((1,), (1,)), ((), ())), preferred_element_type=F32)


def _bdot_tn(a, b):
    return lax.dot_general(a, b, (((0,), (0,)), ((), ())), preferred_element_type=F32)


def _split3(x):
    hi = x.astype(BF16)
    r = x - hi.astype(F32)
    mid = r.astype(BF16)
    lo = (r - mid.astype(F32)).astype(BF16)
    return hi, mid, lo


def _log_sigmoid(x):
    return jnp.minimum(x, 0.0) - jnp.log1p(jnp.exp(-jnp.abs(x)))


def _aug_tables():
    perm = np.zeros((LANES, 2 * FOX_DIM), np.float32)
    const = np.zeros((1, 2 * FOX_DIM), np.float32)
    for h in range(FOX_HEADS):
        base = (h // 2) * LANES + (FOX_HEAD_DIM if h % 2 == 0 else 0)
        for j in range(3):
            perm[SPLIT_STRIDE * j + h, base + j] = 1.0
            const[0, FOX_DIM + base + j] = 1.0
            const[0, base + 3 + j] = 1.0
            perm[SPLIT_STRIDE * j + h, FOX_DIM + base + 3 + j] = -1.0
    return jnp.asarray(perm, BF16), jnp.asarray(const)


def _in_even_kernel(x_ref, g_ref, w_ref, bf_ref, tri_ref, perm_ref, const_ref,
                    a_ref, q_ref, k_ref, v_ref, kb_ref, vb_ref, lf_ref, cum_ref, cumt_ref, aq_ref, ak_ref,
                    carry_ref, *, tiles_per_seq, transposed_kv):
    i = pl.program_id(0)
    xn = _rms(x_ref[...], g_ref[...]).astype(BF16)
    proj = _bdot(xn, w_ref[...])
    o_q, o_k, o_v, o_f = POOL_DIM, POOL_DIM + FOX_DIM, POOL_DIM + 2 * FOX_DIM, POOL_DIM + 3 * FOX_DIM
    a_ref[...] = proj[:, :o_q]
    q_ref[...] = (proj[:, o_q:o_k] * (FOX_HEAD_DIM ** -0.5 * LOG2E)).astype(BF16)
    k = proj[:, o_k:o_v]
    v = proj[:, o_v:o_f]
    if transposed_kv:
        k_ref[0] = k.T
        v_ref[0] = v.T
    else:
        k_ref[...] = k
        v_ref[...] = v
    kb_ref[...] = k.astype(BF16)
    vb_ref[...] = v.astype(BF16)
    lf = _log_sigmoid(proj[:, o_f:] + bf_ref[...])
    lf_ref[...] = lf
    tri3 = _bdot(tri_ref[...], jnp.concatenate(_split3(lf), axis=1))
    cum = tri3[:, :LANES] + tri3[:, LANES:2 * LANES] + tri3[:, 2 * LANES:]
    if tiles_per_seq > 1:
        @pl.when(i % tiles_per_seq == 0)
        def _():
            carry_ref[...] = jnp.zeros_like(carry_ref)
        cum = cum + carry_ref[...]
        carry_ref[...] = cum[-1:, :]
    cum = cum * LOG2E
    cum_ref[...] = cum
    cumt_ref[0] = cum.T[:HEAD_ROWS, :]
    is_head = lax.broadcasted_iota(jnp.int32, (1, LANES), 1) < FOX_HEADS
    packed = sum(pltpu.roll(jnp.where(is_head, p.astype(F32), 0.0), SPLIT_STRIDE * j, 1)
                 for j, p in enumerate(_split3(cum)))
    aug = (_bdot(packed.astype(BF16), perm_ref[...]) + const_ref[...]).astype(BF16)
    aq_ref[...] = aug[:, :FOX_DIM]
    ak_ref[...] = aug[:, FOX_DIM:]


def _in_even(x, g, w, bf, seq_len, transposed_kv):
    rows, d = x.shape
    tm = _row_tile(rows)
    n_tiles = rows // tm
    r = np.arange(tm)
    if seq_len >= tm:
        assert seq_len % tm == 0
        tiles_per_seq = seq_len // tm
        tri = r[None, :] <= r[:, None]
        cumt_shape = (rows // seq_len, HEAD_ROWS, seq_len)
        cumt_map = lambda i: (i // tiles_per_seq, 0, i % tiles_per_seq)
    else:
        assert tm % seq_len == 0 and not transposed_kv
        tiles_per_seq = 1
        tri = (r[None, :] <= r[:, None]) & ((r[None, :] // seq_len) == (r[:, None] // seq_len))
        cumt_shape = (n_tiles, HEAD_ROWS, tm)
        cumt_map = lambda i: (i, 0, 0)
    tri = jnp.asarray(tri, BF16)
    perm, const = _aug_tables()
    n = w.shape[1]
    row = lambda width: pl.BlockSpec((tm, width), lambda i: (i, 0))
    if transposed_kv:
        kv_shape = jax.ShapeDtypeStruct((rows // seq_len, FOX_DIM, seq_len), F32)
        kv_spec = lambda: pl.BlockSpec((1, FOX_DIM, tm), cumt_map)
    else:
        kv_shape = jax.ShapeDtypeStruct((rows, FOX_DIM), F32)
        kv_spec = lambda: row(FOX_DIM)
    out_shape = (
        jax.ShapeDtypeStruct((rows, POOL_DIM), F32),
        jax.ShapeDtypeStruct((rows, FOX_DIM), BF16),
        kv_shape,
        kv_shape,
        jax.ShapeDtypeStruct((rows, FOX_DIM), BF16),
        jax.ShapeDtypeStruct((rows, FOX_DIM), BF16),
        jax.ShapeDtypeStruct((rows, LANES), F32),
        jax.ShapeDtypeStruct((rows, LANES), F32),
        jax.ShapeDtypeStruct(cumt_shape, F32),
        jax.ShapeDtypeStruct((rows, FOX_DIM), BF16),
        jax.ShapeDtypeStruct((rows, FOX_DIM), BF16),
    )
    out_specs = (row(POOL_DIM), row(FOX_DIM), kv_spec(), kv_spec(), row(FOX_DIM), row(FOX_DIM),
                 row(LANES), row(LANES), pl.BlockSpec((1, HEAD_ROWS, tm), cumt_map), row(FOX_DIM), row(FOX_DIM))
    return pl.pallas_call(
        functools.partial(_in_even_kernel, tiles_per_seq=tiles_per_seq, transposed_kv=transposed_kv),
        grid=(n_tiles,),
        in_specs=[row(d), _const_spec((1, d)), _const_spec((d, n)), _const_spec((1, LANES)),
                  _const_spec((tm, tm)), _const_spec(perm.shape), _const_spec(const.shape)],
        out_specs=out_specs,
        out_shape=out_shape,
        scratch_shapes=[pltpu.VMEM((1, LANES), F32)],
        compiler_params=_params("arbitrary"),
        name="in_even",
    )(x, g, w, bf, tri, perm, const)


def _pool_kernel(a_ref, hist_ref, w_ref, scale_ref, out_ref, newhist_ref, ext_ref, *, seq_len, start_pos):
    t = seq_len
    base = POOL_HIST + 1
    x0 = a_ref[0]
    ext_ref[0:1, :] = jnp.zeros((1, POOL_DIM), F32)
    ext_ref[1:base, :] = hist_ref[0]
    ext_ref[base:base + t, :] = x0

    def back(kk):
        return ext_ref[base - kk:base - kk + t, :]

    sums = []
    run = x0
    nxt = 1
    for w in POOL_WINDOWS:
        while nxt < w:
            run = run + back(nxt)
            nxt += 1
        sums.append(run)
    grp = lax.broadcasted_iota(jnp.int32, (1, POOL_DIM), 1) // POOL_GROUP_DIM
    sel = sums[-1]
    win = jnp.full((1, POOL_DIM), POOL_WINDOWS[-1], jnp.int32)
    for gi in range(len(POOL_WINDOWS) - 2, -1, -1):
        sel = jnp.where(grp == gi, sums[gi], sel)
        win = jnp.where(grp == gi, POOL_WINDOWS[gi], win)
    pos = start_pos + lax.broadcasted_iota(jnp.int32, (t, 1), 0)
    cnt = jnp.minimum(pos + 1, win).astype(F32)
    dd = sel / cnt - x0
    y = _bdot(dd.astype(BF16), w_ref[...]) * scale_ref[...]
    out_ref[0] = y.astype(out_ref.dtype)
    newhist_ref[0] = ext_ref[t + 1:t + base, :]


def _pool_mix(a, hist, w_bd, scale, start_pos):
    b, t, _ = a.shape
    return pl.pallas_call(
        functools.partial(_pool_kernel, seq_len=t, start_pos=start_pos),
        grid=(b,),
        in_specs=[pl.BlockSpec((1, t, POOL_DIM), lambda i: (i, 0, 0)),
                  pl.BlockSpec((1, POOL_HIST, POOL_DIM), lambda i: (i, 0, 0)),
                  _const_spec((POOL_DIM, POOL_DIM)), _const_spec((1, POOL_DIM))],
        out_specs=(pl.BlockSpec((1, t, POOL_DIM), lambda i: (i, 0, 0)),
                   pl.BlockSpec((1, POOL_HIST, POOL_DIM), lambda i: (i, 0, 0))),
        out_shape=(jax.ShapeDtypeStruct((b, t, POOL_DIM), BF16),
                   jax.ShapeDtypeStruct((b, POOL_HIST, POOL_DIM), F32)),
        scratch_shapes=[pltpu.VMEM((POOL_HIST + 1 + t, POOL_DIM), F32)],
        compiler_params=_params("arbitrary"),
        name="pool_mix",
    )(a, hist, w_bd, scale)


def _fox_prompt_kernel(q_ref, k_ref, v_ref, aq_ref, ak_ref, o_ref, *, tile):
    s_len = q_ref.shape[1]
    lane = lax.broadcasted_iota(jnp.int32, (1, LANES), 1)
    in_head = (lane < FOX_HEAD_DIM, lane >= FOX_HEAD_DIM)
    visible = (lax.broadcasted_iota(jnp.int32, (tile, tile), 1) <= lax.broadcasted_iota(jnp.int32, (tile, tile), 0))
    km = []
    for ki in range(s_len // tile):
        rows = slice(ki * tile, (ki + 1) * tile)
        km.append([jnp.where(in_head[e], k_ref[0, rows, :], ak_ref[0, rows, :]) for e in range(2)])
    for qi in range(s_len // tile):
        qrows = slice(qi * tile, (qi + 1) * tile)
        outs = []
        for e in range(2):
            qm = jnp.where(in_head[e], q_ref[0, qrows, :], aq_ref[0, qrows, :])
            m = jnp.full((tile, 1), NEG_BIG, F32)
            l = jnp.zeros((tile, 1), F32)
            acc = jnp.zeros((tile, LANES), F32)
            for ki in range(qi + 1):
                s = _bdot_nt(qm, km[ki][e])
                if ki == qi:
                    s = jnp.where(visible, s, NEG_BIG)
                m_new = jnp.maximum(m, jnp.max(s, axis=-1, keepdims=True))
                alpha = jnp.exp2(m - m_new)
                p = jnp.exp2(s - m_new)
                l = alpha * l + jnp.sum(p, axis=-1, keepdims=True)
                acc = alpha * acc + _bdot(p.astype(BF16), v_ref[0, ki * tile:(ki + 1) * tile, :])
                m = m_new
            outs.append(acc / l)
        o_ref[0, qrows, :] = jnp.where(in_head[0], outs[0], outs[1]).astype(o_ref.dtype)


def _fox_prompt(q, kb, vb, aq, ak):
    b, s, _ = q.shape
    tile = ATT_TILE if s % ATT_TILE == 0 else s
    assert s // tile <= 4, "the causal tile structure is unrolled in the kernel"
    full = lambda: pl.BlockSpec((1, s, LANES), lambda bi, hp: (bi, 0, hp))
    return pl.pallas_call(
        functools.partial(_fox_prompt_kernel, tile=tile),
        grid=(b, FOX_HEADS // 2),
        in_specs=[full(), full(), full(), full(), full()],
        out_specs=full(),
        out_shape=jax.ShapeDtypeStruct((b, s, FOX_DIM), BF16),
        compiler_params=_params("arbitrary", "arbitrary"),
        name="fox_prompt",
    )(q, kb, vb, aq, ak)


def _suffix_kernel(lf_ref, mat_ref, out_ref):
    for h in range(FOX_HEADS):
        out_ref[:, h, :] = sum(_bdot(p, mat_ref[...]) for p in _split3(lf_ref[h])) * LOG2E


def _page_suffix(lf_hpp):
    heads, n_phys, page = lf_hpp.shape
    tp = SUFFIX_PAGE_TILE if n_phys % SUFFIX_PAGE_TILE == 0 else n_phys
    r = np.arange(page)
    later = r[:, None] > r[None, :]
    mat = jnp.asarray(np.concatenate([later, np.ones((page, page), bool)], axis=1), BF16)
    return pl.pallas_call(
        _suffix_kernel,
        grid=(n_phys // tp,),
        in_specs=[pl.BlockSpec((heads, tp, page), lambda i: (0, i, 0)), _const_spec((page, 2 * page))],
        out_specs=pl.BlockSpec((tp, heads, 2 * page), lambda i: (i, 0, 0)),
        out_shape=jax.ShapeDtypeStruct((n_phys, heads, 2 * page), F32),
        compiler_params=_params("arbitrary"),
        name="page_suffix",
    )(lf_hpp, mat)


def _fox_sample_mlp_kernel(pt_ref, q_ref, kn_ref, vn_ref, gq_ref, grow_ref, *rest, n_pg, t_new, steps_per_tile,
                           ff_chunk, final_norm):
    k_refs = rest[:n_pg]
    v_refs = rest[n_pg:2 * n_pg]
    s_refs = rest[2 * n_pg:3 * n_pg]
    (h_ref, x1_ref, x2_ref, wo1_ref, wo2_ref, gf_ref, wu_ref, wd_ref, gl_ref,
     o_ref, y_ref, m_sc, l_sc, acc_sc, run_sc, xn_sc) = rest[3 * n_pg:]
    i = pl.program_id(1)
    last = pl.num_programs(1) - 1
    page = LANES
    chunk = (pl.program_id(0) * pl.num_programs(1) + i) % steps_per_tile

    @pl.when(i == 0)
    def _():
        m_sc[...] = jnp.full_like(m_sc, NEG_BIG)
        l_sc[...] = jnp.zeros_like(l_sc)
        acc_sc[...] = jnp.zeros_like(acc_sc)
        run_sc[...] = jnp.zeros_like(run_sc)

    @pl.when(chunk == 0)
    def _():
        h1 = h_ref[...] + _bdot(x1_ref[...], wo1_ref[...]) + _bdot(x2_ref[...], wo2_ref[...])
        y_ref[...] = h1
        xn_sc[...] = _rms(h1, gf_ref[...]).astype(BF16)

    run = run_sc[...]
    bias = [None] * n_pg
    for j in range(n_pg - 1, -1, -1):
        sf = s_refs[j][0]
        bias[j] = sf[:, :page] + run
        run = run + sf[:, page:]
    run_sc[...] = run

    t = t_new
    q = q_ref[0]
    gq = gq_ref[0]
    s_parts = []
    for h in range(FOX_HEADS):
        hs = slice(h * FOX_HEAD_DIM, (h + 1) * FOX_HEAD_DIM)
        kt = jnp.concatenate([k_refs[j][0, h] for j in range(n_pg)], axis=1).astype(BF16)
        s_parts.append(_bdot(q[:, hs], kt) + jnp.concatenate([bias[j][h:h + 1, :] for j in range(n_pg)], axis=1))
    s = jnp.concatenate(s_parts, axis=0) + jnp.concatenate([gq] * n_pg, axis=1)
    ff0 = pl.multiple_of(chunk * ff_chunk, ff_chunk)
    u = jnp.maximum(_bdot(xn_sc[...], wu_ref[:, pl.ds(ff0, ff_chunk)]), 0.0)
    u = (u * u).astype(BF16)
    m_prev = m_sc[...]
    m_new = jnp.maximum(m_prev, jnp.max(s, axis=-1, keepdims=True))
    alpha = jnp.exp2(m_prev - m_new)
    p32 = jnp.exp2(s - jnp.concatenate([m_new] * n_pg, axis=1))
    l_sc[...] = alpha * l_sc[...] + jnp.sum(p32, axis=-1, keepdims=True)
    m_sc[...] = m_new
    pv = []
    for h in range(FOX_HEADS):
        vt = jnp.concatenate([v_refs[j][0, h] for j in range(n_pg)], axis=1).astype(BF16)
        pv.append(_bdot_nt(p32[h * t:(h + 1) * t].astype(BF16), vt))
    acc_sc[...] = alpha[:, :FOX_HEAD_DIM] * acc_sc[...] + jnp.concatenate(pv, axis=0)

    y_ref[...] += _bdot(u, wd_ref[pl.ds(ff0, ff_chunk), :])
    if final_norm:
        @pl.when(chunk == steps_per_tile - 1)
        def _():
            y_ref[...] = _rms(y_ref[...], gl_ref[...])

    @pl.when(i == last)
    def _():
        kn = kn_ref[0]
        vn = vn_ref[0]
        grow = grow_ref[0]
        r = lax.broadcasted_iota(jnp.int32, (t, t), 0)
        c = lax.broadcasted_iota(jnp.int32, (t, t), 1)
        for h in range(FOX_HEADS):
            hs = slice(h * FOX_HEAD_DIM, (h + 1) * FOX_HEAD_DIM)
            rows = slice(h * t, (h + 1) * t)
            s = _bdot_nt(q[:, hs], kn[:, hs]) + (gq[rows, :1] - grow[h:h + 1, :])
            s = jnp.where(c <= r, s, NEG_BIG)
            m_prev = m_sc[rows, :1]
            m_new = jnp.maximum(m_prev, jnp.max(s, axis=-1, keepdims=True))
            alpha = jnp.exp2(m_prev - m_new)
            p32 = jnp.exp2(s - m_new)
            l = alpha * l_sc[rows, :1] + jnp.sum(p32, axis=-1, keepdims=True)
            acc = alpha * acc_sc[rows, :] + _bdot(p32.astype(BF16), vn[:, hs])
            o_ref[0, :, hs] = (acc / l).astype(o_ref.dtype)


def _fox_sample_mlp(page_table, q, kn, vn, gq, grow, cache_kt, cache_vt, suffix,
                    h, x1, x2, wo1, wo2, gf, wu, wd, gl, final_norm):
    b, t, _ = q.shape
    n_pages = page_table.shape[1]
    n_pg = PAGES_PER_STEP if n_pages % PAGES_PER_STEP == 0 else 1
    steps = n_pages // n_pg
    page = cache_kt.shape[-1]
    assert page == LANES
    rows, d = h.shape
    d_ff = wu.shape[1]
    tm = FUSED_ROW_TILE if rows % FUSED_ROW_TILE == 0 else rows
    tiles = rows // tm
    assert (b * steps) % tiles == 0, "grid steps must split evenly over the MLP row tiles"
    spt = (b * steps) // tiles
    assert d_ff % spt == 0 and (d_ff // spt) % LANES == 0
    ck = d_ff // spt
    row = lambda width: pl.BlockSpec((tm, width), lambda bi, i, pt: ((bi * steps + i) // spt, 0))

    def page_idx(bi, i, pt, j):
        return pt[bi, n_pages - n_pg * (i + 1) + j]

    tok = lambda width: pl.BlockSpec((1, t, width), lambda bi, i, pt: (bi, 0, 0))
    kv_specs = [pl.BlockSpec((1, FOX_HEADS, FOX_HEAD_DIM, page),
                             functools.partial(lambda bi, i, pt, j: (page_idx(bi, i, pt, j), 0, 0, 0), j=j))
                for j in range(n_pg)]
    sfx_specs = [pl.BlockSpec((1, FOX_HEADS, 2 * page),
                              functools.partial(lambda bi, i, pt, j: (page_idx(bi, i, pt, j), 0, 0), j=j))
                 for j in range(n_pg)]
    grid_spec = pltpu.PrefetchScalarGridSpec(
        num_scalar_prefetch=1,
        grid=(b, steps),
        in_specs=[tok(FOX_DIM), tok(FOX_DIM), tok(FOX_DIM),
                  pl.BlockSpec((1, FOX_HEADS * t, LANES), lambda bi, i, pt: (bi, 0, 0)),
                  pl.BlockSpec((1, HEAD_ROWS, t), lambda bi, i, pt: (bi, 0, 0))]
                 + kv_specs + kv_specs + sfx_specs
                 + [row(d), row(x1.shape[1]), row(x2.shape[1]), _const_spec(wo1.shape), _const_spec(wo2.shape),
                    _const_spec((1, d)), _const_spec(wu.shape), _const_spec(wd.shape), _const_spec((1, d))],
        out_specs=(tok(FOX_DIM), row(d)),
        scratch_shapes=[pltpu.VMEM((FOX_HEADS * t, LANES), F32), pltpu.VMEM((FOX_HEADS * t, LANES), F32),
                        pltpu.VMEM((FOX_HEADS * t, FOX_HEAD_DIM), F32), pltpu.VMEM((FOX_HEADS, LANES), F32),
                        pltpu.VMEM((tm, d), BF16)],
    )
    return pl.pallas_call(
        functools.partial(_fox_sample_mlp_kernel, n_pg=n_pg, t_new=t, steps_per_tile=spt, ff_chunk=ck,
                          final_norm=final_norm),
        grid_spec=grid_spec,
        out_shape=(jax.ShapeDtypeStruct((b, t, FOX_DIM), BF16), jax.ShapeDtypeStruct((rows, d), F32)),
        compiler_params=_params("arbitrary", "arbitrary"),
        name="fox_sample_mlp",
    )(page_table, q, kn, vn, gq, grow, *([cache_kt] * n_pg), *([cache_vt] * n_pg), *([suffix] * n_pg),
      h, x1, x2, wo1, wo2, gf, wu, wd, gl)


def _mix_mlp_kernel(h_ref, x1_ref, x2_ref, wo1_ref, wo2_ref, gf_ref, wu_ref, wd_ref, gl_ref, o_ref, *,
                    ff_chunk, final_norm):
    h1 = h_ref[...] + _bdot(x1_ref[...], wo1_ref[...]) + _bdot(x2_ref[...], wo2_ref[...])
    xn = _rms(h1, gf_ref[...]).astype(BF16)
    o_ref[...] = h1
    d_ff = wu_ref.shape[1]
    for c in range(d_ff // ff_chunk):
        u = jnp.maximum(_bdot(xn, wu_ref[:, c * ff_chunk:(c + 1) * ff_chunk]), 0.0)
        o_ref[...] += _bdot((u * u).astype(BF16), wd_ref[c * ff_chunk:(c + 1) * ff_chunk, :])
    if final_norm:
        o_ref[...] = _rms(o_ref[...], gl_ref[...])


def _mix_mlp(h, x1, x2, wo1, wo2, gf, wu, wd, gl, final_norm):
    rows, d = h.shape
    tm = _row_tile(rows)
    d_ff = wu.shape[1]
    row = lambda width: pl.BlockSpec((tm, width), lambda i: (i, 0))
    return pl.pallas_call(
        functools.partial(_mix_mlp_kernel, ff_chunk=min(d_ff, 1024), final_norm=final_norm),
        grid=(rows // tm,),
        in_specs=[row(d), row(x1.shape[1]), row(x2.shape[1]), _const_spec(wo1.shape), _const_spec(wo2.shape),
                  _const_spec((1, d)), _const_spec(wu.shape), _const_spec(wd.shape), _const_spec((1, d))],
        out_specs=row(d),
        out_shape=jax.ShapeDtypeStruct((rows, d), F32),
        compiler_params=_params("arbitrary"),
        name="mix_mlp",
    )(h, x1, x2, wo1, wo2, gf, wu, wd, gl)


def _in_odd_kernel(x_ref, g_ref, w_ref, lng_ref, lnb_ref, cos_ref, sin_ref, ws_ref, bs_ref,
                   c_ref, vn_ref, q_ref, k_ref, rv_ref, rg_ref, *, chunk):
    xn = _rms(x_ref[...], g_ref[...]).astype(BF16)
    proj = _bdot(xn, w_ref[...])
    tm = proj.shape[0]
    gd, rd = GMLP_DIM, RET_DIM
    gv = proj[:, gd:2 * gd]
    mu = jnp.mean(gv, axis=-1, keepdims=True)
    var = jnp.mean((gv - mu) ** 2, axis=-1, keepdims=True)
    vn = (gv - mu) * lax.rsqrt(var + EPS) * lng_ref[...] + lnb_ref[...]
    vn_ref[...] = vn
    vnb = vn.astype(BF16)
    wrows = ws_ref.shape[1]
    r = lax.broadcasted_iota(jnp.int32, (wrows, wrows), 0)
    c = lax.broadcasted_iota(jnp.int32, (wrows, wrows), 1)
    causal_in_chunk = (c <= r) & (c >= (r // chunk) * chunk)
    for gi in range(GMLP_GROUPS):
        sl = slice(gi * LANES, (gi + 1) * LANES)
        ws = jnp.where(causal_in_chunk, ws_ref[gi], 0.0).astype(BF16)
        for ti in range(tm // wrows):
            rows = slice(ti * wrows, (ti + 1) * wrows)
            z = _bdot(ws, vnb[rows, sl]) + bs_ref[gi]
            c_ref[rows, sl] = (proj[rows, sl] * z).astype(c_ref.dtype)
    cos = cos_ref[...]
    sin = sin_ref[...]
    o_q, o_k, o_v, o_g = 2 * gd, 2 * gd + rd, 2 * gd + 2 * rd, 2 * gd + 3 * rd
    for hh in range(RET_HEADS):
        sl = slice(hh * RET_HEAD_DIM, (hh + 1) * RET_HEAD_DIM)
        qh = proj[:, o_q + hh * RET_HEAD_DIM:o_q + (hh + 1) * RET_HEAD_DIM]
        kh = proj[:, o_k + hh * RET_HEAD_DIM:o_k + (hh + 1) * RET_HEAD_DIM]
        q_ref[:, sl] = (qh * cos + pltpu.roll(qh, RET_HEAD_DIM // 2, 1) * sin).astype(BF16)
        k_ref[:, sl] = (kh * cos + pltpu.roll(kh, RET_HEAD_DIM // 2, 1) * sin) * (RET_HEAD_DIM ** -0.5)
    rv_ref[...] = proj[:, o_v:o_g].astype(BF16)
    rg_ref[...] = proj[:, o_g:]


def _in_odd(x, g, w, lng, lnb, cos_t, sin_t, gmlp_ws, gmlp_bs, seq_len):
    rows, d = x.shape
    tm = _row_tile(rows)
    chunk = min(seq_len, GMLP_CHUNK)
    assert seq_len % chunk == 0 and tm % chunk == 0
    wrows = chunk if chunk % LANES == 0 else tm
    reps = wrows // chunk
    ws_t = jnp.tile(gmlp_ws[:, :chunk, :chunk], (1, reps, reps))
    bs_t = jnp.broadcast_to(jnp.tile(gmlp_bs[:, :chunk], (1, reps))[:, :, None], (GMLP_GROUPS, wrows, LANES))
    if seq_len >= tm:
        tiles_per_seq = seq_len // tm
        rope_map = lambda i: (i % tiles_per_seq, 0)
    else:
        reps = tm // seq_len
        cos_t = jnp.tile(cos_t, (reps, 1))
        sin_t = jnp.tile(sin_t, (reps, 1))
        rope_map = lambda i: (0, 0)
    row = lambda width: pl.BlockSpec((tm, width), lambda i: (i, 0))
    outs = ((GMLP_DIM, BF16), (GMLP_DIM, F32), (RET_DIM, BF16), (RET_DIM, F32), (RET_DIM, BF16), (RET_DIM, F32))
    return pl.pallas_call(
        functools.partial(_in_odd_kernel, chunk=chunk),
        grid=(rows // tm,),
        in_specs=[row(d), _const_spec((1, d)), _const_spec(w.shape), _const_spec((1, GMLP_DIM)),
                  _const_spec((1, GMLP_DIM)), pl.BlockSpec((tm, RET_HEAD_DIM), rope_map),
                  pl.BlockSpec((tm, RET_HEAD_DIM), rope_map), _const_spec(ws_t.shape), _const_spec(bs_t.shape)],
        out_specs=tuple(row(wd) for wd, _ in outs),
        out_shape=tuple(jax.ShapeDtypeStruct((rows, wd), dt) for wd, dt in outs),
        compiler_params=_params("arbitrary"),
        name="in_odd",
    )(x, g, w, lng, lnb, cos_t, sin_t, ws_t, bs_t)


def _ret_kernel(q_ref, k_ref, v_ref, g_ref, s0_ref, dintra_ref, dq_ref, dk_ref, ds_ref, gn_ref,
                o_ref, sout_ref, st_sc, *, batch_block):
    c = pl.program_id(1)

    @pl.when(c == 0)
    def _():
        st_sc[...] = s0_ref[...]

    for bi in range(batch_block):
        for hh in range(RET_HEADS):
            sl = slice(hh * RET_HEAD_DIM, (hh + 1) * RET_HEAD_DIM)
            q = q_ref[bi, :, sl]
            kf = k_ref[bi, :, sl]
            v = v_ref[bi, :, sl]
            st = st_sc[bi, hh]
            sc = _bdot_nt(q, kf.astype(BF16)) * dintra_ref[hh]
            o = _bdot(sc.astype(BF16), v) + _bdot(q, st.astype(BF16)) * dq_ref[hh]
            st_sc[bi, hh] = st * ds_ref[hh] + _bdot_tn((kf * dk_ref[hh]).astype(BF16), v)
            mu = jnp.mean(o, axis=-1, keepdims=True)
            var = jnp.mean((o - mu) ** 2, axis=-1, keepdims=True)
            on = (o - mu) * lax.rsqrt(var + EPS) * gn_ref[:, sl]
            gate = g_ref[bi, :, sl]
            o_ref[bi, :, sl] = (on * (gate * jax.nn.sigmoid(gate))).astype(o_ref.dtype)

    @pl.when(c == pl.num_programs(1) - 1)
    def _():
        sout_ref[...] = st_sc[...]


def _retention(q, k, v, gate, s0, gn):
    b, t, _ = q.shape
    bb = RET_BATCH_BLOCK if b % RET_BATCH_BLOCK == 0 else 1
    lc = math.gcd(t, RET_CHUNK) if t >= RET_CHUNK else t
    log_g = jnp.log1p(-jnp.exp2(-5.0 - jnp.arange(RET_HEADS, dtype=F32)))
    idx = jnp.arange(lc, dtype=F32)
    diff = idx[:, None] - idx[None, :]
    dintra = jnp.where(diff[None] >= 0, jnp.exp(diff[None] * log_g[:, None, None]), 0.0)
    lanes = (RET_HEADS, lc, RET_HEAD_DIM)
    dq = jnp.broadcast_to(jnp.exp((idx[None, :] + 1.0) * log_g[:, None])[:, :, None], lanes)
    dk = jnp.broadcast_to(jnp.exp((lc - 1.0 - idx)[None, :] * log_g[:, None])[:, :, None], lanes)
    ds = jnp.broadcast_to(jnp.exp(lc * log_g)[:, None, None], (RET_HEADS, 1, RET_HEAD_DIM))
    blk = lambda: pl.BlockSpec((bb, lc, RET_DIM), lambda bi, ci: (bi, ci, 0))
    st = lambda: pl.BlockSpec((bb, RET_HEADS, RET_HEAD_DIM, RET_HEAD_DIM), lambda bi, ci: (bi, 0, 0, 0))
    return pl.pallas_call(
        functools.partial(_ret_kernel, batch_block=bb),
        grid=(b // bb, t // lc),
        in_specs=[blk(), blk(), blk(), blk(), st(), _const_spec(dintra.shape), _const_spec(lanes),
                  _const_spec(lanes), _const_spec(ds.shape), _const_spec((1, RET_DIM))],
        out_specs=(blk(), st()),
        out_shape=(jax.ShapeDtypeStruct((b, t, RET_DIM), BF16),
                   jax.ShapeDtypeStruct((b, RET_HEADS, RET_HEAD_DIM, RET_HEAD_DIM), F32)),
        scratch_shapes=[pltpu.VMEM((bb, RET_HEADS, RET_HEAD_DIM, RET_HEAD_DIM), F32)],
        compiler_params=_params("arbitrary", "arbitrary"),
        name="retention",
    )(q, k, v, gate, s0, dintra, dq, dk, ds, gn)


def _rope_tables(start_pos, t):
    half = RET_HEAD_DIM // 2
    pos = (start_pos + jnp.arange(t)).astype(F32)
    inv = ROPE_BASE ** (-jnp.arange(half, dtype=F32) / half)
    ang = pos[:, None] * inv[None, :]
    cos, sin = jnp.cos(ang), jnp.sin(ang)
    return jnp.concatenate([cos, cos], axis=1), jnp.concatenate([-sin, sin], axis=1)


def _block_diag(w):
    g, n, _ = w.shape
    out = jnp.zeros((g * n, g * n), w.dtype)
    for i in range(g):
        out = out.at[i * n:(i + 1) * n, i * n:(i + 1) * n].set(w[i])
    return out


def _even_front(x, start_pos, hist0, transposed_kv, wts):
    b, t, d = x.shape
    e = dict(zip(("a", "q", "k", "v", "kb", "vb", "lf", "cum", "cumt", "aq", "ak"),
                 _in_even(x.reshape(b * t, d), wts["norm_mix"][0].reshape(1, -1), wts["w_in_even"], wts["b_forget"],
                          t, transposed_kv)))
    e["pool_out"], e["new_hist"] = _pool_mix(e["a"].reshape(b, t, POOL_DIM), hist0, wts["pool_w"], wts["pool_scale"],
                                             start_pos)
    return e


def _odd_front(h, b, t, start_pos, state0, wts):
    row1 = lambda a: a.reshape(1, -1)
    r3 = lambda z: z.reshape(b, t, z.shape[-1])
    cos_t, sin_t = _rope_tables(start_pos, t)
    c_out, vn, rq, rk, rv, rg = _in_odd(h, row1(wts["norm_mix"][1]), wts["w_in_odd"], row1(wts["gmlp_ln_g"]),
                                        row1(wts["gmlp_ln_b"]), cos_t, sin_t, wts["gmlp_ws"], wts["gmlp_bs"], t)
    r_out, s_new = _retention(r3(rq), r3(rk), r3(rv), r3(rg), state0, row1(wts["ret_gn_g"]))
    return c_out, r_out.reshape(b * t, RET_DIM), vn, s_new


def _mlp_args(layer, wts):
    w_out, split = (wts["w_out_even"], POOL_DIM) if layer == 0 else (wts["w_out_odd"], GMLP_DIM)
    return (w_out[:split], w_out[split:], wts["norm_ffn"][layer].reshape(1, -1), wts["w_up"][layer],
            wts["w_down"][layer], wts["norm_final"].reshape(1, -1))


def kernel(x_prompt, x_sample, state_pool, cache_k, cache_v, cache_logf, page_table, state_ret, norm_mix, w_in_even, b_forget, pool_w, pool_scale, w_out_even, w_in_odd, gmlp_ln_g, gmlp_ln_b, gmlp_ws, gmlp_bs, ret_gn_g, w_out_odd, norm_ffn, w_up, w_down, norm_final):
    assert norm_mix.shape[0] == 2 and w_in_even.shape[0] == 1 and w_in_odd.shape[0] == 1
    d = x_prompt.shape[-1]
    ev_cols = w_in_even.shape[-1]
    ev_pad = POOL_DIM + 3 * FOX_DIM + LANES - ev_cols
    wts = {
        "norm_mix": norm_mix, "norm_ffn": norm_ffn, "norm_final": norm_final,
        "w_in_even": jnp.pad(w_in_even[0], ((0, 0), (0, ev_pad))).astype(BF16),
        "b_forget": jnp.pad(b_forget[0], (0, LANES - FOX_HEADS)).reshape(1, LANES),
        "pool_w": _block_diag(pool_w[0]).astype(BF16),
        "pool_scale": pool_scale[0].reshape(1, POOL_DIM),
        "w_out_even": w_out_even[0].astype(BF16),
        "w_in_odd": w_in_odd[0].astype(BF16),
        "gmlp_ln_g": gmlp_ln_g[0], "gmlp_ln_b": gmlp_ln_b[0], "gmlp_ws": gmlp_ws[0], "gmlp_bs": gmlp_bs[0],
        "ret_gn_g": ret_gn_g[0],
        "w_out_odd": w_out_odd[0].astype(BF16),
        "w_up": w_up.astype(BF16), "w_down": w_down.astype(BF16),
    }
    b, s, _ = x_prompt.shape
    db, t, _ = x_sample.shape
    n_pages = page_table.shape[1]
    page = cache_k.shape[2]
    past = n_pages * page

    cache_kt = jnp.transpose(cache_k[0], (0, 2, 3, 1))
    cache_vt = jnp.transpose(cache_v[0], (0, 2, 3, 1))
    suffix = _page_suffix(jnp.transpose(cache_logf[0], (2, 0, 1)))

    se = _even_front(x_sample, past, state_pool[0], False, wts)
    pe = _even_front(x_prompt, 0, jnp.zeros((b, POOL_HIST, POOL_DIM), F32), True, wts)
    p3 = lambda z: z.reshape(b, s, z.shape[-1])
    att_p = _fox_prompt(p3(pe["q"]), p3(pe["kb"]), p3(pe["vb"]), p3(pe["aq"]), p3(pe["ak"]))

    assert db % 2 == 0
    half = db // 2
    s3 = lambda z: z.reshape(db, t, z.shape[-1])
    grow = jnp.transpose(s3(se["cum"]), (0, 2, 1))[:, :HEAD_ROWS, :]
    gq = jnp.broadcast_to(grow[:, :FOX_HEADS, :, None], (db, FOX_HEADS, t, LANES)).reshape(db, FOX_HEADS * t, LANES)
    sq, skb, svb = s3(se["q"]), s3(se["kb"]), s3(se["vb"])

    def attend_half(sel, h, x1, x2, layer):
        return _fox_sample_mlp(page_table[sel], sq[sel], skb[sel], svb[sel], gq[sel], grow[sel], cache_kt, cache_vt,
                               suffix, h, x1, x2, *_mlp_args(layer, wts), layer == 1)

    att_s0, h_p = attend_half(slice(0, half), x_prompt.reshape(b * s, d), pe["pool_out"].reshape(b * s, POOL_DIM),
                              att_p.reshape(b * s, FOX_DIM), 0)
    c_p, r_p, _, ret_p = _odd_front(h_p, b, s, 0, jnp.zeros((b, RET_HEADS, RET_HEAD_DIM, RET_HEAD_DIM), F32), wts)
    att_s1, y_p = attend_half(slice(half, db), h_p, c_p, r_p, 1)

    att_s = jnp.concatenate([att_s0, att_s1], axis=0)
    h_s = _mix_mlp(x_sample.reshape(db * t, d), se["pool_out"].reshape(db * t, POOL_DIM),
                   att_s.reshape(db * t, FOX_DIM), *_mlp_args(0, wts), False)
    c_s, r_s, gv_s, ret_s = _odd_front(h_s, db, t, past, state_ret[0], wts)
    y_s = _mix_mlp(h_s, c_s, r_s, *_mlp_args(1, wts), True)

    kv_p = lambda z: jnp.transpose(z.reshape(b, FOX_HEADS, FOX_HEAD_DIM, s), (0, 3, 1, 2))[None]
    kv_s = lambda z: z.reshape(1, db, t, FOX_HEADS, FOX_HEAD_DIM)
    lf_out = lambda e, n, m: e["lf"][:, :FOX_HEADS].reshape(1, n, m, FOX_HEADS)
    return (y_p.reshape(b, s, d), y_s.reshape(db, t, d), pe["new_hist"][None], kv_p(pe["k"]), kv_p(pe["v"]),
            lf_out(pe, b, s), ret_p[None], se["new_hist"][None], kv_s(se["k"]), kv_s(se["v"]), lf_out(se, db, t),
            gv_s.reshape(1, db, t, GMLP_DIM), ret_s[None])
```

```python
import functools
import math

import numpy as np
import jax
import jax.numpy as jnp
from jax import lax
from jax.experimental import pallas as pl
from jax.experimental.pallas import tpu as pltpu

F32 = jnp.float32
BF16 = jnp.bfloat16
EPS = 1e-6

POOL_WINDOWS = (2, 4, 8, 16)
POOL_GROUP_DIM = 64
POOL_DIM = 256
POOL_HIST = 15
FOX_HEAD_DIM = 64
FOX_HEADS = 12
FOX_DIM = FOX_HEADS * FOX_HEAD_DIM
GMLP_CHUNK = 128
GMLP_GROUPS = 4
GMLP_DIM = 512
RET_HEADS = 4
RET_DIM = 512
RET_HEAD_DIM = 128
RET_CHUNK = 128
ROPE_BASE = 10000.0

LANES = 128
SUBLANES = 8
HEAD_ROWS = 16
SPLIT_STRIDE = 16
NEG_BIG = -1e30
LOG2E = math.log2(math.e)
VMEM_LIMIT = 56 * 1024 * 1024
ROW_TILE = 512
ATT_TILE = 512
PAGES_PER_STEP = 16
FUSED_ROW_TILE = 256
RET_BATCH_BLOCK = 8
SUFFIX_PAGE_TILE = 256


def _row_tile(rows):
    return ROW_TILE if rows % ROW_TILE == 0 else rows


def _params(*sem):
    return pltpu.CompilerParams(dimension_semantics=sem, vmem_limit_bytes=VMEM_LIMIT)


def _const_spec(shape):
    nd = len(shape)
    return pl.BlockSpec(shape, lambda *_: (0,) * nd, pipeline_mode=pl.Buffered(1))


def _rms(x, g):
    return x * lax.rsqrt(jnp.mean(x * x, axis=-1, keepdims=True) + EPS) * g


def _bdot(a, b):
    return jnp.dot(a, b, preferred_element_type=F32)


def _bdot_nt(a, b):
    return lax.dot_general(a, b, (((1,), (1,)), ((), ())), preferred_element_type=F32)


def _bdot_tn(a, b):
    return lax.dot_general(a, b, (((0,), (0,)), ((), ())), preferred_element_type=F32)


def _split3(x):
    hi = x.astype(BF16)
    r = x - hi.astype(F32)
    mid = r.astype(BF16)
    lo = (r - mid.astype(F32)).astype(BF16)
    return hi, mid, lo


def _log_sigmoid(x):
    return jnp.minimum(x, 0.0) - jnp.log1p(jnp.exp(-jnp.abs(x)))


def _aug_tables():
    perm = np.zeros((LANES, 2 * FOX_DIM), np.float32)
    const = np.zeros((1, 2 * FOX_DIM), np.float32)
    for h in range(FOX_HEADS):
        base = (h // 2) * LANES + (FOX_HEAD_DIM if h % 2 == 0 else 0)
        for j in range(3):
            perm[SPLIT_STRIDE * j + h, base + j] = 1.0
            const[0, FOX_DIM + base + j] = 1.0
            const[0, base + 3 + j] = 1.0
            perm[SPLIT_STRIDE * j + h, FOX_DIM + base + 3 + j] = -1.0
    return jnp.asarray(perm, BF16), jnp.asarray(const)


def _in_even_kernel(x_ref, g_ref, w_ref, bf_ref, tri_ref, perm_ref, const_ref,
                    a_ref, q_ref, k_ref, v_ref, kb_ref, vb_ref, lf_ref, cum_ref, cumt_ref, aq_ref, ak_ref,
                    carry_ref, *, tiles_per_seq, transposed_kv):
    i = pl.program_id(0)
    xn = _rms(x_ref[...], g_ref[...]).astype(BF16)
    proj = _bdot(xn, w_ref[...])
    o_q, o_k, o_v, o_f = POOL_DIM, POOL_DIM + FOX_DIM, POOL_DIM + 2 * FOX_DIM, POOL_DIM + 3 * FOX_DIM
    a_ref[...] = proj[:, :o_q]
    q_ref[...] = (proj[:, o_q:o_k] * (FOX_HEAD_DIM ** -0.5 * LOG2E)).astype(BF16)
    k = proj[:, o_k:o_v]
    v = proj[:, o_v:o_f]
    if transposed_kv:
        k_ref[0] = k.T
        v_ref[0] = v.T
    else:
        k_ref[...] = k
        v_ref[...] = v
    kb_ref[...] = k.astype(BF16)
    vb_ref[...] = v.astype(BF16)
    lf = _log_sigmoid(proj[:, o_f:] + bf_ref[...])
    lf_ref[...] = lf
    tri3 = _bdot(tri_ref[...], jnp.concatenate(_split3(lf), axis=1))
    cum = tri3[:, :LANES] + tri3[:, LANES:2 * LANES] + tri3[:, 2 * LANES:]
    if tiles_per_seq > 1:
        @pl.when(i % tiles_per_seq == 0)
        def _():
            carry_ref[...] = jnp.zeros_like(carry_ref)
        cum = cum + carry_ref[...]
        carry_ref[...] = cum[-1:, :]
    cum = cum * LOG2E
    cum_ref[...] = cum
    cumt_ref[0] = cum.T[:HEAD_ROWS, :]
    is_head = lax.broadcasted_iota(jnp.int32, (1, LANES), 1) < FOX_HEADS
    packed = sum(pltpu.roll(jnp.where(is_head, p.astype(F32), 0.0), SPLIT_STRIDE * j, 1)
                 for j, p in enumerate(_split3(cum)))
    aug = (_bdot(packed.astype(BF16), perm_ref[...]) + const_ref[...]).astype(BF16)
    aq_ref[...] = aug[:, :FOX_DIM]
    ak_ref[...] = aug[:, FOX_DIM:]


def _in_even(x, g, w, bf, seq_len, transposed_kv):
    rows, d = x.shape
    tm = _row_tile(rows)
    n_tiles = rows // tm
    r = np.arange(tm)
    if seq_len >= tm:
        assert seq_len % tm == 0
        tiles_per_seq = seq_len // tm
        tri = r[None, :] <= r[:, None]
        cumt_shape = (rows // seq_len, HEAD_ROWS, seq_len)
        cumt_map = lambda i: (i // tiles_per_seq, 0, i % tiles_per_seq)
    else:
        assert tm % seq_len == 0 and not transposed_kv
        tiles_per_seq = 1
        tri = (r[None, :] <= r[:, None]) & ((r[None, :] // seq_len) == (r[:, None] // seq_len))
        cumt_shape = (n_tiles, HEAD_ROWS, tm)
        cumt_map = lambda i: (i, 0, 0)
    tri = jnp.asarray(tri, BF16)
    perm, const = _aug_tables()
    n = w.shape[1]
    row = lambda width: pl.BlockSpec((tm, width), lambda i: (i, 0))
    if transposed_kv:
        kv_shape = jax.ShapeDtypeStruct((rows // seq_len, FOX_DIM, seq_len), F32)
        kv_spec = lambda: pl.BlockSpec((1, FOX_DIM, tm), cumt_map)
    else:
        kv_shape = jax.ShapeDtypeStruct((rows, FOX_DIM), F32)
        kv_spec = lambda: row(FOX_DIM)
    out_shape = (
        jax.ShapeDtypeStruct((rows, POOL_DIM), F32),
        jax.ShapeDtypeStruct((rows, FOX_DIM), BF16),
        kv_shape,
        kv_shape,
        jax.ShapeDtypeStruct((rows, FOX_DIM), BF16),
        jax.ShapeDtypeStruct((rows, FOX_DIM), BF16),
        jax.ShapeDtypeStruct((rows, LANES), F32),
        jax.ShapeDtypeStruct((rows, LANES), F32),
        jax.ShapeDtypeStruct(cumt_shape, F32),
        jax.ShapeDtypeStruct((rows, FOX_DIM), BF16),
        jax.ShapeDtypeStruct((rows, FOX_DIM), BF16),
    )
    out_specs = (row(POOL_DIM), row(FOX_DIM), kv_spec(), kv_spec(), row(FOX_DIM), row(FOX_DIM),
                 row(LANES), row(LANES), pl.BlockSpec((1, HEAD_ROWS, tm), cumt_map), row(FOX_DIM), row(FOX_DIM))
    return pl.pallas_call(
        functools.partial(_in_even_kernel, tiles_per_seq=tiles_per_seq, transposed_kv=transposed_kv),
        grid=(n_tiles,),
        in_specs=[row(d), _const_spec((1, d)), _const_spec((d, n)), _const_spec((1, LANES)),
                  _const_spec((tm, tm)), _const_spec(perm.shape), _const_spec(const.shape)],
        out_specs=out_specs,
        out_shape=out_shape,
        scratch_shapes=[pltpu.VMEM((1, LANES), F32)],
        compiler_params=_params("arbitrary"),
        name="in_even",
    )(x, g, w, bf, tri, perm, const)


def _pool_kernel(a_ref, hist_ref, w_ref, scale_ref, out_ref, newhist_ref, ext_ref, *, seq_len, start_pos):
    t = seq_len
    base = POOL_HIST + 1
    x0 = a_ref[0]
    ext_ref[0:1, :] = jnp.zeros((1, POOL_DIM), F32)
    ext_ref[1:base, :] = hist_ref[0]
    ext_ref[base:base + t, :] = x0

    def back(kk):
        return ext_ref[base - kk:base - kk + t, :]

    sums = []
    run = x0
    nxt = 1
    for w in POOL_WINDOWS:
        while nxt < w:
            run = run + back(nxt)
            nxt += 1
        sums.append(run)
    grp = lax.broadcasted_iota(jnp.int32, (1, POOL_DIM), 1) // POOL_GROUP_DIM
    sel = sums[-1]
    win = jnp.full((1, POOL_DIM), POOL_WINDOWS[-1], jnp.int32)
    for gi in range(len(POOL_WINDOWS) - 2, -1, -1):
        sel = jnp.where(grp == gi, sums[gi], sel)
        win = jnp.where(grp == gi, POOL_WINDOWS[gi], win)
    pos = start_pos + lax.broadcasted_iota(jnp.int32, (t, 1), 0)
    cnt = jnp.minimum(pos + 1, win).astype(F32)
    dd = sel / cnt - x0
    y = _bdot(dd.astype(BF16), w_ref[...]) * scale_ref[...]
    out_ref[0] = y.astype(out_ref.dtype)
    newhist_ref[0] = ext_ref[t + 1:t + base, :]


def _pool_mix(a, hist, w_bd, scale, start_pos):
    b, t, _ = a.shape
    return pl.pallas_call(
        functools.partial(_pool_kernel, seq_len=t, start_pos=start_pos),
        grid=(b,),
        in_specs=[pl.BlockSpec((1, t, POOL_DIM), lambda i: (i, 0, 0)),
                  pl.BlockSpec((1, POOL_HIST, POOL_DIM), lambda i: (i, 0, 0)),
                  _const_spec((POOL_DIM, POOL_DIM)), _const_spec((1, POOL_DIM))],
        out_specs=(pl.BlockSpec((1, t, POOL_DIM), lambda i: (i, 0, 0)),
                   pl.BlockSpec((1, POOL_HIST, POOL_DIM), lambda i: (i, 0, 0))),
        out_shape=(jax.ShapeDtypeStruct((b, t, POOL_DIM), BF16),
                   jax.ShapeDtypeStruct((b, POOL_HIST, POOL_DIM), F32)),
        scratch_shapes=[pltpu.VMEM((POOL_HIST + 1 + t, POOL_DIM), F32)],
        compiler_params=_params("arbitrary"),
        name="pool_mix",
    )(a, hist, w_bd, scale)


def _fox_prompt_kernel(q_ref, k_ref, v_ref, aq_ref, ak_ref, o_ref, *, tile):
    s_len = q_ref.shape[1]
    lane = lax.broadcasted_iota(jnp.int32, (1, LANES), 1)
    in_head = (lane < FOX_HEAD_DIM, lane >= FOX_HEAD_DIM)
    visible = (lax.broadcasted_iota(jnp.int32, (tile, tile), 1) <= lax.broadcasted_iota(jnp.int32, (tile, tile), 0))
    km = []
    for ki in range(s_len // tile):
        rows = slice(ki * tile, (ki + 1) * tile)
        km.append([jnp.where(in_head[e], k_ref[0, rows, :], ak_ref[0, rows, :]) for e in range(2)])
    for qi in range(s_len // tile):
        qrows = slice(qi * tile, (qi + 1) * tile)
        outs = []
        for e in range(2):
            qm = jnp.where(in_head[e], q_ref[0, qrows, :], aq_ref[0, qrows, :])
            m = jnp.full((tile, 1), NEG_BIG, F32)
            l = jnp.zeros((tile, 1), F32)
            acc = jnp.zeros((tile, LANES), F32)
            for ki in range(qi + 1):
                s = _bdot_nt(qm, km[ki][e])
                if ki == qi:
                    s = jnp.where(visible, s, NEG_BIG)
                m_new = jnp.maximum(m, jnp.max(s, axis=-1, keepdims=True))
                alpha = jnp.exp2(m - m_new)
                p = jnp.exp2(s - m_new)
                l = alpha * l + jnp.sum(p, axis=-1, keepdims=True)
                acc = alpha * acc + _bdot(p.astype(BF16), v_ref[0, ki * tile:(ki + 1) * tile, :])
                m = m_new
            outs.append(acc / l)
        o_ref[0, qrows, :] = jnp.where(in_head[0], outs[0], outs[1]).astype(o_ref.dtype)


def _fox_prompt(q, kb, vb, aq, ak):
    b, s, _ = q.shape
    tile = ATT_TILE if s % ATT_TILE == 0 else s
    assert s // tile <= 4, "the causal tile structure is unrolled in the kernel"
    full = lambda: pl.BlockSpec((1, s, LANES), lambda bi, hp: (bi, 0, hp))
    return pl.pallas_call(
        functools.partial(_fox_prompt_kernel, tile=tile),
        grid=(b, FOX_HEADS // 2),
        in_specs=[full(), full(), full(), full(), full()],
        out_specs=full(),
        out_shape=jax.ShapeDtypeStruct((b, s, FOX_DIM), BF16),
        compiler_params=_params("arbitrary", "arbitrary"),
        name="fox_prompt",
    )(q, kb, vb, aq, ak)


def _suffix_kernel(lf_ref, mat_ref, out_ref):
    for h in range(FOX_HEADS):
        out_ref[:, h, :] = sum(_bdot(p, mat_ref[...]) for p in _split3(lf_ref[h])) * LOG2E


def _page_suffix(lf_hpp):
    heads, n_phys, page = lf_hpp.shape
    tp = SUFFIX_PAGE_TILE if n_phys % SUFFIX_PAGE_TILE == 0 else n_phys
    r = np.arange(page)
    later = r[:, None] > r[None, :]
    mat = jnp.asarray(np.concatenate([later, np.ones((page, page), bool)], axis=1), BF16)
    return pl.pallas_call(
        _suffix_kernel,
        grid=(n_phys // tp,),
        in_specs=[pl.BlockSpec((heads, tp, page), lambda i: (0, i, 0)), _const_spec((page, 2 * page))],
        out_specs=pl.BlockSpec((tp, heads, 2 * page), lambda i: (i, 0, 0)),
        out_shape=jax.ShapeDtypeStruct((n_phys, heads, 2 * page), F32),
        compiler_params=_params("arbitrary"),
        name="page_suffix",
    )(lf_hpp, mat)


def _page_copies(pt_ref, hbm, bufs, sems, slot, row, first_page, n_pg):
    copies = []
    for j in range(n_pg):
        pg = 0 if pt_ref is None else pt_ref[row, first_page + j]
        for kind in range(3):
            copies.append(pltpu.make_async_copy(hbm[kind].at[pg], bufs[kind].at[slot, j], sems.at[slot, kind]))
    return copies


def _fox_sample_mlp_kernel(pt_ref, q_ref, kn_ref, vn_ref, gq_ref, grow_ref, kt_hbm, vt_hbm, sfx_hbm,
                           h_ref, x1_ref, x2_ref, wo1_ref, wo2_ref, gf_ref, wu_ref, wd_ref, gl_ref,
                           o_ref, y_ref, m_sc, l_sc, acc_sc, run_sc, xn_sc, kbuf, vbuf, sbuf, sems, *,
                           n_pg, n_pages, t_new, steps_per_tile, ff_chunk, final_norm):
    n_seq, steps = pl.num_programs(0), pl.num_programs(1)
    bi, i = pl.program_id(0), pl.program_id(1)
    last = steps - 1
    page = LANES
    g = bi * steps + i
    slot = g % 2
    chunk = g % steps_per_tile
    hbm, bufs = (kt_hbm, vt_hbm, sfx_hbm), (kbuf, vbuf, sbuf)
    first_page = lambda step: n_pages - n_pg * (step + 1)

    @pl.when(g == 0)
    def _():
        for cp in _page_copies(pt_ref, hbm, bufs, sems, 0, 0, first_page(0), n_pg):
            cp.start()

    @pl.when(i == 0)
    def _():
        m_sc[...] = jnp.full_like(m_sc, NEG_BIG)
        l_sc[...] = jnp.zeros_like(l_sc)
        acc_sc[...] = jnp.zeros_like(acc_sc)
        run_sc[...] = jnp.zeros_like(run_sc)

    @pl.when(chunk == 0)
    def _():
        h1 = h_ref[...] + _bdot(x1_ref[...], wo1_ref[...]) + _bdot(x2_ref[...], wo2_ref[...])
        y_ref[...] = h1
        xn_sc[...] = _rms(h1, gf_ref[...]).astype(BF16)

    for cp in _page_copies(None, hbm, bufs, sems, slot, 0, 0, n_pg):
        cp.wait()
    wrap_i = i == last
    next_i = jnp.where(wrap_i, 0, i + 1)
    next_bi = jnp.where(wrap_i, jnp.where(bi == n_seq - 1, 0, bi + 1), bi)
    for cp in _page_copies(pt_ref, hbm, bufs, sems, 1 - slot, next_bi, first_page(next_i), n_pg):
        cp.start()

    run = run_sc[...]
    bias = [None] * n_pg
    for j in range(n_pg - 1, -1, -1):
        sf = sbuf[slot, j]
        bias[j] = sf[:, :page] + run
        run = run + sf[:, page:]
    run_sc[...] = run

    t = t_new
    q = q_ref[0]
    gq = gq_ref[0]
    s_parts = []
    for h in range(FOX_HEADS):
        hs = slice(h * FOX_HEAD_DIM, (h + 1) * FOX_HEAD_DIM)
        kt = jnp.concatenate([kbuf[slot, j, h] for j in range(n_pg)], axis=1).astype(BF16)
        s_parts.append(_bdot(q[:, hs], kt) + jnp.concatenate([bias[j][h:h + 1, :] for j in range(n_pg)], axis=1))
    s = jnp.concatenate(s_parts, axis=0) + jnp.concatenate([gq] * n_pg, axis=1)
    ff0 = pl.multiple_of(chunk * ff_chunk, ff_chunk)
    u = jnp.maximum(_bdot(xn_sc[...], wu_ref[:, pl.ds(ff0, ff_chunk)]), 0.0)
    u = (u * u).astype(BF16)
    m_prev = m_sc[...]
    m_new = jnp.maximum(m_prev, jnp.max(s, axis=-1, keepdims=True))
    alpha = jnp.exp2(m_prev - m_new)
    p32 = jnp.exp2(s - jnp.concatenate([m_new] * n_pg, axis=1))
    l_sc[...] = alpha * l_sc[...] + jnp.sum(p32, axis=-1, keepdims=True)
    m_sc[...] = m_new
    pv = []
    for h in range(FOX_HEADS):
        vt = jnp.concatenate([vbuf[slot, j, h] for j in range(n_pg)], axis=1).astype(BF16)
        pv.append(_bdot_nt(p32[h * t:(h + 1) * t].astype(BF16), vt))
    acc_sc[...] = alpha[:, :FOX_HEAD_DIM] * acc_sc[...] + jnp.concatenate(pv, axis=0)

    y_ref[...] += _bdot(u, wd_ref[pl.ds(ff0, ff_chunk), :])
    if final_norm:
        @pl.when(chunk == steps_per_tile - 1)
        def _():
            y_ref[...] = _rms(y_ref[...], gl_ref[...])

    @pl.when(i == last)
    def _():
        kn = kn_ref[0]
        vn = vn_ref[0]
        grow = grow_ref[0]
        r = lax.broadcasted_iota(jnp.int32, (t, t), 0)
        c = lax.broadcasted_iota(jnp.int32, (t, t), 1)
        for h in range(FOX_HEADS):
            hs = slice(h * FOX_HEAD_DIM, (h + 1) * FOX_HEAD_DIM)
            rows = slice(h * t, (h + 1) * t)
            s = _bdot_nt(q[:, hs], kn[:, hs]) + (gq[rows, :1] - grow[h:h + 1, :])
            s = jnp.where(c <= r, s, NEG_BIG)
            m_prev = m_sc[rows, :1]
            m_new = jnp.maximum(m_prev, jnp.max(s, axis=-1, keepdims=True))
            alpha = jnp.exp2(m_prev - m_new)
            p32 = jnp.exp2(s - m_new)
            l = alpha * l_sc[rows, :1] + jnp.sum(p32, axis=-1, keepdims=True)
            acc = alpha * acc_sc[rows, :] + _bdot(p32.astype(BF16), vn[:, hs])
            o_ref[0, :, hs] = (acc / l).astype(o_ref.dtype)

    @pl.when(g == n_seq * steps - 1)
    def _():
        for cp in _page_copies(None, hbm, bufs, sems, 1 - slot, 0, 0, n_pg):
            cp.wait()


def _fox_sample_mlp(page_table, q, kn, vn, gq, grow, cache_kt, cache_vt, suffix,
                    h, x1, x2, wo1, wo2, gf, wu, wd, gl, final_norm):
    b, t, _ = q.shape
    n_pages = page_table.shape[1]
    n_pg = PAGES_PER_STEP if n_pages % PAGES_PER_STEP == 0 else 1
    steps = n_pages // n_pg
    page = cache_kt.shape[-1]
    assert page == LANES
    rows, d = h.shape
    d_ff = wu.shape[1]
    tm = FUSED_ROW_TILE if rows % FUSED_ROW_TILE == 0 else rows
    tiles = rows // tm
    assert (b * steps) % tiles == 0, "grid steps must split evenly over the MLP row tiles"
    spt = (b * steps) // tiles
    assert d_ff % spt == 0 and (d_ff // spt) % LANES == 0
    ck = d_ff // spt
    row = lambda width: pl.BlockSpec((tm, width), lambda bi, i, pt: ((bi * steps + i) // spt, 0))

    tok = lambda width: pl.BlockSpec((1, t, width), lambda bi, i, pt: (bi, 0, 0))
    in_hbm = lambda: pl.BlockSpec(memory_space=pl.ANY)
    grid_spec = pltpu.PrefetchScalarGridSpec(
        num_scalar_prefetch=1,
        grid=(b, steps),
        in_specs=[tok(FOX_DIM), tok(FOX_DIM), tok(FOX_DIM),
                  pl.BlockSpec((1, FOX_HEADS * t, LANES), lambda bi, i, pt: (bi, 0, 0)),
                  pl.BlockSpec((1, HEAD_ROWS, t), lambda bi, i, pt: (bi, 0, 0)),
                  in_hbm(), in_hbm(), in_hbm(),
                  row(d), row(x1.shape[1]), row(x2.shape[1]), _const_spec(wo1.shape), _const_spec(wo2.shape),
                  _const_spec((1, d)), _const_spec(wu.shape), _const_spec(wd.shape), _const_spec((1, d))],
        out_specs=(tok(FOX_DIM), row(d)),
        scratch_shapes=[pltpu.VMEM((FOX_HEADS * t, LANES), F32), pltpu.VMEM((FOX_HEADS * t, LANES), F32),
                        pltpu.VMEM((FOX_HEADS * t, FOX_HEAD_DIM), F32), pltpu.VMEM((FOX_HEADS, LANES), F32),
                        pltpu.VMEM((tm, d), BF16),
                        pltpu.VMEM((2, n_pg, FOX_HEADS, FOX_HEAD_DIM, page), F32),
                        pltpu.VMEM((2, n_pg, FOX_HEADS, FOX_HEAD_DIM, page), F32),
                        pltpu.VMEM((2, n_pg, FOX_HEADS, 2 * page), F32),
                        pltpu.SemaphoreType.DMA((2, 3))],
    )
    return pl.pallas_call(
        functools.partial(_fox_sample_mlp_kernel, n_pg=n_pg, n_pages=n_pages, t_new=t, steps_per_tile=spt,
                          ff_chunk=ck, final_norm=final_norm),
        grid_spec=grid_spec,
        out_shape=(jax.ShapeDtypeStruct((b, t, FOX_DIM), BF16), jax.ShapeDtypeStruct((rows, d), F32)),
        compiler_params=_params("arbitrary", "arbitrary"),
        name="fox_sample_mlp",
    )(page_table, q, kn, vn, gq, grow, cache_kt, cache_vt, suffix, h, x1, x2, wo1, wo2, gf, wu, wd, gl)


def _mix_mlp_kernel(h_ref, x1_ref, x2_ref, wo1_ref, wo2_ref, gf_ref, wu_ref, wd_ref, gl_ref, o_ref, *,
                    ff_chunk, final_norm):
    h1 = h_ref[...] + _bdot(x1_ref[...], wo1_ref[...]) + _bdot(x2_ref[...], wo2_ref[...])
    xn = _rms(h1, gf_ref[...]).astype(BF16)
    o_ref[...] = h1
    d_ff = wu_ref.shape[1]
    for c in range(d_ff // ff_chunk):
        u = jnp.maximum(_bdot(xn, wu_ref[:, c * ff_chunk:(c + 1) * ff_chunk]), 0.0)
        o_ref[...] += _bdot((u * u).astype(BF16), wd_ref[c * ff_chunk:(c + 1) * ff_chunk, :])
    if final_norm:
        o_ref[...] = _rms(o_ref[...], gl_ref[...])


def _mix_mlp(h, x1, x2, wo1, wo2, gf, wu, wd, gl, final_norm):
    rows, d = h.shape
    tm = _row_tile(rows)
    d_ff = wu.shape[1]
    row = lambda width: pl.BlockSpec((tm, width), lambda i: (i, 0))
    return pl.pallas_call(
        functools.partial(_mix_mlp_kernel, ff_chunk=min(d_ff, 1024), final_norm=final_norm),
        grid=(rows // tm,),
        in_specs=[row(d), row(x1.shape[1]), row(x2.shape[1]), _const_spec(wo1.shape), _const_spec(wo2.shape),
                  _const_spec((1, d)), _const_spec(wu.shape), _const_spec(wd.shape), _const_spec((1, d))],
        out_specs=row(d),
        out_shape=jax.ShapeDtypeStruct((rows, d), F32),
        compiler_params=_params("arbitrary"),
        name="mix_mlp",
    )(h, x1, x2, wo1, wo2, gf, wu, wd, gl)


def _in_odd_kernel(x_ref, g_ref, w_ref, lng_ref, lnb_ref, cos_ref, sin_ref, ws_ref, bs_ref,
                   c_ref, vn_ref, q_ref, k_ref, rv_ref, rg_ref, *, chunk):
    xn = _rms(x_ref[...], g_ref[...]).astype(BF16)
    proj = _bdot(xn, w_ref[...])
    tm = proj.shape[0]
    gd, rd = GMLP_DIM, RET_DIM
    gv = proj[:, gd:2 * gd]
    mu = jnp.mean(gv, axis=-1, keepdims=True)
    var = jnp.mean((gv - mu) ** 2, axis=-1, keepdims=True)
    vn = (gv - mu) * lax.rsqrt(var + EPS) * lng_ref[...] + lnb_ref[...]
    vn_ref[...] = vn
    vnb = vn.astype(BF16)
    wrows = ws_ref.shape[1]
    r = lax.broadcasted_iota(jnp.int32, (wrows, wrows), 0)
    c = lax.broadcasted_iota(jnp.int32, (wrows, wrows), 1)
    causal_in_chunk = (c <= r) & (c >= (r // chunk) * chunk)
    for gi in range(GMLP_GROUPS):
        sl = slice(gi * LANES, (gi + 1) * LANES)
        ws = jnp.where(causal_in_chunk, ws_ref[gi], 0.0).astype(BF16)
        for ti in range(tm // wrows):
            rows = slice(ti * wrows, (ti + 1) * wrows)
            z = _bdot(ws, vnb[rows, sl]) + bs_ref[gi]
            c_ref[rows, sl] = (proj[rows, sl] * z).astype(c_ref.dtype)
    cos = cos_ref[...]
    sin = sin_ref[...]
    o_q, o_k, o_v, o_g = 2 * gd, 2 * gd + rd, 2 * gd + 2 * rd, 2 * gd + 3 * rd
    for hh in range(RET_HEADS):
        sl = slice(hh * RET_HEAD_DIM, (hh + 1) * RET_HEAD_DIM)
        qh = proj[:, o_q + hh * RET_HEAD_DIM:o_q + (hh + 1) * RET_HEAD_DIM]
        kh = proj[:, o_k + hh * RET_HEAD_DIM:o_k + (hh + 1) * RET_HEAD_DIM]
        q_ref[:, sl] = (qh * cos + pltpu.roll(qh, RET_HEAD_DIM // 2, 1) * sin).astype(BF16)
        k_ref[:, sl] = (kh * cos + pltpu.roll(kh, RET_HEAD_DIM // 2, 1) * sin) * (RET_HEAD_DIM ** -0.5)
    rv_ref[...] = proj[:, o_v:o_g].astype(BF16)
    rg_ref[...] = proj[:, o_g:]


def _in_odd(x, g, w, lng, lnb, cos_t, sin_t, gmlp_ws, gmlp_bs, seq_len):
    rows, d = x.shape
    tm = _row_tile(rows)
    chunk = min(seq_len, GMLP_CHUNK)
    assert seq_len % chunk == 0 and tm % chunk == 0
    wrows = chunk if chunk % LANES == 0 else tm
    reps = wrows // chunk
    ws_t = jnp.tile(gmlp_ws[:, :chunk, :chunk], (1, reps, reps))
    bs_t = jnp.broadcast_to(jnp.tile(gmlp_bs[:, :chunk], (1, reps))[:, :, None], (GMLP_GROUPS, wrows, LANES))
    if seq_len >= tm:
        tiles_per_seq = seq_len // tm
        rope_map = lambda i: (i % tiles_per_seq, 0)
    else:
        reps = tm // seq_len
        cos_t = jnp.tile(cos_t, (reps, 1))
        sin_t = jnp.tile(sin_t, (reps, 1))
        rope_map = lambda i: (0, 0)
    row = lambda width: pl.BlockSpec((tm, width), lambda i: (i, 0))
    outs = ((GMLP_DIM, BF16), (GMLP_DIM, F32), (RET_DIM, BF16), (RET_DIM, F32), (RET_DIM, BF16), (RET_DIM, F32))
    return pl.pallas_call(
        functools.partial(_in_odd_kernel, chunk=chunk),
        grid=(rows // tm,),
        in_specs=[row(d), _const_spec((1, d)), _const_spec(w.shape), _const_spec((1, GMLP_DIM)),
                  _const_spec((1, GMLP_DIM)), pl.BlockSpec((tm, RET_HEAD_DIM), rope_map),
                  pl.BlockSpec((tm, RET_HEAD_DIM), rope_map), _const_spec(ws_t.shape), _const_spec(bs_t.shape)],
        out_specs=tuple(row(wd) for wd, _ in outs),
        out_shape=tuple(jax.ShapeDtypeStruct((rows, wd), dt) for wd, dt in outs),
        compiler_params=_params("arbitrary"),
        name="in_odd",
    )(x, g, w, lng, lnb, cos_t, sin_t, ws_t, bs_t)


def _ret_kernel(q_ref, k_ref, v_ref, g_ref, s0_ref, dintra_ref, dq_ref, dk_ref, ds_ref, gn_ref,
                o_ref, sout_ref, st_sc, *, batch_block):
    c = pl.program_id(1)

    @pl.when(c == 0)
    def _():
        st_sc[...] = s0_ref[...]

    for bi in range(batch_block):
        for hh in range(RET_HEADS):
            sl = slice(hh * RET_HEAD_DIM, (hh + 1) * RET_HEAD_DIM)
            q = q_ref[bi, :, sl]
            kf = k_ref[bi, :, sl]
            v = v_ref[bi, :, sl]
            st = st_sc[bi, hh]
            sc = _bdot_nt(q, kf.astype(BF16)) * dintra_ref[hh]
            o = _bdot(sc.astype(BF16), v) + _bdot(q, st.astype(BF16)) * dq_ref[hh]
            st_sc[bi, hh] = st * ds_ref[hh] + _bdot_tn((kf * dk_ref[hh]).astype(BF16), v)
            mu = jnp.mean(o, axis=-1, keepdims=True)
            var = jnp.mean((o - mu) ** 2, axis=-1, keepdims=True)
            on = (o - mu) * lax.rsqrt(var + EPS) * gn_ref[:, sl]
            gate = g_ref[bi, :, sl]
            o_ref[bi, :, sl] = (on * (gate * jax.nn.sigmoid(gate))).astype(o_ref.dtype)

    @pl.when(c == pl.num_programs(1) - 1)
    def _():
        sout_ref[...] = st_sc[...]


def _retention(q, k, v, gate, s0, gn):
    b, t, _ = q.shape
    bb = RET_BATCH_BLOCK if b % RET_BATCH_BLOCK == 0 else 1
    lc = math.gcd(t, RET_CHUNK) if t >= RET_CHUNK else t
    log_g = jnp.log1p(-jnp.exp2(-5.0 - jnp.arange(RET_HEADS, dtype=F32)))
    idx = jnp.arange(lc, dtype=F32)
    diff = idx[:, None] - idx[None, :]
    dintra = jnp.where(diff[None] >= 0, jnp.exp(diff[None] * log_g[:, None, None]), 0.0)
    lanes = (RET_HEADS, lc, RET_HEAD_DIM)
    dq = jnp.broadcast_to(jnp.exp((idx[None, :] + 1.0) * log_g[:, None])[:, :, None], lanes)
    dk = jnp.broadcast_to(jnp.exp((lc - 1.0 - idx)[None, :] * log_g[:, None])[:, :, None], lanes)
    ds = jnp.broadcast_to(jnp.exp(lc * log_g)[:, None, None], (RET_HEADS, 1, RET_HEAD_DIM))
    blk = lambda: pl.BlockSpec((bb, lc, RET_DIM), lambda bi, ci: (bi, ci, 0))
    st = lambda: pl.BlockSpec((bb, RET_HEADS, RET_HEAD_DIM, RET_HEAD_DIM), lambda bi, ci: (bi, 0, 0, 0))
    return pl.pallas_call(
        functools.partial(_ret_kernel, batch_block=bb),
        grid=(b // bb, t // lc),
        in_specs=[blk(), blk(), blk(), blk(), st(), _const_spec(dintra.shape), _const_spec(lanes),
                  _const_spec(lanes), _const_spec(ds.shape), _const_spec((1, RET_DIM))],
        out_specs=(blk(), st()),
        out_shape=(jax.ShapeDtypeStruct((b, t, RET_DIM), BF16),
                   jax.ShapeDtypeStruct((b, RET_HEADS, RET_HEAD_DIM, RET_HEAD_DIM), F32)),
        scratch_shapes=[pltpu.VMEM((bb, RET_HEADS, RET_HEAD_DIM, RET_HEAD_DIM), F32)],
        compiler_params=_params("arbitrary", "arbitrary"),
        name="retention",
    )(q, k, v, gate, s0, dintra, dq, dk, ds, gn)


def _rope_tables(start_pos, t):
    half = RET_HEAD_DIM // 2
    pos = (start_pos + jnp.arange(t)).astype(F32)
    inv = ROPE_BASE ** (-jnp.arange(half, dtype=F32) / half)
    ang = pos[:, None] * inv[None, :]
    cos, sin = jnp.cos(ang), jnp.sin(ang)
    return jnp.concatenate([cos, cos], axis=1), jnp.concatenate([-sin, sin], axis=1)


def _block_diag(w):
    g, n, _ = w.shape
    out = jnp.zeros((g * n, g * n), w.dtype)
    for i in range(g):
        out = out.at[i * n:(i + 1) * n, i * n:(i + 1) * n].set(w[i])
    return out


def _even_front(x, start_pos, hist0, transposed_kv, wts):
    b, t, d = x.shape
    e = dict(zip(("a", "q", "k", "v", "kb", "vb", "lf", "cum", "cumt", "aq", "ak"),
                 _in_even(x.reshape(b * t, d), wts["norm_mix"][0].reshape(1, -1), wts["w_in_even"], wts["b_forget"],
                          t, transposed_kv)))
    e["pool_out"], e["new_hist"] = _pool_mix(e["a"].reshape(b, t, POOL_DIM), hist0, wts["pool_w"], wts["pool_scale"],
                                             start_pos)
    return e


def _odd_front(h, b, t, start_pos, state0, wts):
    row1 = lambda a: a.reshape(1, -1)
    r3 = lambda z: z.reshape(b, t, z.shape[-1])
    cos_t, sin_t = _rope_tables(start_pos, t)
    c_out, vn, rq, rk, rv, rg = _in_odd(h, row1(wts["norm_mix"][1]), wts["w_in_odd"], row1(wts["gmlp_ln_g"]),
                                        row1(wts["gmlp_ln_b"]), cos_t, sin_t, wts["gmlp_ws"], wts["gmlp_bs"], t)
    r_out, s_new = _retention(r3(rq), r3(rk), r3(rv), r3(rg), state0, row1(wts["ret_gn_g"]))
    return c_out, r_out.reshape(b * t, RET_DIM), vn, s_new


def _mlp_args(layer, wts):
    w_out, split = (wts["w_out_even"], POOL_DIM) if layer == 0 else (wts["w_out_odd"], GMLP_DIM)
    return (w_out[:split], w_out[split:], wts["norm_ffn"][layer].reshape(1, -1), wts["w_up"][layer],
            wts["w_down"][layer], wts["norm_final"].reshape(1, -1))


def kernel(x_prompt, x_sample, state_pool, cache_k, cache_v, cache_logf, page_table, state_ret, norm_mix, w_in_even, b_forget, pool_w, pool_scale, w_out_even, w_in_odd, gmlp_ln_g, gmlp_ln_b, gmlp_ws, gmlp_bs, ret_gn_g, w_out_odd, norm_ffn, w_up, w_down, norm_final):
    assert norm_mix.shape[0] == 2 and w_in_even.shape[0] == 1 and w_in_odd.shape[0] == 1
    d = x_prompt.shape[-1]
    ev_cols = w_in_even.shape[-1]
    ev_pad = POOL_DIM + 3 * FOX_DIM + LANES - ev_cols
    wts = {
        "norm_mix": norm_mix, "norm_ffn": norm_ffn, "norm_final": norm_final,
        "w_in_even": jnp.pad(w_in_even[0], ((0, 0), (0, ev_pad))).astype(BF16),
        "b_forget": jnp.pad(b_forget[0], (0, LANES - FOX_HEADS)).reshape(1, LANES),
        "pool_w": _block_diag(pool_w[0]).astype(BF16),
        "pool_scale": pool_scale[0].reshape(1, POOL_DIM),
        "w_out_even": w_out_even[0].astype(BF16),
        "w_in_odd": w_in_odd[0].astype(BF16),
        "gmlp_ln_g": gmlp_ln_g[0], "gmlp_ln_b": gmlp_ln_b[0], "gmlp_ws": gmlp_ws[0], "gmlp_bs": gmlp_bs[0],
        "ret_gn_g": ret_gn_g[0],
        "w_out_odd": w_out_odd[0].astype(BF16),
        "w_up": w_up.astype(BF16), "w_down": w_down.astype(BF16),
    }
    b, s, _ = x_prompt.shape
    db, t, _ = x_sample.shape
    n_pages = page_table.shape[1]
    page = cache_k.shape[2]
    past = n_pages * page

    cache_kt = jnp.transpose(cache_k[0], (0, 2, 3, 1))
    cache_vt = jnp.transpose(cache_v[0], (0, 2, 3, 1))
    suffix = _page_suffix(jnp.transpose(cache_logf[0], (2, 0, 1)))

    se = _even_front(x_sample, past, state_pool[0], False, wts)
    pe = _even_front(x_prompt, 0, jnp.zeros((b, POOL_HIST, POOL_DIM), F32), True, wts)
    p3 = lambda z: z.reshape(b, s, z.shape[-1])
    att_p = _fox_prompt(p3(pe["q"]), p3(pe["kb"]), p3(pe["vb"]), p3(pe["aq"]), p3(pe["ak"]))

    assert db % 2 == 0
    half = db // 2
    s3 = lambda z: z.reshape(db, t, z.shape[-1])
    grow = jnp.transpose(s3(se["cum"]), (0, 2, 1))[:, :HEAD_ROWS, :]
    gq = jnp.broadcast_to(grow[:, :FOX_HEADS, :, None], (db, FOX_HEADS, t, LANES)).reshape(db, FOX_HEADS * t, LANES)
    sq, skb, svb = s3(se["q"]), s3(se["kb"]), s3(se["vb"])

    def attend_half(sel, h, x1, x2, layer):
        return _fox_sample_mlp(page_table[sel], sq[sel], skb[sel], svb[sel], gq[sel], grow[sel], cache_kt, cache_vt,
                               suffix, h, x1, x2, *_mlp_args(layer, wts), layer == 1)

    att_s0, h_p = attend_half(slice(0, half), x_prompt.reshape(b * s, d), pe["pool_out"].reshape(b * s, POOL_DIM),
                              att_p.reshape(b * s, FOX_DIM), 0)
    c_p, r_p, _, ret_p = _odd_front(h_p, b, s, 0, jnp.zeros((b, RET_HEADS, RET_HEAD_DIM, RET_HEAD_DIM), F32), wts)
    att_s1, y_p = attend_half(slice(half, db), h_p, c_p, r_p, 1)

    att_s = jnp.concatenate([att_s0, att_s1], axis=0)
    h_s = _mix_mlp(x_sample.reshape(db * t, d), se["pool_out"].reshape(db * t, POOL_DIM),
                   att_s.reshape(db * t, FOX_DIM), *_mlp_args(0, wts), False)
    c_s, r_s, gv_s, ret_s = _odd_front(h_s, db, t, past, state_ret[0], wts)
    y_s = _mix_mlp(h_s, c_s, r_s, *_mlp_args(1, wts), True)

    kv_p = lambda z: jnp.transpose(z.reshape(b, FOX_HEADS, FOX_HEAD_DIM, s), (0, 3, 1, 2))[None]
    kv_s = lambda z: z.reshape(1, db, t, FOX_HEADS, FOX_HEAD_DIM)
    lf_out = lambda e, n, m: e["lf"][:, :FOX_HEADS].reshape(1, n, m, FOX_HEADS)
    return (y_p.reshape(b, s, d), y_s.reshape(db, t, d), pe["new_hist"][None], kv_p(pe["k"]), kv_p(pe["v"]),
            lf_out(pe, b, s), ret_p[None], se["new_hist"][None], kv_s(se["k"]), kv_s(se["v"]), lf_out(se, db, t),
            gv_s.reshape(1, db, t, GMLP_DIM), ret_s[None])
```

```python
import functools
import math

import numpy as np
import jax
import jax.numpy as jnp
from jax import lax
from jax.experimental import pallas as pl
from jax.experimental.pallas import tpu as pltpu

F32 = jnp.float32
BF16 = jnp.bfloat16
EPS = 1e-6

POOL_WINDOWS = (2, 4, 8, 16)
POOL_GROUP_DIM = 64
POOL_DIM = 256
POOL_HIST = 15
FOX_HEAD_DIM = 64
FOX_HEADS = 12
FOX_DIM = FOX_HEADS * FOX_HEAD_DIM
GMLP_CHUNK = 128
GMLP_GROUPS = 4
GMLP_DIM = 512
RET_HEADS = 4
RET_DIM = 512
RET_HEAD_DIM = 128
RET_CHUNK = 128
ROPE_BASE = 10000.0

LANES = 128
SUBLANES = 8
HEAD_ROWS = 16
SPLIT_STRIDE = 16
NEG_BIG = -1e30
LOG2E = math.log2(math.e)
VMEM_LIMIT = 56 * 1024 * 1024
ROW_TILE = 512
ATT_TILE = 512
PAGES_PER_STEP = 16
FUSED_ROW_TILE = 256
RET_BATCH_BLOCK = 8
SUFFIX_PAGE_TILE = 256


def _row_tile(rows):
    return ROW_TILE if rows % ROW_TILE == 0 else rows


def _params(*sem):
    return pltpu.CompilerParams(dimension_semantics=sem, vmem_limit_bytes=VMEM_LIMIT)


def _const_spec(shape):
    nd = len(shape)
    return pl.BlockSpec(shape, lambda *_: (0,) * nd, pipeline_mode=pl.Buffered(1))


def _rms(x, g):
    return x * lax.rsqrt(jnp.mean(x * x, axis=-1, keepdims=True) + EPS) * g


def _bdot(a, b):
    return jnp.dot(a, b, preferred_element_type=F32)


def _bdot_nt(a, b):
    return lax.dot_general(a, b, (((1,), (1,)), ((), ())), preferred_element_type=F32)


def _bdot_tn(a, b):
    return lax.dot_general(a, b, (((0,), (0,)), ((), ())), preferred_element_type=F32)


def _split3(x):
    hi = x.astype(BF16)
    r = x - hi.astype(F32)
    mid = r.astype(BF16)
    lo = (r - mid.astype(F32)).astype(BF16)
    return hi, mid, lo


def _log_sigmoid(x):
    return jnp.minimum(x, 0.0) - jnp.log1p(jnp.exp(-jnp.abs(x)))


def _aug_tables():
    perm = np.zeros((LANES, 2 * FOX_DIM), np.float32)
    const = np.zeros((1, 2 * FOX_DIM), np.float32)
    for h in range(FOX_HEADS):
        base = (h // 2) * LANES + (FOX_HEAD_DIM if h % 2 == 0 else 0)
        for j in range(3):
            perm[SPLIT_STRIDE * j + h, base + j] = 1.0
            const[0, FOX_DIM + base + j] = 1.0
            const[0, base + 3 + j] = 1.0
            perm[SPLIT_STRIDE * j + h, FOX_DIM + base + 3 + j] = -1.0
    return jnp.asarray(perm, BF16), jnp.asarray(const)


def _in_even_kernel(x_ref, g_ref, w_ref, bf_ref, tri_ref, perm_ref, const_ref,
                    a_ref, q_ref, k_ref, v_ref, kb_ref, vb_ref, lf_ref, cum_ref, cumt_ref, aq_ref, ak_ref,
                    carry_ref, *, tiles_per_seq, transposed_kv):
    i = pl.program_id(0)
    xn = _rms(x_ref[...], g_ref[...]).astype(BF16)
    proj = _bdot(xn, w_ref[...])
    o_q, o_k, o_v, o_f = POOL_DIM, POOL_DIM + FOX_DIM, POOL_DIM + 2 * FOX_DIM, POOL_DIM + 3 * FOX_DIM
    a_ref[...] = proj[:, :o_q]
    q_ref[...] = (proj[:, o_q:o_k] * (FOX_HEAD_DIM ** -0.5 * LOG2E)).astype(BF16)
    k = proj[:, o_k:o_v]
    v = proj[:, o_v:o_f]
    if transposed_kv:
        k_ref[0] = k.T
        v_ref[0] = v.T
    else:
        k_ref[...] = k
        v_ref[...] = v
    kb_ref[...] = k.astype(BF16)
    vb_ref[...] = v.astype(BF16)
    lf = _log_sigmoid(proj[:, o_f:] + bf_ref[...])
    lf_ref[...] = lf
    tri3 = _bdot(tri_ref[...], jnp.concatenate(_split3(lf), axis=1))
    cum = tri3[:, :LANES] + tri3[:, LANES:2 * LANES] + tri3[:, 2 * LANES:]
    if tiles_per_seq > 1:
        @pl.when(i % tiles_per_seq == 0)
        def _():
            carry_ref[...] = jnp.zeros_like(carry_ref)
        cum = cum + carry_ref[...]
        carry_ref[...] = cum[-1:, :]
    cum = cum * LOG2E
    cum_ref[...] = cum
    cumt_ref[0] = cum.T[:HEAD_ROWS, :]
    is_head = lax.broadcasted_iota(jnp.int32, (1, LANES), 1) < FOX_HEADS
    packed = sum(pltpu.roll(jnp.where(is_head, p.astype(F32), 0.0), SPLIT_STRIDE * j, 1)
                 for j, p in enumerate(_split3(cum)))
    aug = (_bdot(packed.astype(BF16), perm_ref[...]) + const_ref[...]).astype(BF16)
    aq_ref[...] = aug[:, :FOX_DIM]
    ak_ref[...] = aug[:, FOX_DIM:]


def _in_even(x, g, w, bf, seq_len, transposed_kv):
    rows, d = x.shape
    tm = _row_tile(rows)
    n_tiles = rows // tm
    r = np.arange(tm)
    if seq_len >= tm:
        assert seq_len % tm == 0
        tiles_per_seq = seq_len // tm
        tri = r[None, :] <= r[:, None]
        cumt_shape = (rows // seq_len, HEAD_ROWS, seq_len)
        cumt_map = lambda i: (i // tiles_per_seq, 0, i % tiles_per_seq)
    else:
        assert tm % seq_len == 0 and not transposed_kv
        tiles_per_seq = 1
        tri = (r[None, :] <= r[:, None]) & ((r[None, :] // seq_len) == (r[:, None] // seq_len))
        cumt_shape = (n_tiles, HEAD_ROWS, tm)
        cumt_map = lambda i: (i, 0, 0)
    tri = jnp.asarray(tri, BF16)
    perm, const = _aug_tables()
    n = w.shape[1]
    row = lambda width: pl.BlockSpec((tm, width), lambda i: (i, 0))
    if transposed_kv:
        kv_shape = jax.ShapeDtypeStruct((rows // seq_len, FOX_DIM, seq_len), F32)
        kv_spec = lambda: pl.BlockSpec((1, FOX_DIM, tm), cumt_map)
    else:
        kv_shape = jax.ShapeDtypeStruct((rows, FOX_DIM), F32)
        kv_spec = lambda: row(FOX_DIM)
    out_shape = (
        jax.ShapeDtypeStruct((rows, POOL_DIM), F32),
        jax.ShapeDtypeStruct((rows, FOX_DIM), BF16),
        kv_shape,
        kv_shape,
        jax.ShapeDtypeStruct((rows, FOX_DIM), BF16),
        jax.ShapeDtypeStruct((rows, FOX_DIM), BF16),
        jax.ShapeDtypeStruct((rows, LANES), F32),
        jax.ShapeDtypeStruct((rows, LANES), F32),
        jax.ShapeDtypeStruct(cumt_shape, F32),
        jax.ShapeDtypeStruct((rows, FOX_DIM), BF16),
        jax.ShapeDtypeStruct((rows, FOX_DIM), BF16),
    )
    out_specs = (row(POOL_DIM), row(FOX_DIM), kv_spec(), kv_spec(), row(FOX_DIM), row(FOX_DIM),
                 row(LANES), row(LANES), pl.BlockSpec((1, HEAD_ROWS, tm), cumt_map), row(FOX_DIM), row(FOX_DIM))
    return pl.pallas_call(
        functools.partial(_in_even_kernel, tiles_per_seq=tiles_per_seq, transposed_kv=transposed_kv),
        grid=(n_tiles,),
        in_specs=[row(d), _const_spec((1, d)), _const_spec((d, n)), _const_spec((1, LANES)),
                  _const_spec((tm, tm)), _const_spec(perm.shape), _const_spec(const.shape)],
        out_specs=out_specs,
        out_shape=out_shape,
        scratch_shapes=[pltpu.VMEM((1, LANES), F32)],
        compiler_params=_params("arbitrary"),
        name="in_even",
    )(x, g, w, bf, tri, perm, const)


def _pool_kernel(a_ref, hist_ref, w_ref, scale_ref, out_ref, newhist_ref, ext_ref, *, seq_len, start_pos):
    t = seq_len
    base = POOL_HIST + 1
    x0 = a_ref[0]
    ext_ref[0:1, :] = jnp.zeros((1, POOL_DIM), F32)
    ext_ref[1:base, :] = hist_ref[0]
    ext_ref[base:base + t, :] = x0

    def back(kk):
        return ext_ref[base - kk:base - kk + t, :]

    sums = []
    run = x0
    nxt = 1
    for w in POOL_WINDOWS:
        while nxt < w:
            run = run + back(nxt)
            nxt += 1
        sums.append(run)
    grp = lax.broadcasted_iota(jnp.int32, (1, POOL_DIM), 1) // POOL_GROUP_DIM
    sel = sums[-1]
    win = jnp.full((1, POOL_DIM), POOL_WINDOWS[-1], jnp.int32)
    for gi in range(len(POOL_WINDOWS) - 2, -1, -1):
        sel = jnp.where(grp == gi, sums[gi], sel)
        win = jnp.where(grp == gi, POOL_WINDOWS[gi], win)
    pos = start_pos + lax.broadcasted_iota(jnp.int32, (t, 1), 0)
    cnt = jnp.minimum(pos + 1, win).astype(F32)
    dd = sel / cnt - x0
    y = _bdot(dd.astype(BF16), w_ref[...]) * scale_ref[...]
    out_ref[0] = y.astype(out_ref.dtype)
    newhist_ref[0] = ext_ref[t + 1:t + base, :]


def _pool_mix(a, hist, w_bd, scale, start_pos):
    b, t, _ = a.shape
    return pl.pallas_call(
        functools.partial(_pool_kernel, seq_len=t, start_pos=start_pos),
        grid=(b,),
        in_specs=[pl.BlockSpec((1, t, POOL_DIM), lambda i: (i, 0, 0)),
                  pl.BlockSpec((1, POOL_HIST, POOL_DIM), lambda i: (i, 0, 0)),
                  _const_spec((POOL_DIM, POOL_DIM)), _const_spec((1, POOL_DIM))],
        out_specs=(pl.BlockSpec((1, t, POOL_DIM), lambda i: (i, 0, 0)),
                   pl.BlockSpec((1, POOL_HIST, POOL_DIM), lambda i: (i, 0, 0))),
        out_shape=(jax.ShapeDtypeStruct((b, t, POOL_DIM), BF16),
                   jax.ShapeDtypeStruct((b, POOL_HIST, POOL_DIM), F32)),
        scratch_shapes=[pltpu.VMEM((POOL_HIST + 1 + t, POOL_DIM), F32)],
        compiler_params=_params("arbitrary"),
        name="pool_mix",
    )(a, hist, w_bd, scale)


def _fox_prompt_kernel(q_ref, k_ref, v_ref, aq_ref, ak_ref, o_ref, *, tile):
    s_len = q_ref.shape[1]
    lane = lax.broadcasted_iota(jnp.int32, (1, LANES), 1)
    in_head = (lane < FOX_HEAD_DIM, lane >= FOX_HEAD_DIM)
    visible = (lax.broadcasted_iota(jnp.int32, (tile, tile), 1) <= lax.broadcasted_iota(jnp.int32, (tile, tile), 0))
    km = []
    for ki in range(s_len // tile):
        rows = slice(ki * tile, (ki + 1) * tile)
        km.append([jnp.where(in_head[e], k_ref[0, rows, :], ak_ref[0, rows, :]) for e in range(2)])
    for qi in range(s_len // tile):
        qrows = slice(qi * tile, (qi + 1) * tile)
        outs = []
        for e in range(2):
            qm = jnp.where(in_head[e], q_ref[0, qrows, :], aq_ref[0, qrows, :])
            m = jnp.full((tile, 1), NEG_BIG, F32)
            l = jnp.zeros((tile, 1), F32)
            acc = jnp.zeros((tile, LANES), F32)
            for ki in range(qi + 1):
                s = _bdot_nt(qm, km[ki][e])
                if ki == qi:
                    s = jnp.where(visible, s, NEG_BIG)
                m_new = jnp.maximum(m, jnp.max(s, axis=-1, keepdims=True))
                alpha = jnp.exp2(m - m_new)
                p = jnp.exp2(s - m_new)
                l = alpha * l + jnp.sum(p, axis=-1, keepdims=True)
                acc = alpha * acc + _bdot(p.astype(BF16), v_ref[0, ki * tile:(ki + 1) * tile, :])
                m = m_new
            outs.append(acc / l)
        o_ref[0, qrows, :] = jnp.where(in_head[0], outs[0], outs[1]).astype(o_ref.dtype)


def _fox_prompt(q, kb, vb, aq, ak):
    b, s, _ = q.shape
    tile = ATT_TILE if s % ATT_TILE == 0 else s
    assert s // tile <= 4, "the causal tile structure is unrolled in the kernel"
    full = lambda: pl.BlockSpec((1, s, LANES), lambda bi, hp: (bi, 0, hp))
    return pl.pallas_call(
        functools.partial(_fox_prompt_kernel, tile=tile),
        grid=(b, FOX_HEADS // 2),
        in_specs=[full(), full(), full(), full(), full()],
        out_specs=full(),
        out_shape=jax.ShapeDtypeStruct((b, s, FOX_DIM), BF16),
        compiler_params=_params("arbitrary", "arbitrary"),
        name="fox_prompt",
    )(q, kb, vb, aq, ak)


def _suffix_kernel(lf_ref, mat_ref, out_ref):
    for h in range(FOX_HEADS):
        out_ref[:, h, :] = sum(_bdot(p, mat_ref[...]) for p in _split3(lf_ref[h])) * LOG2E


def _page_suffix(lf_hpp):
    heads, n_phys, page = lf_hpp.shape
    tp = SUFFIX_PAGE_TILE if n_phys % SUFFIX_PAGE_TILE == 0 else n_phys
    r = np.arange(page)
    later = r[:, None] > r[None, :]
    mat = jnp.asarray(np.concatenate([later, np.ones((page, page), bool)], axis=1), BF16)
    return pl.pallas_call(
        _suffix_kernel,
        grid=(n_phys // tp,),
        in_specs=[pl.BlockSpec((heads, tp, page), lambda i: (0, i, 0)), _const_spec((page, 2 * page))],
        out_specs=pl.BlockSpec((tp, heads, 2 * page), lambda i: (i, 0, 0)),
        out_shape=jax.ShapeDtypeStruct((n_phys, heads, 2 * page), F32),
        compiler_params=_params("arbitrary"),
        name="page_suffix",
    )(lf_hpp, mat)


K_STREAM, V_STREAM, SUFFIX_STREAM = 0, 1, 2


def _page_copies(pt_ref, hbm, bufs, sems, slot, row, first_page, n_pg, streams):
    copies = []
    for kind in streams:
        for j in range(n_pg):
            pg = 0 if pt_ref is None else pt_ref[row, first_page + j]
            copies.append(pltpu.make_async_copy(hbm[kind].at[pg], bufs[kind].at[slot, j], sems.at[slot, kind]))
    return copies


def _fox_sample_mlp_kernel(pt_ref, q_ref, kn_ref, vn_ref, gq_ref, grow_ref, kt_hbm, vt_hbm, sfx_hbm,
                           h_ref, x1_ref, x2_ref, wo1_ref, wo2_ref, gf_ref, wu_ref, wd_ref, gl_ref,
                           o_ref, y_ref, m_sc, l_sc, acc_sc, run_sc, xn_sc, kbuf, vbuf, sbuf, sems, *,
                           n_pg, n_pages, t_new, steps_per_tile, ff_chunk, final_norm):
    n_seq, steps = pl.num_programs(0), pl.num_programs(1)
    bi, i = pl.program_id(0), pl.program_id(1)
    last = steps - 1
    page = LANES
    g = bi * steps + i
    slot = g % 2
    chunk = g % steps_per_tile
    hbm, bufs = (kt_hbm, vt_hbm, sfx_hbm), (kbuf, vbuf, sbuf)
    first_page = lambda step: n_pages - n_pg * (step + 1)
    issue_order = (K_STREAM, SUFFIX_STREAM, V_STREAM)

    @pl.when(g == 0)
    def _():
        for cp in _page_copies(pt_ref, hbm, bufs, sems, 0, 0, first_page(0), n_pg, issue_order):
            cp.start()

    @pl.when(i == 0)
    def _():
        m_sc[...] = jnp.full_like(m_sc, NEG_BIG)
        l_sc[...] = jnp.zeros_like(l_sc)
        acc_sc[...] = jnp.zeros_like(acc_sc)
        run_sc[...] = jnp.zeros_like(run_sc)

    @pl.when(chunk == 0)
    def _():
        h1 = h_ref[...] + _bdot(x1_ref[...], wo1_ref[...]) + _bdot(x2_ref[...], wo2_ref[...])
        y_ref[...] = h1
        xn_sc[...] = _rms(h1, gf_ref[...]).astype(BF16)

    for cp in _page_copies(None, hbm, bufs, sems, slot, 0, 0, n_pg, (K_STREAM, SUFFIX_STREAM)):
        cp.wait()
    wrap_i = i == last
    next_i = jnp.where(wrap_i, 0, i + 1)
    next_bi = jnp.where(wrap_i, jnp.where(bi == n_seq - 1, 0, bi + 1), bi)
    for cp in _page_copies(pt_ref, hbm, bufs, sems, 1 - slot, next_bi, first_page(next_i), n_pg, issue_order):
        cp.start()

    run = run_sc[...]
    bias = [None] * n_pg
    for j in range(n_pg - 1, -1, -1):
        sf = sbuf[slot, j]
        bias[j] = sf[:, :page] + run
        run = run + sf[:, page:]
    run_sc[...] = run

    t = t_new
    q = q_ref[0]
    gq = gq_ref[0]
    s_parts = []
    for h in range(FOX_HEADS):
        hs = slice(h * FOX_HEAD_DIM, (h + 1) * FOX_HEAD_DIM)
        kt = jnp.concatenate([kbuf[slot, j, h] for j in range(n_pg)], axis=1).astype(BF16)
        s_parts.append(_bdot(q[:, hs], kt) + jnp.concatenate([bias[j][h:h + 1, :] for j in range(n_pg)], axis=1))
    s = jnp.concatenate(s_parts, axis=0) + jnp.concatenate([gq] * n_pg, axis=1)
    ff0 = pl.multiple_of(chunk * ff_chunk, ff_chunk)
    u = jnp.maximum(_bdot(xn_sc[...], wu_ref[:, pl.ds(ff0, ff_chunk)]), 0.0)
    u = (u * u).astype(BF16)
    m_prev = m_sc[...]
    m_new = jnp.maximum(m_prev, jnp.max(s, axis=-1, keepdims=True))
    alpha = jnp.exp2(m_prev - m_new)
    p32 = jnp.exp2(s - jnp.concatenate([m_new] * n_pg, axis=1))
    l_sc[...] = alpha * l_sc[...] + jnp.sum(p32, axis=-1, keepdims=True)
    m_sc[...] = m_new
    for cp in _page_copies(None, hbm, bufs, sems, slot, 0, 0, n_pg, (V_STREAM,)):
        cp.wait()
    pv = []
    for h in range(FOX_HEADS):
        vt = jnp.concatenate([vbuf[slot, j, h] for j in range(n_pg)], axis=1).astype(BF16)
        pv.append(_bdot_nt(p32[h * t:(h + 1) * t].astype(BF16), vt))
    acc_sc[...] = alpha[:, :FOX_HEAD_DIM] * acc_sc[...] + jnp.concatenate(pv, axis=0)

    y_ref[...] += _bdot(u, wd_ref[pl.ds(ff0, ff_chunk), :])
    if final_norm:
        @pl.when(chunk == steps_per_tile - 1)
        def _():
            y_ref[...] = _rms(y_ref[...], gl_ref[...])

    @pl.when(i == last)
    def _():
        kn = kn_ref[0]
        vn = vn_ref[0]
        grow = grow_ref[0]
        r = lax.broadcasted_iota(jnp.int32, (t, t), 0)
        c = lax.broadcasted_iota(jnp.int32, (t, t), 1)
        for h in range(FOX_HEADS):
            hs = slice(h * FOX_HEAD_DIM, (h + 1) * FOX_HEAD_DIM)
            rows = slice(h * t, (h + 1) * t)
            s = _bdot_nt(q[:, hs], kn[:, hs]) + (gq[rows, :1] - grow[h:h + 1, :])
            s = jnp.where(c <= r, s, NEG_BIG)
            m_prev = m_sc[rows, :1]
            m_new = jnp.maximum(m_prev, jnp.max(s, axis=-1, keepdims=True))
            alpha = jnp.exp2(m_prev - m_new)
            p32 = jnp.exp2(s - m_new)
            l = alpha * l_sc[rows, :1] + jnp.sum(p32, axis=-1, keepdims=True)
            acc = alpha * acc_sc[rows, :] + _bdot(p32.astype(BF16), vn[:, hs])
            o_ref[0, :, hs] = (acc / l).astype(o_ref.dtype)

    @pl.when(g == n_seq * steps - 1)
    def _():
        for cp in _page_copies(None, hbm, bufs, sems, 1 - slot, 0, 0, n_pg, issue_order):
            cp.wait()


def _fox_sample_mlp(page_table, q, kn, vn, gq, grow, cache_kt, cache_vt, suffix,
                    h, x1, x2, wo1, wo2, gf, wu, wd, gl, final_norm):
    b, t, _ = q.shape
    n_pages = page_table.shape[1]
    n_pg = PAGES_PER_STEP if n_pages % PAGES_PER_STEP == 0 else 1
    steps = n_pages // n_pg
    page = cache_kt.shape[-1]
    assert page == LANES
    rows, d = h.shape
    d_ff = wu.shape[1]
    tm = FUSED_ROW_TILE if rows % FUSED_ROW_TILE == 0 else rows
    tiles = rows // tm
    assert (b * steps) % tiles == 0, "grid steps must split evenly over the MLP row tiles"
    spt = (b * steps) // tiles
    assert d_ff % spt == 0 and (d_ff // spt) % LANES == 0
    ck = d_ff // spt
    row = lambda width: pl.BlockSpec((tm, width), lambda bi, i, pt: ((bi * steps + i) // spt, 0))

    tok = lambda width: pl.BlockSpec((1, t, width), lambda bi, i, pt: (bi, 0, 0))
    in_hbm = lambda: pl.BlockSpec(memory_space=pl.ANY)
    grid_spec = pltpu.PrefetchScalarGridSpec(
        num_scalar_prefetch=1,
        grid=(b, steps),
        in_specs=[tok(FOX_DIM), tok(FOX_DIM), tok(FOX_DIM),
                  pl.BlockSpec((1, FOX_HEADS * t, LANES), lambda bi, i, pt: (bi, 0, 0)),
                  pl.BlockSpec((1, HEAD_ROWS, t), lambda bi, i, pt: (bi, 0, 0)),
                  in_hbm(), in_hbm(), in_hbm(),
                  row(d), row(x1.shape[1]), row(x2.shape[1]), _const_spec(wo1.shape), _const_spec(wo2.shape),
                  _const_spec((1, d)), _const_spec(wu.shape), _const_spec(wd.shape), _const_spec((1, d))],
        out_specs=(tok(FOX_DIM), row(d)),
        scratch_shapes=[pltpu.VMEM((FOX_HEADS * t, LANES), F32), pltpu.VMEM((FOX_HEADS * t, LANES), F32),
                        pltpu.VMEM((FOX_HEADS * t, FOX_HEAD_DIM), F32), pltpu.VMEM((FOX_HEADS, LANES), F32),
                        pltpu.VMEM((tm, d), BF16),
                        pltpu.VMEM((2, n_pg, FOX_HEADS, FOX_HEAD_DIM, page), F32),
                        pltpu.VMEM((2, n_pg, FOX_HEADS, FOX_HEAD_DIM, page), F32),
                        pltpu.VMEM((2, n_pg, FOX_HEADS, 2 * page), F32),
                        pltpu.SemaphoreType.DMA((2, 3))],
    )
    return pl.pallas_call(
        functools.partial(_fox_sample_mlp_kernel, n_pg=n_pg, n_pages=n_pages, t_new=t, steps_per_tile=spt,
                          ff_chunk=ck, final_norm=final_norm),
        grid_spec=grid_spec,
        out_shape=(jax.ShapeDtypeStruct((b, t, FOX_DIM), BF16), jax.ShapeDtypeStruct((rows, d), F32)),
        compiler_params=_params("arbitrary", "arbitrary"),
        name="fox_sample_mlp",
    )(page_table, q, kn, vn, gq, grow, cache_kt, cache_vt, suffix, h, x1, x2, wo1, wo2, gf, wu, wd, gl)


def _mix_mlp_kernel(h_ref, x1_ref, x2_ref, wo1_ref, wo2_ref, gf_ref, wu_ref, wd_ref, gl_ref, o_ref, *,
                    ff_chunk, final_norm):
    h1 = h_ref[...] + _bdot(x1_ref[...], wo1_ref[...]) + _bdot(x2_ref[...], wo2_ref[...])
    xn = _rms(h1, gf_ref[...]).astype(BF16)
    o_ref[...] = h1
    d_ff = wu_ref.shape[1]
    for c in range(d_ff // ff_chunk):
        u = jnp.maximum(_bdot(xn, wu_ref[:, c * ff_chunk:(c + 1) * ff_chunk]), 0.0)
        o_ref[...] += _bdot((u * u).astype(BF16), wd_ref[c * ff_chunk:(c + 1) * ff_chunk, :])
    if final_norm:
        o_ref[...] = _rms(o_ref[...], gl_ref[...])


def _mix_mlp(h, x1, x2, wo1, wo2, gf, wu, wd, gl, final_norm):
    rows, d = h.shape
    tm = _row_tile(rows)
    d_ff = wu.shape[1]
    row = lambda width: pl.BlockSpec((tm, width), lambda i: (i, 0))
    return pl.pallas_call(
        functools.partial(_mix_mlp_kernel, ff_chunk=min(d_ff, 1024), final_norm=final_norm),
        grid=(rows // tm,),
        in_specs=[row(d), row(x1.shape[1]), row(x2.shape[1]), _const_spec(wo1.shape), _const_spec(wo2.shape),
                  _const_spec((1, d)), _const_spec(wu.shape), _const_spec(wd.shape), _const_spec((1, d))],
        out_specs=row(d),
        out_shape=jax.ShapeDtypeStruct((rows, d), F32),
        compiler_params=_params("arbitrary"),
        name="mix_mlp",
    )(h, x1, x2, wo1, wo2, gf, wu, wd, gl)


def _in_odd_kernel(x_ref, g_ref, w_ref, lng_ref, lnb_ref, cos_ref, sin_ref, ws_ref, bs_ref,
                   c_ref, vn_ref, q_ref, k_ref, rv_ref, rg_ref, *, chunk):
    xn = _rms(x_ref[...], g_ref[...]).astype(BF16)
    proj = _bdot(xn, w_ref[...])
    tm = proj.shape[0]
    gd, rd = GMLP_DIM, RET_DIM
    gv = proj[:, gd:2 * gd]
    mu = jnp.mean(gv, axis=-1, keepdims=True)
    var = jnp.mean((gv - mu) ** 2, axis=-1, keepdims=True)
    vn = (gv - mu) * lax.rsqrt(var + EPS) * lng_ref[...] + lnb_ref[...]
    vn_ref[...] = vn
    vnb = vn.astype(BF16)
    wrows = ws_ref.shape[1]
    r = lax.broadcasted_iota(jnp.int32, (wrows, wrows), 0)
    c = lax.broadcasted_iota(jnp.int32, (wrows, wrows), 1)
    causal_in_chunk = (c <= r) & (c >= (r // chunk) * chunk)
    for gi in range(GMLP_GROUPS):
        sl = slice(gi * LANES, (gi + 1) * LANES)
        ws = jnp.where(causal_in_chunk, ws_ref[gi], 0.0).astype(BF16)
        for ti in range(tm // wrows):
            rows = slice(ti * wrows, (ti + 1) * wrows)
            z = _bdot(ws, vnb[rows, sl]) + bs_ref[gi]
            c_ref[rows, sl] = (proj[rows, sl] * z).astype(c_ref.dtype)
    cos = cos_ref[...]
    sin = sin_ref[...]
    o_q, o_k, o_v, o_g = 2 * gd, 2 * gd + rd, 2 * gd + 2 * rd, 2 * gd + 3 * rd
    for hh in range(RET_HEADS):
        sl = slice(hh * RET_HEAD_DIM, (hh + 1) * RET_HEAD_DIM)
        qh = proj[:, o_q + hh * RET_HEAD_DIM:o_q + (hh + 1) * RET_HEAD_DIM]
        kh = proj[:, o_k + hh * RET_HEAD_DIM:o_k + (hh + 1) * RET_HEAD_DIM]
        q_ref[:, sl] = (qh * cos + pltpu.roll(qh, RET_HEAD_DIM // 2, 1) * sin).astype(BF16)
        k_ref[:, sl] = (kh * cos + pltpu.roll(kh, RET_HEAD_DIM // 2, 1) * sin) * (RET_HEAD_DIM ** -0.5)
    rv_ref[...] = proj[:, o_v:o_g].astype(BF16)
    rg_ref[...] = proj[:, o_g:]


def _in_odd(x, g, w, lng, lnb, cos_t, sin_t, gmlp_ws, gmlp_bs, seq_len):
    rows, d = x.shape
    tm = _row_tile(rows)
    chunk = min(seq_len, GMLP_CHUNK)
    assert seq_len % chunk == 0 and tm % chunk == 0
    wrows = chunk if chunk % LANES == 0 else tm
    reps = wrows // chunk
    ws_t = jnp.tile(gmlp_ws[:, :chunk, :chunk], (1, reps, reps))
    bs_t = jnp.broadcast_to(jnp.tile(gmlp_bs[:, :chunk], (1, reps))[:, :, None], (GMLP_GROUPS, wrows, LANES))
    if seq_len >= tm:
        tiles_per_seq = seq_len // tm
        rope_map = lambda i: (i % tiles_per_seq, 0)
    else:
        reps = tm // seq_len
        cos_t = jnp.tile(cos_t, (reps, 1))
        sin_t = jnp.tile(sin_t, (reps, 1))
        rope_map = lambda i: (0, 0)
    row = lambda width: pl.BlockSpec((tm, width), lambda i: (i, 0))
    outs = ((GMLP_DIM, BF16), (GMLP_DIM, F32), (RET_DIM, BF16), (RET_DIM, F32), (RET_DIM, BF16), (RET_DIM, F32))
    return pl.pallas_call(
        functools.partial(_in_odd_kernel, chunk=chunk),
        grid=(rows // tm,),
        in_specs=[row(d), _const_spec((1, d)), _const_spec(w.shape), _const_spec((1, GMLP_DIM)),
                  _const_spec((1, GMLP_DIM)), pl.BlockSpec((tm, RET_HEAD_DIM), rope_map),
                  pl.BlockSpec((tm, RET_HEAD_DIM), rope_map), _const_spec(ws_t.shape), _const_spec(bs_t.shape)],
        out_specs=tuple(row(wd) for wd, _ in outs),
        out_shape=tuple(jax.ShapeDtypeStruct((rows, wd), dt) for wd, dt in outs),
        compiler_params=_params("arbitrary"),
        name="in_odd",
    )(x, g, w, lng, lnb, cos_t, sin_t, ws_t, bs_t)


def _ret_kernel(q_ref, k_ref, v_ref, g_ref, s0_ref, dintra_ref, dq_ref, dk_ref, ds_ref, gn_ref,
                o_ref, sout_ref, st_sc, *, batch_block):
    c = pl.program_id(1)

    @pl.when(c == 0)
    def _():
        st_sc[...] = s0_ref[...]

    for bi in range(batch_block):
        for hh in range(RET_HEADS):
            sl = slice(hh * RET_HEAD_DIM, (hh + 1) * RET_HEAD_DIM)
            q = q_ref[bi, :, sl]
            kf = k_ref[bi, :, sl]
            v = v_ref[bi, :, sl]
            st = st_sc[bi, hh]
            sc = _bdot_nt(q, kf.astype(BF16)) * dintra_ref[hh]
            o = _bdot(sc.astype(BF16), v) + _bdot(q, st.astype(BF16)) * dq_ref[hh]
            st_sc[bi, hh] = st * ds_ref[hh] + _bdot_tn((kf * dk_ref[hh]).astype(BF16), v)
            mu = jnp.mean(o, axis=-1, keepdims=True)
            var = jnp.mean((o - mu) ** 2, axis=-1, keepdims=True)
            on = (o - mu) * lax.rsqrt(var + EPS) * gn_ref[:, sl]
            gate = g_ref[bi, :, sl]
            o_ref[bi, :, sl] = (on * (gate * jax.nn.sigmoid(gate))).astype(o_ref.dtype)

    @pl.when(c == pl.num_programs(1) - 1)
    def _():
        sout_ref[...] = st_sc[...]


def _retention(q, k, v, gate, s0, gn):
    b, t, _ = q.shape
    bb = RET_BATCH_BLOCK if b % RET_BATCH_BLOCK == 0 else 1
    lc = math.gcd(t, RET_CHUNK) if t >= RET_CHUNK else t
    log_g = jnp.log1p(-jnp.exp2(-5.0 - jnp.arange(RET_HEADS, dtype=F32)))
    idx = jnp.arange(lc, dtype=F32)
    diff = idx[:, None] - idx[None, :]
    dintra = jnp.where(diff[None] >= 0, jnp.exp(diff[None] * log_g[:, None, None]), 0.0)
    lanes = (RET_HEADS, lc, RET_HEAD_DIM)
    dq = jnp.broadcast_to(jnp.exp((idx[None, :] + 1.0) * log_g[:, None])[:, :, None], lanes)
    dk = jnp.broadcast_to(jnp.exp((lc - 1.0 - idx)[None, :] * log_g[:, None])[:, :, None], lanes)
    ds = jnp.broadcast_to(jnp.exp(lc * log_g)[:, None, None], (RET_HEADS, 1, RET_HEAD_DIM))
    blk = lambda: pl.BlockSpec((bb, lc, RET_DIM), lambda bi, ci: (bi, ci, 0))
    st = lambda: pl.BlockSpec((bb, RET_HEADS, RET_HEAD_DIM, RET_HEAD_DIM), lambda bi, ci: (bi, 0, 0, 0))
    return pl.pallas_call(
        functools.partial(_ret_kernel, batch_block=bb),
        grid=(b // bb, t // lc),
        in_specs=[blk(), blk(), blk(), blk(), st(), _const_spec(dintra.shape), _const_spec(lanes),
                  _const_spec(lanes), _const_spec(ds.shape), _const_spec((1, RET_DIM))],
        out_specs=(blk(), st()),
        out_shape=(jax.ShapeDtypeStruct((b, t, RET_DIM), BF16),
                   jax.ShapeDtypeStruct((b, RET_HEADS, RET_HEAD_DIM, RET_HEAD_DIM), F32)),
        scratch_shapes=[pltpu.VMEM((bb, RET_HEADS, RET_HEAD_DIM, RET_HEAD_DIM), F32)],
        compiler_params=_params("arbitrary", "arbitrary"),
        name="retention",
    )(q, k, v, gate, s0, dintra, dq, dk, ds, gn)


def _rope_tables(start_pos, t):
    half = RET_HEAD_DIM // 2
    pos = (start_pos + jnp.arange(t)).astype(F32)
    inv = ROPE_BASE ** (-jnp.arange(half, dtype=F32) / half)
    ang = pos[:, None] * inv[None, :]
    cos, sin = jnp.cos(ang), jnp.sin(ang)
    return jnp.concatenate([cos, cos], axis=1), jnp.concatenate([-sin, sin], axis=1)


def _block_diag(w):
    g, n, _ = w.shape
    out = jnp.zeros((g * n, g * n), w.dtype)
    for i in range(g):
        out = out.at[i * n:(i + 1) * n, i * n:(i + 1) * n].set(w[i])
    return out


def _even_front(x, start_pos, hist0, transposed_kv, wts):
    b, t, d = x.shape
    e = dict(zip(("a", "q", "k", "v", "kb", "vb", "lf", "cum", "cumt", "aq", "ak"),
                 _in_even(x.reshape(b * t, d), wts["norm_mix"][0].reshape(1, -1), wts["w_in_even"], wts["b_forget"],
                          t, transposed_kv)))
    e["pool_out"], e["new_hist"] = _pool_mix(e["a"].reshape(b, t, POOL_DIM), hist0, wts["pool_w"], wts["pool_scale"],
                                             start_pos)
    return e


def _odd_front(h, b, t, start_pos, state0, wts):
    row1 = lambda a: a.reshape(1, -1)
    r3 = lambda z: z.reshape(b, t, z.shape[-1])
    cos_t, sin_t = _rope_tables(start_pos, t)
    c_out, vn, rq, rk, rv, rg = _in_odd(h, row1(wts["norm_mix"][1]), wts["w_in_odd"], row1(wts["gmlp_ln_g"]),
                                        row1(wts["gmlp_ln_b"]), cos_t, sin_t, wts["gmlp_ws"], wts["gmlp_bs"], t)
    r_out, s_new = _retention(r3(rq), r3(rk), r3(rv), r3(rg), state0, row1(wts["ret_gn_g"]))
    return c_out, r_out.reshape(b * t, RET_DIM), vn, s_new


def _mlp_args(layer, wts):
    w_out, split = (wts["w_out_even"], POOL_DIM) if layer == 0 else (wts["w_out_odd"], GMLP_DIM)
    return (w_out[:split], w_out[split:], wts["norm_ffn"][layer].reshape(1, -1), wts["w_up"][layer],
            wts["w_down"][layer], wts["norm_final"].reshape(1, -1))


def kernel(x_prompt, x_sample, state_pool, cache_k, cache_v, cache_logf, page_table, state_ret, norm_mix, w_in_even, b_forget, pool_w, pool_scale, w_out_even, w_in_odd, gmlp_ln_g, gmlp_ln_b, gmlp_ws, gmlp_bs, ret_gn_g, w_out_odd, norm_ffn, w_up, w_down, norm_final):
    assert norm_mix.shape[0] == 2 and w_in_even.shape[0] == 1 and w_in_odd.shape[0] == 1
    d = x_prompt.shape[-1]
    ev_cols = w_in_even.shape[-1]
    ev_pad = POOL_DIM + 3 * FOX_DIM + LANES - ev_cols
    wts = {
        "norm_mix": norm_mix, "norm_ffn": norm_ffn, "norm_final": norm_final,
        "w_in_even": jnp.pad(w_in_even[0], ((0, 0), (0, ev_pad))).astype(BF16),
        "b_forget": jnp.pad(b_forget[0], (0, LANES - FOX_HEADS)).reshape(1, LANES),
        "pool_w": _block_diag(pool_w[0]).astype(BF16),
        "pool_scale": pool_scale[0].reshape(1, POOL_DIM),
        "w_out_even": w_out_even[0].astype(BF16),
        "w_in_odd": w_in_odd[0].astype(BF16),
        "gmlp_ln_g": gmlp_ln_g[0], "gmlp_ln_b": gmlp_ln_b[0], "gmlp_ws": gmlp_ws[0], "gmlp_bs": gmlp_bs[0],
        "ret_gn_g": ret_gn_g[0],
        "w_out_odd": w_out_odd[0].astype(BF16),
        "w_up": w_up.astype(BF16), "w_down": w_down.astype(BF16),
    }
    b, s, _ = x_prompt.shape
    db, t, _ = x_sample.shape
    n_pages = page_table.shape[1]
    page = cache_k.shape[2]
    past = n_pages * page

    cache_kt = jnp.transpose(cache_k[0], (0, 2, 3, 1))
    cache_vt = jnp.transpose(cache_v[0], (0, 2, 3, 1))
    suffix = _page_suffix(jnp.transpose(cache_logf[0], (2, 0, 1)))

    se = _even_front(x_sample, past, state_pool[0], False, wts)
    pe = _even_front(x_prompt, 0, jnp.zeros((b, POOL_HIST, POOL_DIM), F32), True, wts)
    p3 = lambda z: z.reshape(b, s, z.shape[-1])
    att_p = _fox_prompt(p3(pe["q"]), p3(pe["kb"]), p3(pe["vb"]), p3(pe["aq"]), p3(pe["ak"]))

    assert db % 2 == 0
    half = db // 2
    s3 = lambda z: z.reshape(db, t, z.shape[-1])
    grow = jnp.transpose(s3(se["cum"]), (0, 2, 1))[:, :HEAD_ROWS, :]
    gq = jnp.broadcast_to(grow[:, :FOX_HEADS, :, None], (db, FOX_HEADS, t, LANES)).reshape(db, FOX_HEADS * t, LANES)
    sq, skb, svb = s3(se["q"]), s3(se["kb"]), s3(se["vb"])

    def attend_half(sel, h, x1, x2, layer):
        return _fox_sample_mlp(page_table[sel], sq[sel], skb[sel], svb[sel], gq[sel], grow[sel], cache_kt, cache_vt,
                               suffix, h, x1, x2, *_mlp_args(layer, wts), layer == 1)

    att_s0, h_p = attend_half(slice(0, half), x_prompt.reshape(b * s, d), pe["pool_out"].reshape(b * s, POOL_DIM),
                              att_p.reshape(b * s, FOX_DIM), 0)
    c_p, r_p, _, ret_p = _odd_front(h_p, b, s, 0, jnp.zeros((b, RET_HEADS, RET_HEAD_DIM, RET_HEAD_DIM), F32), wts)
    att_s1, y_p = attend_half(slice(half, db), h_p, c_p, r_p, 1)

    att_s = jnp.concatenate([att_s0, att_s1], axis=0)
    h_s = _mix_mlp(x_sample.reshape(db * t, d), se["pool_out"].reshape(db * t, POOL_DIM),
                   att_s.reshape(db * t, FOX_DIM), *_mlp_args(0, wts), False)
    c_s, r_s, gv_s, ret_s = _odd_front(h_s, db, t, past, state_ret[0], wts)
    y_s = _mix_mlp(h_s, c_s, r_s, *_mlp_args(1, wts), True)

    kv_p = lambda z: jnp.transpose(z.reshape(b, FOX_HEADS, FOX_HEAD_DIM, s), (0, 3, 1, 2))[None]
    kv_s = lambda z: z.reshape(1, db, t, FOX_HEADS, FOX_HEAD_DIM)
    lf_out = lambda e, n, m: e["lf"][:, :FOX_HEADS].reshape(1, n, m, FOX_HEADS)
    return (y_p.reshape(b, s, d), y_s.reshape(db, t, d), pe["new_hist"][None], kv_p(pe["k"]), kv_p(pe["v"]),
            lf_out(pe, b, s), ret_p[None], se["new_hist"][None], kv_s(se["k"]), kv_s(se["v"]), lf_out(se, db, t),
            gv_s.reshape(1, db, t, GMLP_DIM), ret_s[None])
```

```python
import functools
import math

import numpy as np
import jax
import jax.numpy as jnp
from jax import lax
from jax.experimental import pallas as pl
from jax.experimental.pallas import tpu as pltpu

F32 = jnp.float32
BF16 = jnp.bfloat16
EPS = 1e-6

POOL_WINDOWS = (2, 4, 8, 16)
POOL_GROUP_DIM = 64
POOL_DIM = 256
POOL_HIST = 15
FOX_HEAD_DIM = 64
FOX_HEADS = 12
FOX_DIM = FOX_HEADS * FOX_HEAD_DIM
GMLP_CHUNK = 128
GMLP_GROUPS = 4
GMLP_DIM = 512
RET_HEADS = 4
RET_DIM = 512
RET_HEAD_DIM = 128
RET_CHUNK = 128
ROPE_BASE = 10000.0

LANES = 128
SUBLANES = 8
HEAD_ROWS = 16
SPLIT_STRIDE = 16
NEG_BIG = -1e30
LOG2E = math.log2(math.e)
VMEM_LIMIT = 56 * 1024 * 1024
ROW_TILE = 512
ATT_TILE = 512
PAGES_PER_STEP = 16
FUSED_ROW_TILE = 256
RET_BATCH_BLOCK = 8
SUFFIX_PAGE_TILE = 256


def _row_tile(rows):
    return ROW_TILE if rows % ROW_TILE == 0 else rows


def _params(*sem):
    return pltpu.CompilerParams(dimension_semantics=sem, vmem_limit_bytes=VMEM_LIMIT)


def _const_spec(shape):
    nd = len(shape)
    return pl.BlockSpec(shape, lambda *_: (0,) * nd, pipeline_mode=pl.Buffered(1))


def _rms(x, g):
    return x * lax.rsqrt(jnp.mean(x * x, axis=-1, keepdims=True) + EPS) * g


def _bdot(a, b):
    return jnp.dot(a, b, preferred_element_type=F32)


def _bdot_nt(a, b):
    return lax.dot_general(a, b, (((1,), (1,)), ((), ())), preferred_element_type=F32)


def _bdot_tn(a, b):
    return lax.dot_general(a, b, (((0,), (0,)), ((), ())), preferred_element_type=F32)


def _split3(x):
    hi = x.astype(BF16)
    r = x - hi.astype(F32)
    mid = r.astype(BF16)
    lo = (r - mid.astype(F32)).astype(BF16)
    return hi, mid, lo


def _log_sigmoid(x):
    return jnp.minimum(x, 0.0) - jnp.log1p(jnp.exp(-jnp.abs(x)))


def _aug_tables():
    perm = np.zeros((LANES, 2 * FOX_DIM), np.float32)
    const = np.zeros((1, 2 * FOX_DIM), np.float32)
    for h in range(FOX_HEADS):
        base = (h // 2) * LANES + (FOX_HEAD_DIM if h % 2 == 0 else 0)
        for j in range(3):
            perm[SPLIT_STRIDE * j + h, base + j] = 1.0
            const[0, FOX_DIM + base + j] = 1.0
            const[0, base + 3 + j] = 1.0
            perm[SPLIT_STRIDE * j + h, FOX_DIM + base + 3 + j] = -1.0
    return jnp.asarray(perm, BF16), jnp.asarray(const)


def _in_even_kernel(x_ref, g_ref, w_ref, bf_ref, tri_ref, perm_ref, const_ref,
                    a_ref, q_ref, k_ref, v_ref, kb_ref, vb_ref, lf_ref, cum_ref, cumt_ref, aq_ref, ak_ref,
                    carry_ref, *, tiles_per_seq, transposed_kv):
    i = pl.program_id(0)
    xn = _rms(x_ref[...], g_ref[...]).astype(BF16)
    proj = _bdot(xn, w_ref[...])
    o_q, o_k, o_v, o_f = POOL_DIM, POOL_DIM + FOX_DIM, POOL_DIM + 2 * FOX_DIM, POOL_DIM + 3 * FOX_DIM
    a_ref[...] = proj[:, :o_q]
    q_ref[...] = (proj[:, o_q:o_k] * (FOX_HEAD_DIM ** -0.5 * LOG2E)).astype(BF16)
    k = proj[:, o_k:o_v]
    v = proj[:, o_v:o_f]
    if transposed_kv:
        k_ref[0] = k.T
        v_ref[0] = v.T
    else:
        k_ref[...] = k
        v_ref[...] = v
    kb_ref[...] = k.astype(BF16)
    vb_ref[...] = v.astype(BF16)
    lf = _log_sigmoid(proj[:, o_f:] + bf_ref[...])
    lf_ref[...] = lf
    tri3 = _bdot(tri_ref[...], jnp.concatenate(_split3(lf), axis=1))
    cum = tri3[:, :LANES] + tri3[:, LANES:2 * LANES] + tri3[:, 2 * LANES:]
    if tiles_per_seq > 1:
        @pl.when(i % tiles_per_seq == 0)
        def _():
            carry_ref[...] = jnp.zeros_like(carry_ref)
        cum = cum + carry_ref[...]
        carry_ref[...] = cum[-1:, :]
    cum = cum * LOG2E
    cum_ref[...] = cum
    cumt_ref[0] = cum.T[:HEAD_ROWS, :]
    is_head = lax.broadcasted_iota(jnp.int32, (1, LANES), 1) < FOX_HEADS
    packed = sum(pltpu.roll(jnp.where(is_head, p.astype(F32), 0.0), SPLIT_STRIDE * j, 1)
                 for j, p in enumerate(_split3(cum)))
    aug = (_bdot(packed.astype(BF16), perm_ref[...]) + const_ref[...]).astype(BF16)
    aq_ref[...] = aug[:, :FOX_DIM]
    ak_ref[...] = aug[:, FOX_DIM:]


def _in_even(x, g, w, bf, seq_len, transposed_kv):
    rows, d = x.shape
    tm = _row_tile(rows)
    n_tiles = rows // tm
    r = np.arange(tm)
    if seq_len >= tm:
        assert seq_len % tm == 0
        tiles_per_seq = seq_len // tm
        tri = r[None, :] <= r[:, None]
        cumt_shape = (rows // seq_len, HEAD_ROWS, seq_len)
        cumt_map = lambda i: (i // tiles_per_seq, 0, i % tiles_per_seq)
    else:
        assert tm % seq_len == 0 and not transposed_kv
        tiles_per_seq = 1
        tri = (r[None, :] <= r[:, None]) & ((r[None, :] // seq_len) == (r[:, None] // seq_len))
        cumt_shape = (n_tiles, HEAD_ROWS, tm)
        cumt_map = lambda i: (i, 0, 0)
    tri = jnp.asarray(tri, BF16)
    perm, const = _aug_tables()
    n = w.shape[1]
    row = lambda width: pl.BlockSpec((tm, width), lambda i: (i, 0))
    if transposed_kv:
        kv_shape = jax.ShapeDtypeStruct((rows // seq_len, FOX_DIM, seq_len), F32)
        kv_spec = lambda: pl.BlockSpec((1, FOX_DIM, tm), cumt_map)
    else:
        kv_shape = jax.ShapeDtypeStruct((rows, FOX_DIM), F32)
        kv_spec = lambda: row(FOX_DIM)
    out_shape = (
        jax.ShapeDtypeStruct((rows, POOL_DIM), F32),
        jax.ShapeDtypeStruct((rows, FOX_DIM), BF16),
        kv_shape,
        kv_shape,
        jax.ShapeDtypeStruct((rows, FOX_DIM), BF16),
        jax.ShapeDtypeStruct((rows, FOX_DIM), BF16),
        jax.ShapeDtypeStruct((rows, LANES), F32),
        jax.ShapeDtypeStruct((rows, LANES), F32),
        jax.ShapeDtypeStruct(cumt_shape, F32),
        jax.ShapeDtypeStruct((rows, FOX_DIM), BF16),
        jax.ShapeDtypeStruct((rows, FOX_DIM), BF16),
    )
    out_specs = (row(POOL_DIM), row(FOX_DIM), kv_spec(), kv_spec(), row(FOX_DIM), row(FOX_DIM),
                 row(LANES), row(LANES), pl.BlockSpec((1, HEAD_ROWS, tm), cumt_map), row(FOX_DIM), row(FOX_DIM))
    return pl.pallas_call(
        functools.partial(_in_even_kernel, tiles_per_seq=tiles_per_seq, transposed_kv=transposed_kv),
        grid=(n_tiles,),
        in_specs=[row(d), _const_spec((1, d)), _const_spec((d, n)), _const_spec((1, LANES)),
                  _const_spec((tm, tm)), _const_spec(perm.shape), _const_spec(const.shape)],
        out_specs=out_specs,
        out_shape=out_shape,
        scratch_shapes=[pltpu.VMEM((1, LANES), F32)],
        compiler_params=_params("arbitrary"),
        name="in_even",
    )(x, g, w, bf, tri, perm, const)


def _pool_kernel(a_ref, hist_ref, w_ref, scale_ref, out_ref, newhist_ref, ext_ref, *, seq_len, start_pos):
    t = seq_len
    base = POOL_HIST + 1
    x0 = a_ref[0]
    ext_ref[0:1, :] = jnp.zeros((1, POOL_DIM), F32)
    ext_ref[1:base, :] = hist_ref[0]
    ext_ref[base:base + t, :] = x0

    def back(kk):
        return ext_ref[base - kk:base - kk + t, :]

    sums = []
    run = x0
    nxt = 1
    for w in POOL_WINDOWS:
        while nxt < w:
            run = run + back(nxt)
            nxt += 1
        sums.append(run)
    grp = lax.broadcasted_iota(jnp.int32, (1, POOL_DIM), 1) // POOL_GROUP_DIM
    sel = sums[-1]
    win = jnp.full((1, POOL_DIM), POOL_WINDOWS[-1], jnp.int32)
    for gi in range(len(POOL_WINDOWS) - 2, -1, -1):
        sel = jnp.where(grp == gi, sums[gi], sel)
        win = jnp.where(grp == gi, POOL_WINDOWS[gi], win)
    pos = start_pos + lax.broadcasted_iota(jnp.int32, (t, 1), 0)
    cnt = jnp.minimum(pos + 1, win).astype(F32)
    dd = sel / cnt - x0
    y = _bdot(dd.astype(BF16), w_ref[...]) * scale_ref[...]
    out_ref[0] = y.astype(out_ref.dtype)
    newhist_ref[0] = ext_ref[t + 1:t + base, :]


def _pool_mix(a, hist, w_bd, scale, start_pos):
    b, t, _ = a.shape
    return pl.pallas_call(
        functools.partial(_pool_kernel, seq_len=t, start_pos=start_pos),
        grid=(b,),
        in_specs=[pl.BlockSpec((1, t, POOL_DIM), lambda i: (i, 0, 0)),
                  pl.BlockSpec((1, POOL_HIST, POOL_DIM), lambda i: (i, 0, 0)),
                  _const_spec((POOL_DIM, POOL_DIM)), _const_spec((1, POOL_DIM))],
        out_specs=(pl.BlockSpec((1, t, POOL_DIM), lambda i: (i, 0, 0)),
                   pl.BlockSpec((1, POOL_HIST, POOL_DIM), lambda i: (i, 0, 0))),
        out_shape=(jax.ShapeDtypeStruct((b, t, POOL_DIM), BF16),
                   jax.ShapeDtypeStruct((b, POOL_HIST, POOL_DIM), F32)),
        scratch_shapes=[pltpu.VMEM((POOL_HIST + 1 + t, POOL_DIM), F32)],
        compiler_params=_params("arbitrary"),
        name="pool_mix",
    )(a, hist, w_bd, scale)


def _fox_prompt_kernel(q_ref, k_ref, v_ref, aq_ref, ak_ref, o_ref, *, tile):
    s_len = q_ref.shape[1]
    lane = lax.broadcasted_iota(jnp.int32, (1, LANES), 1)
    in_head = (lane < FOX_HEAD_DIM, lane >= FOX_HEAD_DIM)
    visible = (lax.broadcasted_iota(jnp.int32, (tile, tile), 1) <= lax.broadcasted_iota(jnp.int32, (tile, tile), 0))
    km = []
    for ki in range(s_len // tile):
        rows = slice(ki * tile, (ki + 1) * tile)
        km.append([jnp.where(in_head[e], k_ref[0, rows, :], ak_ref[0, rows, :]) for e in range(2)])
    for qi in range(s_len // tile):
        qrows = slice(qi * tile, (qi + 1) * tile)
        outs = []
        for e in range(2):
            qm = jnp.where(in_head[e], q_ref[0, qrows, :], aq_ref[0, qrows, :])
            m = jnp.full((tile, 1), NEG_BIG, F32)
            l = jnp.zeros((tile, 1), F32)
            acc = jnp.zeros((tile, LANES), F32)
            for ki in range(qi + 1):
                s = _bdot_nt(qm, km[ki][e])
                if ki == qi:
                    s = jnp.where(visible, s, NEG_BIG)
                m_new = jnp.maximum(m, jnp.max(s, axis=-1, keepdims=True))
                alpha = jnp.exp2(m - m_new)
                p = jnp.exp2(s - m_new)
                l = alpha * l + jnp.sum(p, axis=-1, keepdims=True)
                acc = alpha * acc + _bdot(p.astype(BF16), v_ref[0, ki * tile:(ki + 1) * tile, :])
                m = m_new
            outs.append(acc / l)
        o_ref[0, qrows, :] = jnp.where(in_head[0], outs[0], outs[1]).astype(o_ref.dtype)


def _fox_prompt(q, kb, vb, aq, ak):
    b, s, _ = q.shape
    tile = ATT_TILE if s % ATT_TILE == 0 else s
    assert s // tile <= 4, "the causal tile structure is unrolled in the kernel"
    full = lambda: pl.BlockSpec((1, s, LANES), lambda bi, hp: (bi, 0, hp))
    return pl.pallas_call(
        functools.partial(_fox_prompt_kernel, tile=tile),
        grid=(b, FOX_HEADS // 2),
        in_specs=[full(), full(), full(), full(), full()],
        out_specs=full(),
        out_shape=jax.ShapeDtypeStruct((b, s, FOX_DIM), BF16),
        compiler_params=_params("arbitrary", "arbitrary"),
        name="fox_prompt",
    )(q, kb, vb, aq, ak)


def _suffix_kernel(lf_ref, mat_ref, out_ref):
    for h in range(FOX_HEADS):
        out_ref[:, h, :] = sum(_bdot(p, mat_ref[...]) for p in _split3(lf_ref[h])) * LOG2E


def _page_suffix(lf_hpp):
    heads, n_phys, page = lf_hpp.shape
    tp = SUFFIX_PAGE_TILE if n_phys % SUFFIX_PAGE_TILE == 0 else n_phys
    r = np.arange(page)
    later = r[:, None] > r[None, :]
    mat = jnp.asarray(np.concatenate([later, np.ones((page, page), bool)], axis=1), BF16)
    return pl.pallas_call(
        _suffix_kernel,
        grid=(n_phys // tp,),
        in_specs=[pl.BlockSpec((heads, tp, page), lambda i: (0, i, 0)), _const_spec((page, 2 * page))],
        out_specs=pl.BlockSpec((tp, heads, 2 * page), lambda i: (i, 0, 0)),
        out_shape=jax.ShapeDtypeStruct((n_phys, heads, 2 * page), F32),
        compiler_params=_params("arbitrary"),
        name="page_suffix",
    )(lf_hpp, mat)


K_STREAM, V_STREAM, SUFFIX_STREAM = 0, 1, 2


def _page_copies(pt_ref, hbm, bufs, sems, slot, row, first_page, n_pg, streams):
    copies = []
    for kind in streams:
        for j in range(n_pg):
            pg = 0 if pt_ref is None else pt_ref[row, first_page + j]
            copies.append(pltpu.make_async_copy(hbm[kind].at[pg], bufs[kind].at[slot, j], sems.at[slot, kind]))
    return copies


def _fox_sample_mlp_kernel(pt_ref, q_ref, kn_ref, vn_ref, gq_ref, grow_ref, kt_hbm, vt_hbm, sfx_hbm,
                           h_ref, x1_ref, x2_ref, wo1_ref, wo2_ref, gf_ref, wu_ref, wd_ref, gl_ref,
                           o_ref, y_ref, m_sc, l_sc, acc_sc, run_sc, xn_sc, kbuf, vbuf, sbuf, sems, *,
                           n_pg, n_pages, t_new, steps_per_tile, ff_chunk, final_norm):
    n_seq, steps = pl.num_programs(0), pl.num_programs(1)
    bi, i = pl.program_id(0), pl.program_id(1)
    last = steps - 1
    page = LANES
    g = bi * steps + i
    slot = g % 2
    chunk = g % steps_per_tile
    hbm, bufs = (kt_hbm, vt_hbm, sfx_hbm), (kbuf, vbuf, sbuf)
    first_page = lambda step: n_pages - n_pg * (step + 1)
    issue_order = (K_STREAM, SUFFIX_STREAM, V_STREAM)

    @pl.when(g == 0)
    def _():
        for cp in _page_copies(pt_ref, hbm, bufs, sems, 0, 0, first_page(0), n_pg, issue_order):
            cp.start()

    @pl.when(i == 0)
    def _():
        m_sc[...] = jnp.full_like(m_sc, NEG_BIG)
        l_sc[...] = jnp.zeros_like(l_sc)
        acc_sc[...] = jnp.zeros_like(acc_sc)
        run_sc[...] = jnp.zeros_like(run_sc)

    @pl.when(chunk == 0)
    def _():
        h1 = h_ref[...] + _bdot(x1_ref[...], wo1_ref[...]) + _bdot(x2_ref[...], wo2_ref[...])
        y_ref[...] = h1
        xn_sc[...] = _rms(h1, gf_ref[...]).astype(BF16)

    wrap_i = i == last
    next_i = jnp.where(wrap_i, 0, i + 1)
    next_bi = jnp.where(wrap_i, jnp.where(bi == n_seq - 1, 0, bi + 1), bi)
    for cp in _page_copies(pt_ref, hbm, bufs, sems, 1 - slot, next_bi, first_page(next_i), n_pg, issue_order):
        cp.start()
    ff0 = pl.multiple_of(chunk * ff_chunk, ff_chunk)
    u = jnp.maximum(_bdot(xn_sc[...], wu_ref[:, pl.ds(ff0, ff_chunk)]), 0.0)
    u = (u * u).astype(BF16)
    for cp in _page_copies(None, hbm, bufs, sems, slot, 0, 0, n_pg, (K_STREAM, SUFFIX_STREAM)):
        cp.wait()

    run = run_sc[...]
    bias = [None] * n_pg
    for j in range(n_pg - 1, -1, -1):
        sf = sbuf[slot, j]
        bias[j] = sf[:, :page] + run
        run = run + sf[:, page:]
    run_sc[...] = run

    t = t_new
    q = q_ref[0]
    gq = gq_ref[0]
    s_parts = []
    for h in range(FOX_HEADS):
        hs = slice(h * FOX_HEAD_DIM, (h + 1) * FOX_HEAD_DIM)
        kt = jnp.concatenate([kbuf[slot, j, h] for j in range(n_pg)], axis=1).astype(BF16)
        s_parts.append(_bdot(q[:, hs], kt) + jnp.concatenate([bias[j][h:h + 1, :] for j in range(n_pg)], axis=1))
    s = jnp.concatenate(s_parts, axis=0) + jnp.concatenate([gq] * n_pg, axis=1)
    y_ref[...] += _bdot(u, wd_ref[pl.ds(ff0, ff_chunk), :])
    m_prev = m_sc[...]
    m_new = jnp.maximum(m_prev, jnp.max(s, axis=-1, keepdims=True))
    alpha = jnp.exp2(m_prev - m_new)
    p32 = jnp.exp2(s - jnp.concatenate([m_new] * n_pg, axis=1))
    l_sc[...] = alpha * l_sc[...] + jnp.sum(p32, axis=-1, keepdims=True)
    m_sc[...] = m_new
    for cp in _page_copies(None, hbm, bufs, sems, slot, 0, 0, n_pg, (V_STREAM,)):
        cp.wait()
    pv = []
    for h in range(FOX_HEADS):
        vt = jnp.concatenate([vbuf[slot, j, h] for j in range(n_pg)], axis=1).astype(BF16)
        pv.append(_bdot_nt(p32[h * t:(h + 1) * t].astype(BF16), vt))
    acc_sc[...] = alpha[:, :FOX_HEAD_DIM] * acc_sc[...] + jnp.concatenate(pv, axis=0)

    if final_norm:
        @pl.when(chunk == steps_per_tile - 1)
        def _():
            y_ref[...] = _rms(y_ref[...], gl_ref[...])

    @pl.when(i == last)
    def _():
        kn = kn_ref[0]
        vn = vn_ref[0]
        grow = grow_ref[0]
        r = lax.broadcasted_iota(jnp.int32, (t, t), 0)
        c = lax.broadcasted_iota(jnp.int32, (t, t), 1)
        for h in range(FOX_HEADS):
            hs = slice(h * FOX_HEAD_DIM, (h + 1) * FOX_HEAD_DIM)
            rows = slice(h * t, (h + 1) * t)
            s = _bdot_nt(q[:, hs], kn[:, hs]) + (gq[rows, :1] - grow[h:h + 1, :])
            s = jnp.where(c <= r, s, NEG_BIG)
            m_prev = m_sc[rows, :1]
            m_new = jnp.maximum(m_prev, jnp.max(s, axis=-1, keepdims=True))
            alpha = jnp.exp2(m_prev - m_new)
            p32 = jnp.exp2(s - m_new)
            l = alpha * l_sc[rows, :1] + jnp.sum(p32, axis=-1, keepdims=True)
            acc = alpha * acc_sc[rows, :] + _bdot(p32.astype(BF16), vn[:, hs])
            o_ref[0, :, hs] = (acc / l).astype(o_ref.dtype)

    @pl.when(g == n_seq * steps - 1)
    def _():
        for cp in _page_copies(None, hbm, bufs, sems, 1 - slot, 0, 0, n_pg, issue_order):
            cp.wait()


def _fox_sample_mlp(page_table, q, kn, vn, gq, grow, cache_kt, cache_vt, suffix,
                    h, x1, x2, wo1, wo2, gf, wu, wd, gl, final_norm):
    b, t, _ = q.shape
    n_pages = page_table.shape[1]
    n_pg = PAGES_PER_STEP if n_pages % PAGES_PER_STEP == 0 else 1
    steps = n_pages // n_pg
    page = cache_kt.shape[-1]
    assert page == LANES
    rows, d = h.shape
    d_ff = wu.shape[1]
    tm = FUSED_ROW_TILE if rows % FUSED_ROW_TILE == 0 else rows
    tiles = rows // tm
    assert (b * steps) % tiles == 0, "grid steps must split evenly over the MLP row tiles"
    spt = (b * steps) // tiles
    assert d_ff % spt == 0 and (d_ff // spt) % LANES == 0
    ck = d_ff // spt
    row = lambda width: pl.BlockSpec((tm, width), lambda bi, i, pt: ((bi * steps + i) // spt, 0))

    tok = lambda width: pl.BlockSpec((1, t, width), lambda bi, i, pt: (bi, 0, 0))
    in_hbm = lambda: pl.BlockSpec(memory_space=pl.ANY)
    grid_spec = pltpu.PrefetchScalarGridSpec(
        num_scalar_prefetch=1,
        grid=(b, steps),
        in_specs=[tok(FOX_DIM), tok(FOX_DIM), tok(FOX_DIM),
                  pl.BlockSpec((1, FOX_HEADS * t, LANES), lambda bi, i, pt: (bi, 0, 0)),
                  pl.BlockSpec((1, HEAD_ROWS, t), lambda bi, i, pt: (bi, 0, 0)),
                  in_hbm(), in_hbm(), in_hbm(),
                  row(d), row(x1.shape[1]), row(x2.shape[1]), _const_spec(wo1.shape), _const_spec(wo2.shape),
                  _const_spec((1, d)), _const_spec(wu.shape), _const_spec(wd.shape), _const_spec((1, d))],
        out_specs=(tok(FOX_DIM), row(d)),
        scratch_shapes=[pltpu.VMEM((FOX_HEADS * t, LANES), F32), pltpu.VMEM((FOX_HEADS * t, LANES), F32),
                        pltpu.VMEM((FOX_HEADS * t, FOX_HEAD_DIM), F32), pltpu.VMEM((FOX_HEADS, LANES), F32),
                        pltpu.VMEM((tm, d), BF16),
                        pltpu.VMEM((2, n_pg, FOX_HEADS, FOX_HEAD_DIM, page), F32),
                        pltpu.VMEM((2, n_pg, FOX_HEADS, FOX_HEAD_DIM, page), F32),
                        pltpu.VMEM((2, n_pg, FOX_HEADS, 2 * page), F32),
                        pltpu.SemaphoreType.DMA((2, 3))],
    )
    return pl.pallas_call(
        functools.partial(_fox_sample_mlp_kernel, n_pg=n_pg, n_pages=n_pages, t_new=t, steps_per_tile=spt,
                          ff_chunk=ck, final_norm=final_norm),
        grid_spec=grid_spec,
        out_shape=(jax.ShapeDtypeStruct((b, t, FOX_DIM), BF16), jax.ShapeDtypeStruct((rows, d), F32)),
        compiler_params=_params("arbitrary", "arbitrary"),
        name="fox_sample_mlp",
    )(page_table, q, kn, vn, gq, grow, cache_kt, cache_vt, suffix, h, x1, x2, wo1, wo2, gf, wu, wd, gl)


def _mix_mlp_kernel(h_ref, x1_ref, x2_ref, wo1_ref, wo2_ref, gf_ref, wu_ref, wd_ref, gl_ref, o_ref, *,
                    ff_chunk, final_norm):
    h1 = h_ref[...] + _bdot(x1_ref[...], wo1_ref[...]) + _bdot(x2_ref[...], wo2_ref[...])
    xn = _rms(h1, gf_ref[...]).astype(BF16)
    o_ref[...] = h1
    d_ff = wu_ref.shape[1]
    for c in range(d_ff // ff_chunk):
        u = jnp.maximum(_bdot(xn, wu_ref[:, c * ff_chunk:(c + 1) * ff_chunk]), 0.0)
        o_ref[...] += _bdot((u * u).astype(BF16), wd_ref[c * ff_chunk:(c + 1) * ff_chunk, :])
    if final_norm:
        o_ref[...] = _rms(o_ref[...], gl_ref[...])


def _mix_mlp(h, x1, x2, wo1, wo2, gf, wu, wd, gl, final_norm):
    rows, d = h.shape
    tm = _row_tile(rows)
    d_ff = wu.shape[1]
    row = lambda width: pl.BlockSpec((tm, width), lambda i: (i, 0))
    return pl.pallas_call(
        functools.partial(_mix_mlp_kernel, ff_chunk=min(d_ff, 1024), final_norm=final_norm),
        grid=(rows // tm,),
        in_specs=[row(d), row(x1.shape[1]), row(x2.shape[1]), _const_spec(wo1.shape), _const_spec(wo2.shape),
                  _const_spec((1, d)), _const_spec(wu.shape), _const_spec(wd.shape), _const_spec((1, d))],
        out_specs=row(d),
        out_shape=jax.ShapeDtypeStruct((rows, d), F32),
        compiler_params=_params("arbitrary"),
        name="mix_mlp",
    )(h, x1, x2, wo1, wo2, gf, wu, wd, gl)


def _in_odd_kernel(x_ref, g_ref, w_ref, lng_ref, lnb_ref, cos_ref, sin_ref, ws_ref, bs_ref,
                   c_ref, vn_ref, q_ref, k_ref, rv_ref, rg_ref, *, chunk):
    xn = _rms(x_ref[...], g_ref[...]).astype(BF16)
    proj = _bdot(xn, w_ref[...])
    tm = proj.shape[0]
    gd, rd = GMLP_DIM, RET_DIM
    gv = proj[:, gd:2 * gd]
    mu = jnp.mean(gv, axis=-1, keepdims=True)
    var = jnp.mean((gv - mu) ** 2, axis=-1, keepdims=True)
    vn = (gv - mu) * lax.rsqrt(var + EPS) * lng_ref[...] + lnb_ref[...]
    vn_ref[...] = vn
    vnb = vn.astype(BF16)
    wrows = ws_ref.shape[1]
    r = lax.broadcasted_iota(jnp.int32, (wrows, wrows), 0)
    c = lax.broadcasted_iota(jnp.int32, (wrows, wrows), 1)
    causal_in_chunk = (c <= r) & (c >= (r // chunk) * chunk)
    for gi in range(GMLP_GROUPS):
        sl = slice(gi * LANES, (gi + 1) * LANES)
        ws = jnp.where(causal_in_chunk, ws_ref[gi], 0.0).astype(BF16)
        for ti in range(tm // wrows):
            rows = slice(ti * wrows, (ti + 1) * wrows)
            z = _bdot(ws, vnb[rows, sl]) + bs_ref[gi]
            c_ref[rows, sl] = (proj[rows, sl] * z).astype(c_ref.dtype)
    cos = cos_ref[...]
    sin = sin_ref[...]
    o_q, o_k, o_v, o_g = 2 * gd, 2 * gd + rd, 2 * gd + 2 * rd, 2 * gd + 3 * rd
    for hh in range(RET_HEADS):
        sl = slice(hh * RET_HEAD_DIM, (hh + 1) * RET_HEAD_DIM)
        qh = proj[:, o_q + hh * RET_HEAD_DIM:o_q + (hh + 1) * RET_HEAD_DIM]
        kh = proj[:, o_k + hh * RET_HEAD_DIM:o_k + (hh + 1) * RET_HEAD_DIM]
        q_ref[:, sl] = (qh * cos + pltpu.roll(qh, RET_HEAD_DIM // 2, 1) * sin).astype(BF16)
        k_ref[:, sl] = (kh * cos + pltpu.roll(kh, RET_HEAD_DIM // 2, 1) * sin) * (RET_HEAD_DIM ** -0.5)
    rv_ref[...] = proj[:, o_v:o_g].astype(BF16)
    rg_ref[...] = proj[:, o_g:]


def _in_odd(x, g, w, lng, lnb, cos_t, sin_t, gmlp_ws, gmlp_bs, seq_len):
    rows, d = x.shape
    tm = _row_tile(rows)
    chunk = min(seq_len, GMLP_CHUNK)
    assert seq_len % chunk == 0 and tm % chunk == 0
    wrows = chunk if chunk % LANES == 0 else tm
    reps = wrows // chunk
    ws_t = jnp.tile(gmlp_ws[:, :chunk, :chunk], (1, reps, reps))
    bs_t = jnp.broadcast_to(jnp.tile(gmlp_bs[:, :chunk], (1, reps))[:, :, None], (GMLP_GROUPS, wrows, LANES))
    if seq_len >= tm:
        tiles_per_seq = seq_len // tm
        rope_map = lambda i: (i % tiles_per_seq, 0)
    else:
        reps = tm // seq_len
        cos_t = jnp.tile(cos_t, (reps, 1))
        sin_t = jnp.tile(sin_t, (reps, 1))
        rope_map = lambda i: (0, 0)
    row = lambda width: pl.BlockSpec((tm, width), lambda i: (i, 0))
    outs = ((GMLP_DIM, BF16), (GMLP_DIM, F32), (RET_DIM, BF16), (RET_DIM, F32), (RET_DIM, BF16), (RET_DIM, F32))
    return pl.pallas_call(
        functools.partial(_in_odd_kernel, chunk=chunk),
        grid=(rows // tm,),
        in_specs=[row(d), _const_spec((1, d)), _const_spec(w.shape), _const_spec((1, GMLP_DIM)),
                  _const_spec((1, GMLP_DIM)), pl.BlockSpec((tm, RET_HEAD_DIM), rope_map),
                  pl.BlockSpec((tm, RET_HEAD_DIM), rope_map), _const_spec(ws_t.shape), _const_spec(bs_t.shape)],
        out_specs=tuple(row(wd) for wd, _ in outs),
        out_shape=tuple(jax.ShapeDtypeStruct((rows, wd), dt) for wd, dt in outs),
        compiler_params=_params("arbitrary"),
        name="in_odd",
    )(x, g, w, lng, lnb, cos_t, sin_t, ws_t, bs_t)


def _ret_kernel(q_ref, k_ref, v_ref, g_ref, s0_ref, dintra_ref, dq_ref, dk_ref, ds_ref, gn_ref,
                o_ref, sout_ref, st_sc, *, batch_block):
    c = pl.program_id(1)

    @pl.when(c == 0)
    def _():
        st_sc[...] = s0_ref[...]

    for bi in range(batch_block):
        for hh in range(RET_HEADS):
            sl = slice(hh * RET_HEAD_DIM, (hh + 1) * RET_HEAD_DIM)
            q = q_ref[bi, :, sl]
            kf = k_ref[bi, :, sl]
            v = v_ref[bi, :, sl]
            st = st_sc[bi, hh]
            sc = _bdot_nt(q, kf.astype(BF16)) * dintra_ref[hh]
            o = _bdot(sc.astype(BF16), v) + _bdot(q, st.astype(BF16)) * dq_ref[hh]
            st_sc[bi, hh] = st * ds_ref[hh] + _bdot_tn((kf * dk_ref[hh]).astype(BF16), v)
            mu = jnp.mean(o, axis=-1, keepdims=True)
            var = jnp.mean((o - mu) ** 2, axis=-1, keepdims=True)
            on = (o - mu) * lax.rsqrt(var + EPS) * gn_ref[:, sl]
            gate = g_ref[bi, :, sl]
            o_ref[bi, :, sl] = (on * (gate * jax.nn.sigmoid(gate))).astype(o_ref.dtype)

    @pl.when(c == pl.num_programs(1) - 1)
    def _():
        sout_ref[...] = st_sc[...]


def _retention(q, k, v, gate, s0, gn):
    b, t, _ = q.shape
    bb = RET_BATCH_BLOCK if b % RET_BATCH_BLOCK == 0 else 1
    lc = math.gcd(t, RET_CHUNK) if t >= RET_CHUNK else t
    log_g = jnp.log1p(-jnp.exp2(-5.0 - jnp.arange(RET_HEADS, dtype=F32)))
    idx = jnp.arange(lc, dtype=F32)
    diff = idx[:, None] - idx[None, :]
    dintra = jnp.where(diff[None] >= 0, jnp.exp(diff[None] * log_g[:, None, None]), 0.0)
    lanes = (RET_HEADS, lc, RET_HEAD_DIM)
    dq = jnp.broadcast_to(jnp.exp((idx[None, :] + 1.0) * log_g[:, None])[:, :, None], lanes)
    dk = jnp.broadcast_to(jnp.exp((lc - 1.0 - idx)[None, :] * log_g[:, None])[:, :, None], lanes)
    ds = jnp.broadcast_to(jnp.exp(lc * log_g)[:, None, None], (RET_HEADS, 1, RET_HEAD_DIM))
    blk = lambda: pl.BlockSpec((bb, lc, RET_DIM), lambda bi, ci: (bi, ci, 0))
    st = lambda: pl.BlockSpec((bb, RET_HEADS, RET_HEAD_DIM, RET_HEAD_DIM), lambda bi, ci: (bi, 0, 0, 0))
    return pl.pallas_call(
        functools.partial(_ret_kernel, batch_block=bb),
        grid=(b // bb, t // lc),
        in_specs=[blk(), blk(), blk(), blk(), st(), _const_spec(dintra.shape), _const_spec(lanes),
                  _const_spec(lanes), _const_spec(ds.shape), _const_spec((1, RET_DIM))],
        out_specs=(blk(), st()),
        out_shape=(jax.ShapeDtypeStruct((b, t, RET_DIM), BF16),
                   jax.ShapeDtypeStruct((b, RET_HEADS, RET_HEAD_DIM, RET_HEAD_DIM), F32)),
        scratch_shapes=[pltpu.VMEM((bb, RET_HEADS, RET_HEAD_DIM, RET_HEAD_DIM), F32)],
        compiler_params=_params("arbitrary", "arbitrary"),
        name="retention",
    )(q, k, v, gate, s0, dintra, dq, dk, ds, gn)


def _rope_tables(start_pos, t):
    half = RET_HEAD_DIM // 2
    pos = (start_pos + jnp.arange(t)).astype(F32)
    inv = ROPE_BASE ** (-jnp.arange(half, dtype=F32) / half)
    ang = pos[:, None] * inv[None, :]
    cos, sin = jnp.cos(ang), jnp.sin(ang)
    return jnp.concatenate([cos, cos], axis=1), jnp.concatenate([-sin, sin], axis=1)


def _block_diag(w):
    g, n, _ = w.shape
    out = jnp.zeros((g * n, g * n), w.dtype)
    for i in range(g):
        out = out.at[i * n:(i + 1) * n, i * n:(i + 1) * n].set(w[i])
    return out


def _even_front(x, start_pos, hist0, transposed_kv, wts):
    b, t, d = x.shape
    e = dict(zip(("a", "q", "k", "v", "kb", "vb", "lf", "cum", "cumt", "aq", "ak"),
                 _in_even(x.reshape(b * t, d), wts["norm_mix"][0].reshape(1, -1), wts["w_in_even"], wts["b_forget"],
                          t, transposed_kv)))
    e["pool_out"], e["new_hist"] = _pool_mix(e["a"].reshape(b, t, POOL_DIM), hist0, wts["pool_w"], wts["pool_scale"],
                                             start_pos)
    return e


def _odd_front(h, b, t, start_pos, state0, wts):
    row1 = lambda a: a.reshape(1, -1)
    r3 = lambda z: z.reshape(b, t, z.shape[-1])
    cos_t, sin_t = _rope_tables(start_pos, t)
    c_out, vn, rq, rk, rv, rg = _in_odd(h, row1(wts["norm_mix"][1]), wts["w_in_odd"], row1(wts["gmlp_ln_g"]),
                                        row1(wts["gmlp_ln_b"]), cos_t, sin_t, wts["gmlp_ws"], wts["gmlp_bs"], t)
    r_out, s_new = _retention(r3(rq), r3(rk), r3(rv), r3(rg), state0, row1(wts["ret_gn_g"]))
    return c_out, r_out.reshape(b * t, RET_DIM), vn, s_new


def _mlp_args(layer, wts):
    w_out, split = (wts["w_out_even"], POOL_DIM) if layer == 0 else (wts["w_out_odd"], GMLP_DIM)
    return (w_out[:split], w_out[split:], wts["norm_ffn"][layer].reshape(1, -1), wts["w_up"][layer],
            wts["w_down"][layer], wts["norm_final"].reshape(1, -1))


def kernel(x_prompt, x_sample, state_pool, cache_k, cache_v, cache_logf, page_table, state_ret, norm_mix, w_in_even, b_forget, pool_w, pool_scale, w_out_even, w_in_odd, gmlp_ln_g, gmlp_ln_b, gmlp_ws, gmlp_bs, ret_gn_g, w_out_odd, norm_ffn, w_up, w_down, norm_final):
    assert norm_mix.shape[0] == 2 and w_in_even.shape[0] == 1 and w_in_odd.shape[0] == 1
    d = x_prompt.shape[-1]
    ev_cols = w_in_even.shape[-1]
    ev_pad = POOL_DIM + 3 * FOX_DIM + LANES - ev_cols
    wts = {
        "norm_mix": norm_mix, "norm_ffn": norm_ffn, "norm_final": norm_final,
        "w_in_even": jnp.pad(w_in_even[0], ((0, 0), (0, ev_pad))).astype(BF16),
        "b_forget": jnp.pad(b_forget[0], (0, LANES - FOX_HEADS)).reshape(1, LANES),
        "pool_w": _block_diag(pool_w[0]).astype(BF16),
        "pool_scale": pool_scale[0].reshape(1, POOL_DIM),
        "w_out_even": w_out_even[0].astype(BF16),
        "w_in_odd": w_in_odd[0].astype(BF16),
        "gmlp_ln_g": gmlp_ln_g[0], "gmlp_ln_b": gmlp_ln_b[0], "gmlp_ws": gmlp_ws[0], "gmlp_bs": gmlp_bs[0],
        "ret_gn_g": ret_gn_g[0],
        "w_out_odd": w_out_odd[0].astype(BF16),
        "w_up": w_up.astype(BF16), "w_down": w_down.astype(BF16),
    }
    b, s, _ = x_prompt.shape
    db, t, _ = x_sample.shape
    n_pages = page_table.shape[1]
    page = cache_k.shape[2]
    past = n_pages * page

    cache_kt = jnp.transpose(cache_k[0], (0, 2, 3, 1))
    cache_vt = jnp.transpose(cache_v[0], (0, 2, 3, 1))
    suffix = _page_suffix(jnp.transpose(cache_logf[0], (2, 0, 1)))

    se = _even_front(x_sample, past, state_pool[0], False, wts)
    pe = _even_front(x_prompt, 0, jnp.zeros((b, POOL_HIST, POOL_DIM), F32), True, wts)
    p3 = lambda z: z.reshape(b, s, z.shape[-1])
    att_p = _fox_prompt(p3(pe["q"]), p3(pe["kb"]), p3(pe["vb"]), p3(pe["aq"]), p3(pe["ak"]))

    assert db % 2 == 0
    half = db // 2
    s3 = lambda z: z.reshape(db, t, z.shape[-1])
    grow = jnp.transpose(s3(se["cum"]), (0, 2, 1))[:, :HEAD_ROWS, :]
    gq = jnp.broadcast_to(grow[:, :FOX_HEADS, :, None], (db, FOX_HEADS, t, LANES)).reshape(db, FOX_HEADS * t, LANES)
    sq, skb, svb = s3(se["q"]), s3(se["kb"]), s3(se["vb"])

    def attend_half(sel, h, x1, x2, layer):
        return _fox_sample_mlp(page_table[sel], sq[sel], skb[sel], svb[sel], gq[sel], grow[sel], cache_kt, cache_vt,
                               suffix, h, x1, x2, *_mlp_args(layer, wts), layer == 1)

    att_s0, h_p = attend_half(slice(0, half), x_prompt.reshape(b * s, d), pe["pool_out"].reshape(b * s, POOL_DIM),
                              att_p.reshape(b * s, FOX_DIM), 0)
    c_p, r_p, _, ret_p = _odd_front(h_p, b, s, 0, jnp.zeros((b, RET_HEADS, RET_HEAD_DIM, RET_HEAD_DIM), F32), wts)
    att_s1, y_p = attend_half(slice(half, db), h_p, c_p, r_p, 1)

    att_s = jnp.concatenate([att_s0, att_s1], axis=0)
    h_s = _mix_mlp(x_sample.reshape(db * t, d), se["pool_out"].reshape(db * t, POOL_DIM),
                   att_s.reshape(db * t, FOX_DIM), *_mlp_args(0, wts), False)
    c_s, r_s, gv_s, ret_s = _odd_front(h_s, db, t, past, state_ret[0], wts)
    y_s = _mix_mlp(h_s, c_s, r_s, *_mlp_args(1, wts), True)

    kv_p = lambda z: jnp.transpose(z.reshape(b, FOX_HEADS, FOX_HEAD_DIM, s), (0, 3, 1, 2))[None]
    kv_s = lambda z: z.reshape(1, db, t, FOX_HEADS, FOX_HEAD_DIM)
    lf_out = lambda e, n, m: e["lf"][:, :FOX_HEADS].reshape(1, n, m, FOX_HEADS)
    return (y_p.reshape(b, s, d), y_s.reshape(db, t, d), pe["new_hist"][None], kv_p(pe["k"]), kv_p(pe["v"]),
            lf_out(pe, b, s), ret_p[None], se["new_hist"][None], kv_s(se["k"]), kv_s(se["v"]), lf_out(se, db, t),
            gv_s.reshape(1, db, t, GMLP_DIM), ret_s[None])
```
